```python
import math, functools
import jax, jax.numpy as jnp
from jax import lax
import numpy as np

D_MODEL = 1024
BATCH = 2
SEQ = 8192
DEPTH = 2
DEC_BATCH = 32
DEC_SEQ = 8
PAST_LEN = 16384
PAGE_SIZE = 128

W_A = D_MODEL // 2
N_GROUPS_A = 8
CONV_A = 3
N_HEADS_B = 8
HEAD_DIM_B = 64
W_B = N_HEADS_B * HEAD_DIM_B
Q_BLOCK = 128
SB_BIAS_NORM = 8.0
N_HEADS_C = 8
DK_C = 128
DV_C = 128
W_QK_C = N_HEADS_C * DK_C
W_V_C = N_HEADS_C * DV_C
W_CONV_C = 2 * W_QK_C + W_V_C
CONV_C = 4
CHUNK = 64
N_BRANCH = 3
IN_SIZES = (W_A, W_A, W_A, W_B, W_B, W_B, W_CONV_C, W_V_C, N_HEADS_C, N_HEADS_C, N_BRANCH * D_MODEL)
N_IN = 3 * W_A + 3 * W_B + W_CONV_C + W_V_C + 2 * N_HEADS_C + N_BRANCH * D_MODEL
Q_B_OFF = 3 * W_A
K_B_OFF = 3 * W_A + W_B
D_FF = 2816
N_EXPERTS = 8
TOP_K = 2
D_FF_EXPERT = 3584
N_DENSE = (DEPTH + 1) // 2
N_MOE = DEPTH // 2
RMS_EPS = 1e-6
L2_EPS = 1e-6

kernel_name = 'hybrid_gated_conv_stickbreak_deltanet_decoder_step'


def rmsnorm(x, g):
    xf = x.astype(jnp.float32)
    y = xf * lax.rsqrt(jnp.mean(xf * xf, axis=-1, keepdims=True) + RMS_EPS)
    return (y * g.astype(jnp.float32)).astype(x.dtype)


def l2norm(x):
    return x * lax.rsqrt(jnp.sum(x * x, axis=-1, keepdims=True) + L2_EPS)


def short_conv(x, buf, w):
    width = w.shape[0]
    t = x.shape[1]
    xp = jnp.concatenate([buf.astype(x.dtype), x], axis=1)
    y = xp[:, 0:t] * w[0]
    for i in range(1, width):
        y = y + xp[:, i:i + t] * w[i]
    return y, xp[:, t:]


def stick_breaking_block(q, k, v, q_pos0):
    scale = q.shape[-1] ** -0.5
    z = jnp.einsum('bqhd,bkhd->bhqk', q, k).astype(jnp.float32) * scale
    q_pos = q_pos0 + jnp.arange(q.shape[1])
    k_pos = jnp.arange(k.shape[1])
    earlier = k_pos[None, :] < q_pos[:, None]
    log_stay = jnp.where(earlier, jax.nn.log_sigmoid(-z), 0.0)
    log_after = lax.cumsum(log_stay, axis=3, reverse=True) - log_stay
    a = jnp.where(earlier, jnp.exp(jax.nn.log_sigmoid(z) + log_after), 0.0)
    return jnp.einsum('bhqk,bkhd->bqhd', a.astype(v.dtype), v)


def stick_breaking_prompt(q, k, v):
    b, t, h, d = q.shape
    nb = t // Q_BLOCK
    qb = jnp.moveaxis(q.reshape(b, nb, Q_BLOCK, h, d), 1, 0)
    ob = lax.map(lambda args: stick_breaking_block(args[0], k, v, args[1] * Q_BLOCK), (qb, jnp.arange(nb)))
    return jnp.moveaxis(ob, 0, 1).reshape(b, t, h, d)


def stick_breaking_sample(q, k, v, cache_k, cache_v, page_table):
    db = q.shape[0]
    kp = cache_k[page_table].reshape(db, -1, N_HEADS_B, HEAD_DIM_B).astype(k.dtype)
    vp = cache_v[page_table].reshape(db, -1, N_HEADS_B, HEAD_DIM_B).astype(v.dtype)
    past = kp.shape[1]
    return stick_breaking_block(q, jnp.concatenate([kp, k], axis=1), jnp.concatenate([vp, v], axis=1), past)


def gated_delta_rule(q, k, v, g, beta, s0):
    b, t, h, dk = q.shape
    dv = v.shape[-1]
    n = -(-t // CHUNK)
    pad = n * CHUNK - t

    def blocks(a):
        a = jnp.pad(a, [(0, 0), (0, pad)] + [(0, 0)] * (a.ndim - 2))
        a = a.reshape((b, n, CHUNK) + a.shape[2:])
        return jnp.moveaxis(jnp.moveaxis(a, 3, 2), 1, 0)

    qc, kc, vc, gc, bc = blocks(q), blocks(k), blocks(v), blocks(g), blocks(beta)
    gcum = jnp.cumsum(gc, axis=-1)
    idx = jnp.arange(CHUNK)
    strict = idx[:, None] > idx[None, :]
    incl = idx[:, None] >= idx[None, :]
    diff = gcum[..., :, None] - gcum[..., None, :]
    dec_strict = jnp.exp(jnp.where(strict, diff, -jnp.inf))
    dec_incl = jnp.exp(jnp.where(incl, diff, -jnp.inf))
    m = bc[..., None] * jnp.einsum('...id,...jd->...ij', kc, kc) * dec_strict
    eye = jnp.eye(CHUNK, dtype=m.dtype)
    rhs = jnp.concatenate([kc * (bc * jnp.exp(gcum))[..., None], vc * bc[..., None]], axis=-1)
    sol = lax.linalg.triangular_solve(eye + m, rhs, left_side=True, lower=True, unit_diagonal=True)
    w, uv = sol[..., :dk], sol[..., dk:]
    aqk = jnp.einsum('...id,...jd->...ij', qc, kc) * dec_incl
    q_dec = qc * jnp.exp(gcum)[..., None]
    k_dec = kc * jnp.exp(gcum[..., -1:] - gcum)[..., None]
    g_end = jnp.exp(gcum[..., -1])

    def step(s, xs):
        q_dec_c, w_c, uv_c, aqk_c, k_dec_c, g_end_c = xs
        u = uv_c - jnp.einsum('bhck,bhkv->bhcv', w_c, s)
        o = jnp.einsum('bhck,bhkv->bhcv', q_dec_c, s) + jnp.einsum('bhij,bhjv->bhiv', aqk_c, u)
        s = g_end_c[..., None, None] * s + jnp.einsum('bhck,bhcv->bhkv', k_dec_c, u)
        return s, o

    s_final, o = lax.scan(step, s0, (q_dec, w, uv, aqk, k_dec, g_end))
    o = jnp.moveaxis(jnp.moveaxis(o, 0, 1), 2, 3).reshape(b, n * CHUNK, h, dv)[:, :t]
    return o, s_final


def token_mixers(hn, attend, buf_a, buf_c, s0, w_in, b_in, conv_a_w, conv_c_w, a_log, dt_bias, norm_c,
                 w_br_a, w_br_b, w_br_c, w_out):
    b, t, _ = hn.shape
    u = hn @ w_in + b_in
    splits = np.cumsum(IN_SIZES)[:-1].tolist()
    h_a, gb_a, gc_a, q_b, k_b, v_b, qkv_c, z_c, b_c, a_c, gates = jnp.split(u, splits, axis=-1)
    conv_a, new_buf_a = short_conv(gc_a * h_a, buf_a, conv_a_w)
    y_a = (gb_a * conv_a) @ w_br_a
    q_b = q_b.reshape(b, t, N_HEADS_B, HEAD_DIM_B)
    k_b = k_b.reshape(b, t, N_HEADS_B, HEAD_DIM_B)
    v_b = v_b.reshape(b, t, N_HEADS_B, HEAD_DIM_B)
    o_b = attend(q_b, k_b, v_b)
    y_b = o_b.reshape(b, t, W_B) @ w_br_b
    conv_c, new_buf_c = short_conv(qkv_c, buf_c, conv_c_w)
    conv_c = jax.nn.silu(conv_c).astype(jnp.float32)
    q_c, k_c, v_c = jnp.split(conv_c, [W_QK_C, 2 * W_QK_C], axis=-1)
    q_c = l2norm(q_c.reshape(b, t, N_HEADS_C, DK_C)) * (DK_C ** -0.5)
    k_c = l2norm(k_c.reshape(b, t, N_HEADS_C, DK_C))
    v_c = v_c.reshape(b, t, N_HEADS_C, DV_C)
    g = -jnp.exp(a_log.astype(jnp.float32)) * jax.nn.softplus(a_c.astype(jnp.float32) + dt_bias.astype(jnp.float32))
    beta = jax.nn.sigmoid(b_c.astype(jnp.float32))
    o_c, s_new = gated_delta_rule(q_c, k_c, v_c, g, beta, s0.astype(jnp.float32))
    o_c = (o_c * lax.rsqrt(jnp.mean(o_c * o_c, axis=-1, keepdims=True) + RMS_EPS) * norm_c.astype(jnp.float32)
           * jax.nn.silu(z_c.astype(jnp.float32)).reshape(b, t, N_HEADS_C, DV_C))
    y_c = o_c.reshape(b, t, W_V_C).astype(hn.dtype) @ w_br_c
    gt = jax.nn.sigmoid(gates).reshape(b, t, N_BRANCH, D_MODEL)
    merged = gt[:, :, 0] * y_a + gt[:, :, 1] * y_b + gt[:, :, 2] * y_c
    return merged @ w_out, (k_b, v_b, new_buf_a, new_buf_c, s_new)


def swiglu(x, wg, wu, wd):
    return (jax.nn.silu(x @ wg) * (x @ wu)) @ wd


def moe_swiglu(x, w_router, wg, wu, wd):
    b, t, d = x.shape
    xf = x.reshape(b * t, d)
    logits = (xf @ w_router).astype(jnp.float32)
    top_v, top_i = lax.top_k(logits, TOP_K)
    top_w = jax.nn.softmax(top_v, axis=-1)
    gate = jnp.sum(jax.nn.one_hot(top_i, N_EXPERTS, dtype=jnp.float32) * top_w[..., None], axis=1)
    out = jnp.zeros_like(xf)
    for e in range(N_EXPERTS):
        out = out + gate[:, e:e + 1].astype(x.dtype) * swiglu(xf, wg[e], wu[e], wd[e])
    return out.reshape(b, t, d)


def trunk(x, attends, bufs_a, bufs_c, states, norm_mix, w_in, b_in, conv_a_w, conv_c_w, a_log, dt_bias, norm_c,
          w_br_a, w_br_b, w_br_c, w_out, norm_ffn, w_ffn_gate, w_ffn_up, w_ffn_down,
          w_router, w_exp_gate, w_exp_up, w_exp_down, norm_final):
    ks, vs, bas, bcs, ss = [], [], [], [], []
    for l in range(DEPTH):
        mix, (k_l, v_l, ba_l, bc_l, s_l) = token_mixers(
            rmsnorm(x, norm_mix[l]), attends[l], bufs_a[l], bufs_c[l], states[l], w_in[l], b_in[l], conv_a_w[l],
            conv_c_w[l], a_log[l], dt_bias[l], norm_c[l], w_br_a[l], w_br_b[l], w_br_c[l], w_out[l])
        x = x + mix
        hn = rmsnorm(x, norm_ffn[l])
        if l % 2 == 0:
            x = x + swiglu(hn, w_ffn_gate[l // 2], w_ffn_up[l // 2], w_ffn_down[l // 2])
        else:
            x = x + moe_swiglu(hn, w_router[l // 2], w_exp_gate[l // 2], w_exp_up[l // 2], w_exp_down[l // 2])
        ks.append(k_l); vs.append(v_l); bas.append(ba_l); bcs.append(bc_l); ss.append(s_l)
    return rmsnorm(x, norm_final), jnp.stack(ks), jnp.stack(vs), jnp.stack(bas), jnp.stack(bcs), jnp.stack(ss)


def setup_inputs(seed: int = 0) -> dict:
    key = jax.random.key(seed)
    ks = jax.random.split(key, 32)
    n_pages = PAST_LEN // PAGE_SIZE
    n_used = DEC_BATCH * n_pages
    n_phys = n_used + n_used // 4

    def nrm(k, shape, scale):
        return jax.random.normal(k, shape, jnp.float32) * scale

    def gain(k, shape):
        return 1.0 + 0.01 * jax.random.normal(k, shape, jnp.float32)

    page_table = jax.random.permutation(ks[3], n_phys)[:n_used].reshape(DEC_BATCH, n_pages).astype(jnp.int32)
    dt = jnp.exp(jax.random.uniform(ks[12], (DEPTH, N_HEADS_C), jnp.float32, math.log(1e-3), math.log(1e-1)))
    sb_dir = l2norm(nrm(ks[28], (DEPTH, N_HEADS_B, HEAD_DIM_B), 1.0)).reshape(DEPTH, W_B) * SB_BIAS_NORM
    b_in = nrm(ks[29], (DEPTH, N_IN), 0.02)
    b_in = b_in.at[:, Q_B_OFF:Q_B_OFF + W_B].add(sb_dir).at[:, K_B_OFF:K_B_OFF + W_B].add(-sb_dir)
    k_offset = b_in[:, K_B_OFF:K_B_OFF + W_B].reshape(DEPTH, 1, 1, N_HEADS_B, HEAD_DIM_B)
    return {
        'x_prompt': nrm(ks[0], (BATCH, SEQ, D_MODEL), 1.0),
        'x_sample': nrm(ks[1], (DEC_BATCH, DEC_SEQ, D_MODEL), 1.0),
        'cache_k': nrm(ks[2], (DEPTH, n_phys, PAGE_SIZE, N_HEADS_B, HEAD_DIM_B), 1.0) + k_offset,
        'cache_v': nrm(ks[4], (DEPTH, n_phys, PAGE_SIZE, N_HEADS_B, HEAD_DIM_B), 1.0),
        'page_table': page_table,
        'state_conv_a': nrm(ks[5], (DEPTH, DEC_BATCH, CONV_A - 1, W_A), 1.0),
        'state_conv_c': nrm(ks[6], (DEPTH, DEC_BATCH, CONV_C - 1, W_CONV_C), 1.0),
        'state_delta': nrm(ks[7], (DEPTH, DEC_BATCH, N_HEADS_C, DK_C, DV_C), 0.1),
        'norm_mix': gain(ks[8], (DEPTH, D_MODEL)),
        'w_in': nrm(ks[9], (DEPTH, D_MODEL, N_IN), D_MODEL ** -0.5),
        'b_in': b_in,
        'conv_a_w': nrm(ks[10], (DEPTH, CONV_A, W_A), CONV_A ** -0.5),
        'conv_c_w': nrm(ks[11], (DEPTH, CONV_C, W_CONV_C), CONV_C ** -0.5),
        'a_log': jnp.log(jax.random.uniform(ks[13], (DEPTH, N_HEADS_C), jnp.float32, 1.0, 16.0)),
        'dt_bias': dt + jnp.log(-jnp.expm1(-dt)),
        'norm_c': gain(ks[14], (DEPTH, DV_C)),
        'w_br_a': nrm(ks[15], (DEPTH, W_A, D_MODEL), W_A ** -0.5),
        'w_br_b': nrm(ks[16], (DEPTH, W_B, D_MODEL), W_B ** -0.5),
        'w_br_c': nrm(ks[17], (DEPTH, W_V_C, D_MODEL), W_V_C ** -0.5),
        'w_out': nrm(ks[18], (DEPTH, D_MODEL, D_MODEL), D_MODEL ** -0.5),
        'norm_ffn': gain(ks[19], (DEPTH, D_MODEL)),
        'w_ffn_gate': nrm(ks[20], (N_DENSE, D_MODEL, D_FF), D_MODEL ** -0.5),
        'w_ffn_up': nrm(ks[21], (N_DENSE, D_MODEL, D_FF), D_MODEL ** -0.5),
        'w_ffn_down': nrm(ks[22], (N_DENSE, D_FF, D_MODEL), D_FF ** -0.5),
        'w_router': nrm(ks[23], (N_MOE, D_MODEL, N_EXPERTS), D_MODEL ** -0.5),
        'w_exp_gate': nrm(ks[24], (N_MOE, N_EXPERTS, D_MODEL, D_FF_EXPERT), D_MODEL ** -0.5),
        'w_exp_up': nrm(ks[25], (N_MOE, N_EXPERTS, D_MODEL, D_FF_EXPERT), D_MODEL ** -0.5),
        'w_exp_down': nrm(ks[26], (N_MOE, N_EXPERTS, D_FF_EXPERT, D_MODEL), D_FF_EXPERT ** -0.5),
        'norm_final': gain(ks[27], (D_MODEL,)),
    }


def reference(x_prompt, x_sample, cache_k, cache_v, page_table, state_conv_a, state_conv_c, state_delta,
              norm_mix, w_in, b_in, conv_a_w, conv_c_w, a_log, dt_bias, norm_c, w_br_a, w_br_b, w_br_c, w_out,
              norm_ffn, w_ffn_gate, w_ffn_up, w_ffn_down, w_router, w_exp_gate, w_exp_up, w_exp_down,
              norm_final):
    params = (norm_mix, w_in, b_in, conv_a_w, conv_c_w, a_log, dt_bias, norm_c, w_br_a, w_br_b, w_br_c, w_out,
              norm_ffn, w_ffn_gate, w_ffn_up, w_ffn_down, w_router, w_exp_gate, w_exp_up, w_exp_down, norm_final)
    bp = x_prompt.shape[0]
    zero_a = jnp.zeros((bp, CONV_A - 1, W_A), x_prompt.dtype)
    zero_c = jnp.zeros((bp, CONV_C - 1, W_CONV_C), x_prompt.dtype)
    zero_s = jnp.zeros((bp, N_HEADS_C, DK_C, DV_C), jnp.float32)
    y_prompt, k_prompt, v_prompt, conv_a_prompt, conv_c_prompt, delta_prompt = trunk(
        x_prompt, [stick_breaking_prompt] * DEPTH, [zero_a] * DEPTH, [zero_c] * DEPTH, [zero_s] * DEPTH, *params)
    attends = [functools.partial(stick_breaking_sample, cache_k=cache_k[l], cache_v=cache_v[l], page_table=page_table)
               for l in range(DEPTH)]
    y_sample, k_sample, v_sample, conv_a_sample, conv_c_sample, delta_sample = trunk(
        x_sample, attends, state_conv_a, state_conv_c, state_delta, *params)
    return (y_prompt, y_sample, k_prompt, v_prompt, k_sample, v_sample, conv_a_prompt, conv_a_sample,
            conv_c_prompt, conv_c_sample, delta_prompt, delta_sample)
```

```python
import functools

import jax
import jax.numpy as jnp
from jax import lax
from jax.experimental import pallas as pl
from jax.experimental.pallas import tpu as pltpu

F32 = jnp.float32
BF16 = jnp.bfloat16

RMS_EPS = 1e-6
L2_EPS = 1e-6

N_HEADS_B = 8
HEAD_DIM_B = 64
N_HEADS_C = 8
DK_C = 128
DV_C = 128
N_EXPERTS = 8
CHUNK = 64
CONV_A = 3
CONV_C = 4

LANES = 128
SUBLANES = 8
VMEM_LIMIT_BYTES = 56 * 1024 * 1024

W_A = 512
W_B = 512
W_QKV_C = 3072
W_Z_C = 1024
N_MAIN = 3 * W_A + 3 * W_B + W_QKV_C + W_Z_C
N_GATES = 3072
COL_GATES = N_MAIN
COL_BD = N_MAIN + N_GATES
N_PROJ = COL_BD + LANES


def _cparams(*sem):
    return pltpu.CompilerParams(dimension_semantics=sem, vmem_limit_bytes=VMEM_LIMIT_BYTES)


def _sigmoid(x):
    return 1.0 / (1.0 + jnp.exp(-x))


def _softplus(x):
    return jnp.maximum(x, 0.0) + jnp.log1p(jnp.exp(-jnp.abs(x)))


def _dot(a, b):
    return jnp.dot(a, b, preferred_element_type=F32)


def _dot_nt(a, b):
    return lax.dot_general(a, b, (((1,), (1,)), ((), ())), preferred_element_type=F32)


def _dot_tn(a, b):
    return lax.dot_general(a, b, (((0,), (0,)), ((), ())), preferred_element_type=F32)


def _dot_hi(a, b):
    return jnp.dot(a, b, preferred_element_type=F32, precision=lax.Precision.HIGHEST)


def _rms_scale(x):
    return x * lax.rsqrt(jnp.mean(x * x, axis=-1, keepdims=True) + RMS_EPS)


def _norm_matmul_kernel(x_ref, g_ref, w_ref, b_ref, o_ref, xn_ref):
    @pl.when(pl.program_id(1) == 0)
    def _():
        xn_ref[...] = (_rms_scale(x_ref[...]) * g_ref[...]).astype(BF16)

    o_ref[...] = _dot(xn_ref[...], w_ref[...]) + b_ref[...]


def _norm_matmul(x, g, w, b, *, tm, tn):
    m, d = x.shape
    n = w.shape[1]
    return pl.pallas_call(
        _norm_matmul_kernel,
        grid=(m // tm, n // tn),
        in_specs=[
            pl.BlockSpec((tm, d), lambda i, j: (i, 0)),
            pl.BlockSpec((1, d), lambda i, j: (0, 0)),
            pl.BlockSpec((d, tn), lambda i, j: (0, j)),
            pl.BlockSpec((1, tn), lambda i, j: (0, j)),
        ],
        out_specs=pl.BlockSpec((tm, tn), lambda i, j: (i, j)),
        out_shape=jax.ShapeDtypeStruct((m, n), F32),
        scratch_shapes=[pltpu.VMEM((tm, d), BF16)],
        compiler_params=_cparams("parallel", "arbitrary"),
        name="norm_in_proj",
    )(x, g, w, b)


def _mixer_a_kernel(h_ref, gb_ref, gc_ref, buf_ref, w_ref, act_ref, nbuf_ref, p_ref, *, tt, width):
    t = pl.program_id(1)
    lo = SUBLANES - (width - 1)

    @pl.when(t == 0)
    def _():
        p_ref[lo:SUBLANES, :] = buf_ref[...]

    @pl.when(t > 0)
    def _():
        p_ref[0:SUBLANES, :] = p_ref[tt:tt + SUBLANES, :]

    p_ref[SUBLANES:SUBLANES + tt, :] = gc_ref[...] * h_ref[...]
    y = p_ref[lo:lo + tt, :] * w_ref[0:1, :]
    for i in range(1, width):
        y = y + p_ref[lo + i:lo + i + tt, :] * w_ref[i:i + 1, :]
    act_ref[...] = (gb_ref[...] * y).astype(BF16)
    nbuf_ref[...] = p_ref[SUBLANES + tt - (width - 1):SUBLANES + tt, :]


def _mixer_a(u3, buf, w, *, tt):
    b, t, _ = u3.shape
    width, c = w.shape
    assert t % tt == 0 and t >= width - 1 and c == W_A
    col = lambda k: pl.BlockSpec((None, tt, c), lambda bi, ti: (bi, ti, k))
    return pl.pallas_call(
        functools.partial(_mixer_a_kernel, tt=tt, width=width),
        grid=(b, t // tt),
        in_specs=[
            col(0), col(1), col(2),
            pl.BlockSpec((None, width - 1, c), lambda bi, ti: (bi, 0, 0)),
            pl.BlockSpec((width, c), lambda bi, ti: (0, 0)),
        ],
        out_specs=[
            pl.BlockSpec((None, tt, c), lambda bi, ti: (bi, ti, 0)),
            pl.BlockSpec((None, width - 1, c), lambda bi, ti: (bi, 0, 0)),
        ],
        out_shape=[
            jax.ShapeDtypeStruct((b, t, c), BF16),
            jax.ShapeDtypeStruct((b, width - 1, c), F32),
        ],
        scratch_shapes=[pltpu.VMEM((tt + SUBLANES, c), F32)],
        compiler_params=_cparams("parallel", "arbitrary"),
        name="mixer_a_conv",
    )(u3, u3, u3, buf, w)


def _sb_tile(z, v_bf, tri, r_run, strict):
    sp = _softplus(z)
    ls = -sp
    if strict is not None:
        ls = jnp.where(strict, ls, 0.0)
    hi = ls.astype(BF16)
    lo = (ls - hi.astype(F32)).astype(BF16)
    la = _dot(hi, tri) + _dot(lo, tri)
    a = jnp.exp(z - sp + la + r_run)
    if strict is not None:
        a = jnp.where(strict, a, 0.0)
    return _dot(a.astype(BF16), v_bf), r_run + jnp.sum(ls, axis=-1, keepdims=True)


def _later_key_matrix(n):
    s = lax.broadcasted_iota(jnp.int32, (n, n), 0)
    j = lax.broadcasted_iota(jnp.int32, (n, n), 1)
    return jnp.where(s > j, 1.0, 0.0).astype(BF16)


def _sb_prompt_kernel(q_ref, k_ref, v_ref, o_ref, kb_ref, vb_ref, *, tq, scale):
    qi = pl.program_id(2)

    @pl.when(qi == 0)
    def _():
        kb_ref[...] = k_ref[...].astype(BF16)
        vb_ref[...] = v_ref[...].astype(BF16)

    lane = lax.broadcasted_iota(jnp.int32, (tq, LANES), 1)
    q = q_ref[...] * scale
    first = lane < HEAD_DIM_B
    qh = (jnp.where(first, q, 0.0).astype(BF16), jnp.where(first, 0.0, q).astype(BF16))
    row = lax.broadcasted_iota(jnp.int32, (tq, tq), 0)
    col = lax.broadcasted_iota(jnp.int32, (tq, tq), 1)
    strict = col < row
    tri = _later_key_matrix(tq)

    def tile(kb, carry, mask):
        start = pl.multiple_of(kb * tq, tq)
        k_bf = kb_ref[pl.ds(start, tq), :]
        v_bf = vb_ref[pl.ds(start, tq), :]
        out = []
        for h in range(2):
            acc, r_run = carry[h]
            av, r_new = _sb_tile(_dot_nt(qh[h], k_bf), v_bf, tri, r_run, mask)
            out.append((acc + av, r_new))
        return tuple(out)

    zero = (jnp.zeros((tq, LANES), F32), jnp.zeros((tq, 1), F32))
    carry = tile(qi, (zero, zero), strict)
    carry = lax.fori_loop(0, qi, lambda j, c: tile(qi - 1 - j, c, None), carry)
    o_ref[...] = jnp.where(first, carry[0][0], carry[1][0]).astype(BF16)


def _sb_prompt(u3, *, tq):
    b, t, _ = u3.shape
    assert t % tq == 0
    pairs = W_B // LANES
    q_blk, k_blk, v_blk = (3 * W_A) // LANES, (3 * W_A + W_B) // LANES, (3 * W_A + 2 * W_B) // LANES
    return pl.pallas_call(
        functools.partial(_sb_prompt_kernel, tq=tq, scale=HEAD_DIM_B ** -0.5),
        grid=(b, pairs, t // tq),
        in_specs=[
            pl.BlockSpec((None, tq, LANES), lambda bi, hp, qi: (bi, qi, q_blk + hp)),
            pl.BlockSpec((None, t, LANES), lambda bi, hp, qi: (bi, 0, k_blk + hp)),
            pl.BlockSpec((None, t, LANES), lambda bi, hp, qi: (bi, 0, v_blk + hp)),
        ],
        out_specs=pl.BlockSpec((None, tq, LANES), lambda bi, hp, qi: (bi, qi, hp)),
        out_shape=jax.ShapeDtypeStruct((b, t, W_B), BF16),
        scratch_shapes=[pltpu.VMEM((t, LANES), BF16), pltpu.VMEM((t, LANES), BF16)],
        compiler_params=_cparams("parallel", "parallel", "arbitrary"),
        name="stickbreak_prompt",
    )(u3, u3, u3)


def _sb_sample_kernel(pt_ref, q_ref, ko_ref, vo_ref, kp_ref, vp_ref, o_ref, qs_ref, acc_ref, r_ref, own_ref,
                      *, n_new, page, scale):
    del pt_ref
    j = pl.program_id(1)
    nh, hd = N_HEADS_B, HEAD_DIM_B
    rows = nh * n_new
    tri = _later_key_matrix(page)

    def attend(z, v_heads, strict):
        av_r = _sb_tile_multi(z, v_heads, tri, r_ref[...], strict, n_new)
        avs, r_new = av_r
        for h in range(nh):
            acc_ref[h] += avs[h]
        r_ref[...] = r_new

    @pl.when(j == 0)
    def _():
        for h in range(nh):
            qs_ref[h] = (q_ref[:, h * hd:(h + 1) * hd] * scale).astype(BF16)
            acc_ref[h] = jnp.zeros((n_new, hd), F32)
        r_ref[...] = jnp.zeros((rows, 1), F32)
        own_ref[...] = jnp.zeros(own_ref.shape, BF16)
        own_ref[0, 0:n_new, :] = ko_ref[...].astype(BF16)
        own_ref[1, 0:n_new, :] = vo_ref[...].astype(BF16)
        z = jnp.concatenate(
            [_dot_nt(qs_ref[h], own_ref[0, :, h * hd:(h + 1) * hd]) for h in range(nh)], axis=0)
        qpos = lax.broadcasted_iota(jnp.int32, (rows, page), 0) % n_new
        kpos = lax.broadcasted_iota(jnp.int32, (rows, page), 1)
        attend(z, [own_ref[1, :, h * hd:(h + 1) * hd] for h in range(nh)], kpos < qpos)

    z = jnp.concatenate([_dot_nt(qs_ref[h], kp_ref[:, h, :].astype(BF16)) for h in range(nh)], axis=0)
    attend(z, [vp_ref[:, h, :].astype(BF16) for h in range(nh)], None)

    @pl.when(j == pl.num_programs(1) - 1)
    def _():
        o_ref[...] = jnp.concatenate([acc_ref[h] for h in range(nh)], axis=-1).astype(BF16)


def _sb_tile_multi(z, v_heads, tri, r_run, strict, n_new):
    sp = _softplus(z)
    ls = -sp
    if strict is not None:
        ls = jnp.where(strict, ls, 0.0)
    hi = ls.astype(BF16)
    lo = (ls - hi.astype(F32)).astype(BF16)
    la = _dot(hi, tri) + _dot(lo, tri)
    a = jnp.exp(z - sp + la + r_run)
    if strict is not None:
        a = jnp.where(strict, a, 0.0)
    a = a.astype(BF16)
    avs = [_dot(a[h * n_new:(h + 1) * n_new, :], v_heads[h]) for h in range(len(v_heads))]
    return avs, r_run + jnp.sum(ls, axis=-1, keepdims=True)


def _sb_sample(u3, cache_k, cache_v, page_table, layer):
    db, n_new, _ = u3.shape
    _, _, page, nh, hd = cache_k.shape
    n_pages = page_table.shape[1]
    assert nh == N_HEADS_B and hd == HEAD_DIM_B and n_new % SUBLANES == 0
    blk = (3 * W_A) // W_B
    page_spec = pl.BlockSpec(
        (None, None, page, nh, hd), lambda bi, j, pt: (layer, pt[bi, n_pages - 1 - j], 0, 0, 0))
    grid_spec = pltpu.PrefetchScalarGridSpec(
        num_scalar_prefetch=1,
        grid=(db, n_pages),
        in_specs=[
            pl.BlockSpec((None, n_new, W_B), lambda bi, j, pt: (bi, 0, blk)),
            pl.BlockSpec((None, n_new, W_B), lambda bi, j, pt: (bi, 0, blk + 1)),
            pl.BlockSpec((None, n_new, W_B), lambda bi, j, pt: (bi, 0, blk + 2)),
            page_spec, page_spec,
        ],
        out_specs=pl.BlockSpec((None, n_new, W_B), lambda bi, j, pt: (bi, 0, 0)),
        scratch_shapes=[
            pltpu.VMEM((nh, n_new, hd), BF16),
            pltpu.VMEM((nh, n_new, hd), F32),
            pltpu.VMEM((nh * n_new, 1), F32),
            pltpu.VMEM((2, page, W_B), BF16),
        ],
    )
    return pl.pallas_call(
        functools.partial(_sb_sample_kernel, n_new=n_new, page=page, scale=hd ** -0.5),
        grid_spec=grid_spec,
        out_shape=jax.ShapeDtypeStruct((db, n_new, W_B), BF16),
        compiler_params=_cparams("parallel", "arbitrary"),
        name="stickbreak_sample",
    )(page_table, u3, u3, u3, cache_k, cache_v)


def _delta_kernel(q_ref, k_ref, v_ref, z_ref, bdc_ref, bdr_ref, buf_ref, cw_ref, gp_ref, nc_ref, s0_ref,
                  y_ref, s_ref, p_ref, *, tt, chunk, width, t_valid):
    h = pl.program_id(1)
    t = pl.program_id(2)
    lo = SUBLANES - (width - 1)
    dk = DK_C

    @pl.when(t == 0)
    def _():
        for i in range(3):
            p_ref[i, lo:SUBLANES, :] = buf_ref[i]
        s_ref[...] = s0_ref[...]

    @pl.when(t > 0)
    def _():
        for i in range(3):
            p_ref[i, 0:SUBLANES, :] = p_ref[i, tt:tt + SUBLANES, :]

    conv = []
    for i, ref in enumerate((q_ref, k_ref, v_ref)):
        p_ref[i, SUBLANES:SUBLANES + tt, :] = ref[...]
        y = p_ref[i, lo:lo + tt, :] * cw_ref[i, 0:1, :]
        for w in range(1, width):
            y = y + p_ref[i, lo + w:lo + w + tt, :] * cw_ref[i, w:w + 1, :]
        conv.append(y * _sigmoid(y))
    qc, kc, vc = conv
    qn = qc * lax.rsqrt(jnp.sum(qc * qc, axis=-1, keepdims=True) + L2_EPS) * (dk ** -0.5)
    kn = kc * lax.rsqrt(jnp.sum(kc * kc, axis=-1, keepdims=True) + L2_EPS)

    neg_a = -jnp.exp(gp_ref[0:1, :])
    dt_b = gp_ref[1:2, :]
    lane = lax.broadcasted_iota(jnp.int32, (tt, LANES), 1)
    bd = bdc_ref[...]
    b_col = jnp.sum(jnp.where(lane == h, bd, 0.0), axis=-1, keepdims=True)
    a_col = jnp.sum(jnp.where(lane == h + N_HEADS_C, bd, 0.0), axis=-1, keepdims=True)
    beta_col = _sigmoid(b_col)
    g_col = neg_a[:, 0:1] * _softplus(a_col + dt_b[:, 0:1])
    a_row = bdr_ref[pl.ds(h + N_HEADS_C, 1), :]
    g_row = neg_a[:, 0:1] * _softplus(a_row + dt_b[:, 0:1])
    if t_valid < tt:
        beta_col = jnp.where(lax.broadcasted_iota(jnp.int32, (tt, 1), 0) < t_valid, beta_col, 0.0)
        g_col = jnp.where(lax.broadcasted_iota(jnp.int32, (tt, 1), 0) < t_valid, g_col, 0.0)
        g_row = jnp.where(lax.broadcasted_iota(jnp.int32, (1, tt), 1) < t_valid, g_row, 0.0)

    ri = lax.broadcasted_iota(jnp.int32, (chunk, chunk), 0)
    ci = lax.broadcasted_iota(jnp.int32, (chunk, chunk), 1)
    incl = ri >= ci
    strict = ri > ci
    eye = jnp.where(ri == ci, 1.0, 0.0)
    n_doubling = chunk.bit_length() - 2

    outs = []
    for c in range(tt // chunk):
        sl = slice(c * chunk, (c + 1) * chunk)
        qk, kk_, vk = qn[sl], kn[sl], vc[sl]
        gc_col, gc_row, bc = g_col[sl], g_row[:, sl], beta_col[sl]
        gcum_col = jnp.sum(jnp.where(incl, gc_row, 0.0), axis=1, keepdims=True)
        gcum_row = jnp.sum(jnp.where(ri <= ci, gc_col, 0.0), axis=0, keepdims=True)
        diff = gcum_col - gcum_row
        dec_strict = jnp.where(strict, jnp.exp(jnp.where(strict, diff, 0.0)), 0.0)
        dec_incl = jnp.where(incl, jnp.exp(jnp.where(incl, diff, 0.0)), 0.0)
        k_bf = kk_.astype(BF16)
        m = bc * _dot_nt(k_bf, k_bf) * dec_strict
        pw = -m
        inv = eye + pw
        for _ in range(n_doubling):
            pw = _dot_hi(pw, pw)
            inv = inv + _dot_hi(inv, pw)
        e_col = jnp.exp(gcum_col)
        rhs = jnp.concatenate([kk_ * (bc * e_col), vk * bc], axis=-1)
        sol = _dot_hi(inv, rhs)
        w_, uv = sol[:, :dk], sol[:, dk:]
        aqk = _dot_nt(qk.astype(BF16), k_bf) * dec_incl
        q_dec = qk * e_col
        g_last = gcum_col[chunk - 1:chunk, :]
        k_dec = kk_ * jnp.exp(g_last - gcum_col)
        s = s_ref[...]
        s_bf = s.astype(BF16)
        u = uv - _dot(w_.astype(BF16), s_bf)
        u_bf = u.astype(BF16)
        outs.append(_dot(q_dec.astype(BF16), s_bf) + _dot(aqk.astype(BF16), u_bf))
        s_ref[...] = jnp.exp(g_last) * s + _dot_tn(k_dec.astype(BF16), u_bf)
    o = jnp.concatenate(outs, axis=0) if len(outs) > 1 else outs[0]
    zg = z_ref[...]
    y_ref[...] = (_rms_scale(o) * nc_ref[...] * (zg * _sigmoid(zg))).astype(BF16)


def _delta(u3, bd_rows, buf, conv_w, gate_par, norm_c, s0, *, tt, t_valid):
    b, t, _ = u3.shape
    width = conv_w.shape[0]
    assert t % tt == 0 and tt % CHUNK == 0
    nh = N_HEADS_C
    qb, kb, vb = (3 * W_A + 3 * W_B) // LANES, (3 * W_A + 3 * W_B) // LANES + nh, (3 * W_A + 3 * W_B) // LANES + 2 * nh
    zb = (3 * W_A + 3 * W_B + W_QKV_C) // LANES
    bdb = COL_BD // LANES
    cw3 = conv_w.reshape(width, 3, nh, LANES).transpose(1, 2, 0, 3)
    buf3 = buf.reshape(b, width - 1, 3, nh, LANES).transpose(0, 3, 2, 1, 4)
    col = lambda k: pl.BlockSpec((None, tt, LANES), lambda bi, hi, ti: (bi, ti, k + hi))
    return pl.pallas_call(
        functools.partial(_delta_kernel, tt=tt, chunk=CHUNK, width=width, t_valid=t_valid),
        grid=(b, nh, t // tt),
        in_specs=[
            col(qb), col(kb), col(vb), col(zb),
            pl.BlockSpec((None, tt, LANES), lambda bi, hi, ti: (bi, ti, bdb)),
            pl.BlockSpec((None, 2 * nh, tt), lambda bi, hi, ti: (bi, 0, ti)),
            pl.BlockSpec((None, None, 3, width - 1, LANES), lambda bi, hi, ti: (bi, hi, 0, 0, 0)),
            pl.BlockSpec((3, None, width, LANES), lambda bi, hi, ti: (0, hi, 0, 0)),
            pl.BlockSpec((None, 2, LANES), lambda bi, hi, ti: (hi, 0, 0)),
            pl.BlockSpec((1, DV_C), lambda bi, hi, ti: (0, 0)),
            pl.BlockSpec((None, None, DK_C, DV_C), lambda bi, hi, ti: (bi, hi, 0, 0)),
        ],
        out_specs=[
            pl.BlockSpec((None, tt, LANES), lambda bi, hi, ti: (bi, ti, hi)),
            pl.BlockSpec((None, None, DK_C, DV_C), lambda bi, hi, ti: (bi, hi, 0, 0)),
        ],
        out_shape=[
            jax.ShapeDtypeStruct((b, t, nh * DV_C), BF16),
            jax.ShapeDtypeStruct((b, nh, DK_C, DV_C), F32),
        ],
        scratch_shapes=[pltpu.VMEM((3, tt + SUBLANES, LANES), F32)],
        compiler_params=_cparams("parallel", "parallel", "arbitrary"),
        name="gated_delta",
    )(u3, u3, u3, u3, u3, bd_rows, buf3, cw3, gate_par, norm_c, s0)


def _merge_kernel(x_ref, a_ref, b_ref, c_ref, g0_ref, g1_ref, g2_ref, wa_ref, wb_ref, wc_ref, wo_ref, o_ref):
    merged = (_sigmoid(g0_ref[...]) * _dot(a_ref[...], wa_ref[...])
              + _sigmoid(g1_ref[...]) * _dot(b_ref[...], wb_ref[...])
              + _sigmoid(g2_ref[...]) * _dot(c_ref[...], wc_ref[...]))
    o_ref[...] = x_ref[...] + _dot(merged.astype(BF16), wo_ref[...])


def _merge(x, act_a, act_b, act_c, u, wa, wb, wc, wo, *, tm):
    m, d = x.shape
    gb = COL_GATES // d
    row = lambda width, k=0: pl.BlockSpec((tm, width), lambda i: (i, k))
    full = lambda w: pl.BlockSpec(w.shape, lambda i: (0, 0))
    return pl.pallas_call(
        _merge_kernel,
        grid=(m // tm,),
        in_specs=[
            row(d), row(act_a.shape[1]), row(act_b.shape[1]), row(act_c.shape[1]),
            row(d, gb), row(d, gb + 1), row(d, gb + 2),
            full(wa), full(wb), full(wc), full(wo),
        ],
        out_specs=row(d),
        out_shape=jax.ShapeDtypeStruct((m, d), F32),
        compiler_params=_cparams("parallel"),
        name="merge_out_proj",
    )(x, act_a, act_b, act_c, u, u, u, wa, wb, wc, wo)


def _finish(x_ref, total, gf_ref, o_ref, final_norm):
    y = x_ref[...] + total
    if final_norm:
        y = _rms_scale(y) * gf_ref[...]
    o_ref[...] = y


def _ffn_kernel(x_ref, g_ref, gf_ref, wg_ref, wu_ref, wd_ref, o_ref, hn_ref, acc_ref, *, final_norm):
    f = pl.program_id(1)

    @pl.when(f == 0)
    def _():
        hn_ref[...] = (_rms_scale(x_ref[...]) * g_ref[...]).astype(BF16)
        acc_ref[...] = jnp.zeros(acc_ref.shape, F32)

    hn = hn_ref[...]
    a = _dot(hn, wg_ref[...])
    hidden = (a * _sigmoid(a) * _dot(hn, wu_ref[...])).astype(BF16)
    acc_ref[...] += _dot(hidden, wd_ref[...])

    @pl.when(f == pl.num_programs(1) - 1)
    def _():
        _finish(x_ref, acc_ref[...], gf_ref, o_ref, final_norm)


def _ffn(x, g, gf, wg, wu, wd, *, tm, tf, final_norm):
    m, d = x.shape
    ff = wg.shape[1]
    return pl.pallas_call(
        functools.partial(_ffn_kernel, final_norm=final_norm),
        grid=(m // tm, ff // tf),
        in_specs=[
            pl.BlockSpec((tm, d), lambda i, f: (i, 0)),
            pl.BlockSpec((1, d), lambda i, f: (0, 0)),
            pl.BlockSpec((1, d), lambda i, f: (0, 0)),
            pl.BlockSpec((d, tf), lambda i, f: (0, f)),
            pl.BlockSpec((d, tf), lambda i, f: (0, f)),
            pl.BlockSpec((tf, d), lambda i, f: (f, 0)),
        ],
        out_specs=pl.BlockSpec((tm, d), lambda i, f: (i, 0)),
        out_shape=jax.ShapeDtypeStruct((m, d), F32),
        scratch_shapes=[pltpu.VMEM((tm, d), BF16), pltpu.VMEM((tm, d), F32)],
        compiler_params=_cparams("parallel", "arbitrary"),
        name="dense_swiglu",
    )(x, g, gf, wg, wu, wd)


def _top2_gates(logits):
    lane = lax.broadcasted_iota(jnp.int32, logits.shape, 1).astype(F32)
    neg = -jnp.inf
    lg = jnp.where(lane < N_EXPERTS, logits, neg)
    m1 = jnp.max(lg, axis=-1, keepdims=True)
    i1 = jnp.min(jnp.where(lg == m1, lane, float(LANES)), axis=-1, keepdims=True)
    lg2 = jnp.where(lane == i1, neg, lg)
    m2 = jnp.max(lg2, axis=-1, keepdims=True)
    i2 = jnp.min(jnp.where(lg2 == m2, lane, float(LANES)), axis=-1, keepdims=True)
    e2 = jnp.exp(m2 - m1)
    w1 = 1.0 / (1.0 + e2)
    return jnp.where(lane == i1, w1, jnp.where(lane == i2, e2 * w1, 0.0))


def _moe_kernel(x_ref, g_ref, gf_ref, wr_ref, wg_ref, wu_ref, wd_ref, o_ref, hn_ref, gate_ref, acc_ref, tot_ref,
                *, final_norm):
    e = pl.program_id(1)
    f = pl.program_id(2)
    last_f = pl.num_programs(2) - 1

    @pl.when((e == 0) & (f == 0))
    def _():
        hn = _rms_scale(x_ref[...]) * g_ref[...]
        hn_ref[...] = hn.astype(BF16)
        gate_ref[...] = _top2_gates(_dot_hi(hn, wr_ref[...]))
        tot_ref[...] = jnp.zeros(tot_ref.shape, F32)

    @pl.when(f == 0)
    def _():
        acc_ref[...] = jnp.zeros(acc_ref.shape, F32)

    hn = hn_ref[...]
    a = _dot(hn, wg_ref[...])
    hidden = (a * _sigmoid(a) * _dot(hn, wu_ref[...])).astype(BF16)
    acc_ref[...] += _dot(hidden, wd_ref[...])

    @pl.when(f == last_f)
    def _():
        lane = lax.broadcasted_iota(jnp.int32, gate_ref.shape, 1)
        gate_e = jnp.sum(jnp.where(lane == e, gate_ref[...], 0.0), axis=-1, keepdims=True)
        tot_ref[...] += gate_e * acc_ref[...]

    @pl.when((e == pl.num_programs(1) - 1) & (f == last_f))
    def _():
        _finish(x_ref, tot_ref[...], gf_ref, o_ref, final_norm)


def _moe(x, g, gf, w_router, wg, wu, wd, *, tm, tf, final_norm):
    m, d = x.shape
    ne, _, ff = wg.shape
    return pl.pallas_call(
        functools.partial(_moe_kernel, final_norm=final_norm),
        grid=(m // tm, ne, ff // tf),
        in_specs=[
            pl.BlockSpec((tm, d), lambda i, e, f: (i, 0)),
            pl.BlockSpec((1, d), lambda i, e, f: (0, 0)),
            pl.BlockSpec((1, d), lambda i, e, f: (0, 0)),
            pl.BlockSpec((d, LANES), lambda i, e, f: (0, 0)),
            pl.BlockSpec((None, d, tf), lambda i, e, f: (e, 0, f)),
            pl.BlockSpec((None, d, tf), lambda i, e, f: (e, 0, f)),
            pl.BlockSpec((None, tf, d), lambda i, e, f: (e, f, 0)),
        ],
        out_specs=pl.BlockSpec((tm, d), lambda i, e, f: (i, 0)),
        out_shape=jax.ShapeDtypeStruct((m, d), F32),
        scratch_shapes=[
            pltpu.VMEM((tm, d), BF16), pltpu.VMEM((tm, LANES), F32),
            pltpu.VMEM((tm, d), F32), pltpu.VMEM((tm, d), F32),
        ],
        compiler_params=_cparams("parallel", "arbitrary", "arbitrary"),
        name="moe_swiglu",
    )(x, g, gf, w_router, wg, wu, wd)


def _tile(n, pref):
    return pref if n % pref == 0 else n


def _trunk(x3, attend, bufs_a, bufs_c, states, p, *, delta_pad):
    b, t, d = x3.shape
    m = b * t
    depth = p["w_proj"].shape[0]
    x = x3.reshape(m, d)
    tm = _tile(m, 1024)
    ks, vs, bas, bcs, ss = [], [], [], [], []
    for l in range(depth):
        u = _norm_matmul(x, p["norm_mix"][l], p["w_proj"][l], p["b_proj"][l], tm=tm, tn=1152)
        u3 = u.reshape(b, t, N_PROJ)
        act_a, nbuf_a = _mixer_a(u3, bufs_a[l], p["conv_a_w"][l], tt=_tile(t, 512))
        act_b = attend(u3, l)
        tp = t + delta_pad
        u3c = jnp.pad(u3, ((0, 0), (0, delta_pad), (0, 0))) if delta_pad else u3
        bd_rows = jnp.swapaxes(u3c[:, :, COL_BD:COL_BD + 2 * N_HEADS_C], 1, 2)
        act_c, s_new = _delta(u3c, bd_rows, bufs_c[l], p["conv_c_w"][l], p["gate_par"][l], p["norm_c"][l],
                              states[l], tt=_tile(tp, 256), t_valid=t if delta_pad else tp)
        act_c = act_c[:, :t].reshape(m, -1)
        x = _merge(x, act_a.reshape(m, -1), act_b.reshape(m, -1), act_c, u,
                   p["w_br_a"][l], p["w_br_b"][l], p["w_br_c"][l], p["w_out"][l], tm=_tile(m, 256))
        final = l == depth - 1
        if l % 2 == 0:
            x = _ffn(x, p["norm_ffn"][l], p["norm_final"], p["w_ffn_gate"][l // 2], p["w_ffn_up"][l // 2],
                     p["w_ffn_down"][l // 2], tm=tm, tf=256, final_norm=final)
        else:
            x = _moe(x, p["norm_ffn"][l], p["norm_final"], p["w_router"][l // 2], p["w_exp_gate"][l // 2],
                     p["w_exp_up"][l // 2], p["w_exp_down"][l // 2], tm=tm, tf=512, final_norm=final)
        ks.append(u3[:, :, 3 * W_A + W_B:3 * W_A + 2 * W_B].reshape(b, t, N_HEADS_B, HEAD_DIM_B))
        vs.append(u3[:, :, 3 * W_A + 2 * W_B:3 * W_A + 3 * W_B].reshape(b, t, N_HEADS_B, HEAD_DIM_B))
        bas.append(nbuf_a)
        wc = p["conv_c_w"].shape[1]
        bcs.append(u3[:, t - (wc - 1):, 3 * W_A + 3 * W_B:3 * W_A + 3 * W_B + W_QKV_C])
        ss.append(s_new)
    return (x.reshape(b, t, d), jnp.stack(ks), jnp.stack(vs), jnp.stack(bas), jnp.stack(bcs), jnp.stack(ss))


def kernel(x_prompt, x_sample, cache_k, cache_v, page_table, state_conv_a, state_conv_c, state_delta, norm_mix, w_in, b_in, conv_a_w, conv_c_w, a_log, dt_bias, norm_c, w_br_a, w_br_b, w_br_c, w_out, norm_ffn, w_ffn_gate, w_ffn_up, w_ffn_down, w_router, w_exp_gate, w_exp_up, w_exp_down, norm_final):
    depth, d, n_in = w_in.shape
    assert n_in == N_MAIN + 2 * N_HEADS_C + N_GATES
    col_gate_src = N_MAIN + 2 * N_HEADS_C
    pad = N_PROJ - n_in

    def reorder(a):
        return jnp.concatenate(
            [a[..., :N_MAIN], a[..., col_gate_src:], a[..., N_MAIN:col_gate_src],
             jnp.zeros(a.shape[:-1] + (pad,), a.dtype)], axis=-1)

    p = {
        "norm_mix": norm_mix[:, None, :],
        "w_proj": reorder(w_in).astype(BF16),
        "b_proj": reorder(b_in)[:, None, :],
        "conv_a_w": conv_a_w,
        "conv_c_w": conv_c_w,
        "gate_par": jnp.broadcast_to(jnp.stack([a_log, dt_bias], axis=-1)[..., None], (depth, N_HEADS_C, 2, LANES)),
        "norm_c": norm_c[:, None, :],
        "w_br_a": w_br_a.astype(BF16), "w_br_b": w_br_b.astype(BF16), "w_br_c": w_br_c.astype(BF16),
        "w_out": w_out.astype(BF16),
        "norm_ffn": norm_ffn[:, None, :],
        "norm_final": norm_final[None, :],
        "w_ffn_gate": w_ffn_gate.astype(BF16), "w_ffn_up": w_ffn_up.astype(BF16),
        "w_ffn_down": w_ffn_down.astype(BF16),
        "w_router": jnp.pad(w_router, ((0, 0), (0, 0), (0, LANES - w_router.shape[-1]))),
        "w_exp_gate": w_exp_gate.astype(BF16), "w_exp_up": w_exp_up.astype(BF16),
        "w_exp_down": w_exp_down.astype(BF16),
    }

    bp = x_prompt.shape[0]
    zero_a = jnp.zeros((depth, bp) + state_conv_a.shape[2:], F32)
    zero_c = jnp.zeros((depth, bp) + state_conv_c.shape[2:], F32)
    zero_s = jnp.zeros((depth, bp) + state_delta.shape[2:], F32)
    y_p, k_p, v_p, ca_p, cc_p, s_p = _trunk(
        x_prompt, lambda u3, l: _sb_prompt(u3, tq=128), zero_a, zero_c, zero_s, p, delta_pad=0)

    n_new = x_sample.shape[1]
    y_s, k_s, v_s, ca_s, cc_s, s_s = _trunk(
        x_sample, lambda u3, l: _sb_sample(u3, cache_k, cache_v, page_table, l),
        state_conv_a, state_conv_c, state_delta, p, delta_pad=(-n_new) % CHUNK)
    return (y_p, y_s, k_p, v_p, k_s, v_s, ca_p, ca_s, cc_p, cc_s, s_p, s_s)
```

```python
import functools

import jax
import jax.numpy as jnp
from jax import lax
from jax.experimental import pallas as pl
from jax.experimental.pallas import tpu as pltpu

F32 = jnp.float32
BF16 = jnp.bfloat16

RMS_EPS = 1e-6
L2_EPS = 1e-6

N_HEADS_B = 8
HEAD_DIM_B = 64
N_HEADS_C = 8
DK_C = 128
DV_C = 128
N_EXPERTS = 8
DELTA_CHUNK = 128

LANES = 128
SUBLANES = 8
VMEM_LIMIT_BYTES = 56 * 1024 * 1024

W_A = 512
W_B = 512
W_QKV_C = 3072
W_Z_C = 1024
N_MAIN = 3 * W_A + 3 * W_B + W_QKV_C + W_Z_C
N_GATES = 3072
COL_GATES = N_MAIN
COL_BD = N_MAIN + N_GATES
N_PROJ = COL_BD + LANES


def _cparams(*sem):
    return pltpu.CompilerParams(dimension_semantics=sem, vmem_limit_bytes=VMEM_LIMIT_BYTES)


def _sigmoid(x):
    return 1.0 / (1.0 + jnp.exp(-x))


def _softplus(x):
    return jnp.maximum(x, 0.0) + jnp.log1p(jnp.exp(-jnp.abs(x)))


def _dot(a, b):
    return jnp.dot(a, b, preferred_element_type=F32)


def _dot_nt(a, b):
    return lax.dot_general(a, b, (((1,), (1,)), ((), ())), preferred_element_type=F32)


def _dot_tn(a, b):
    return lax.dot_general(a, b, (((0,), (0,)), ((), ())), preferred_element_type=F32)


def _dot_hi(a, b):
    return jnp.dot(a, b, preferred_element_type=F32, precision=lax.Precision.HIGHEST)


def _rms_scale(x):
    return x * lax.rsqrt(jnp.mean(x * x, axis=-1, keepdims=True) + RMS_EPS)


def _split_hi_lo(x):
    bits = lax.bitcast_convert_type(x, jnp.uint32) & jnp.uint32(0xFFFF0000)
    hi = lax.bitcast_convert_type(bits, F32)
    return hi.astype(BF16), (x - hi).astype(BF16)


def _norm_matmul_kernel(x_ref, g_ref, w_ref, b_ref, o_ref, xn_ref):
    @pl.when(pl.program_id(1) == 0)
    def _():
        xn_ref[...] = (_rms_scale(x_ref[...]) * g_ref[...]).astype(BF16)

    o_ref[...] = _dot(xn_ref[...], w_ref[...]) + b_ref[...]


def _norm_matmul(x, g, w, b, *, tm, tn):
    m, d = x.shape
    n = w.shape[1]
    return pl.pallas_call(
        _norm_matmul_kernel,
        grid=(m // tm, n // tn),
        in_specs=[
            pl.BlockSpec((tm, d), lambda i, j: (i, 0)),
            pl.BlockSpec((1, d), lambda i, j: (0, 0)),
            pl.BlockSpec((d, tn), lambda i, j: (0, j)),
            pl.BlockSpec((1, tn), lambda i, j: (0, j)),
        ],
        out_specs=pl.BlockSpec((tm, tn), lambda i, j: (i, j)),
        out_shape=jax.ShapeDtypeStruct((m, n), F32),
        scratch_shapes=[pltpu.VMEM((tm, d), BF16)],
        compiler_params=_cparams("parallel", "arbitrary"),
        name="norm_in_proj",
    )(x, g, w, b)


def _mixer_a_kernel(h_ref, gb_ref, gc_ref, buf_ref, w_ref, act_ref, nbuf_ref, p_ref, *, tt, width):
    t = pl.program_id(1)
    lo = SUBLANES - (width - 1)

    @pl.when(t == 0)
    def _():
        p_ref[lo:SUBLANES, :] = buf_ref[...]

    @pl.when(t > 0)
    def _():
        p_ref[0:SUBLANES, :] = p_ref[tt:tt + SUBLANES, :]

    p_ref[SUBLANES:SUBLANES + tt, :] = gc_ref[...] * h_ref[...]
    y = p_ref[lo:lo + tt, :] * w_ref[0:1, :]
    for i in range(1, width):
        y = y + p_ref[lo + i:lo + i + tt, :] * w_ref[i:i + 1, :]
    act_ref[...] = (gb_ref[...] * y).astype(BF16)
    nbuf_ref[...] = p_ref[SUBLANES + tt - (width - 1):SUBLANES + tt, :]


def _mixer_a(u3, buf, w, *, tt):
    b, t, _ = u3.shape
    width, c = w.shape
    assert t % tt == 0 and t >= width - 1 and c == W_A
    col = lambda k: pl.BlockSpec((None, tt, c), lambda bi, ti: (bi, ti, k))
    return pl.pallas_call(
        functools.partial(_mixer_a_kernel, tt=tt, width=width),
        grid=(b, t // tt),
        in_specs=[
            col(0), col(1), col(2),
            pl.BlockSpec((None, width - 1, c), lambda bi, ti: (bi, 0, 0)),
            pl.BlockSpec((width, c), lambda bi, ti: (0, 0)),
        ],
        out_specs=[
            pl.BlockSpec((None, tt, c), lambda bi, ti: (bi, ti, 0)),
            pl.BlockSpec((None, width - 1, c), lambda bi, ti: (bi, 0, 0)),
        ],
        out_shape=[
            jax.ShapeDtypeStruct((b, t, c), BF16),
            jax.ShapeDtypeStruct((b, width - 1, c), F32),
        ],
        scratch_shapes=[pltpu.VMEM((tt + SUBLANES, c), F32)],
        compiler_params=_cparams("parallel", "arbitrary"),
        name="mixer_a_conv",
    )(u3, u3, u3, buf, w)


def _suffix_sum_matrix(n):
    s = lax.broadcasted_iota(jnp.int32, (2 * n, n), 0)
    s = jnp.where(s >= n, s - n, s)
    j = lax.broadcasted_iota(jnp.int32, (2 * n, n), 1)
    return jnp.where(s >= j, 1.0, 0.0).astype(BF16)


def _sb_softplus_sums(z, tri2, mask):
    neg_abs = lax.bitcast_convert_type(lax.bitcast_convert_type(z, jnp.uint32) | jnp.uint32(0x80000000), F32)
    sp = jnp.maximum(z, 0.0) + jnp.log(1.0 + jnp.exp(neg_abs))
    if mask is not None:
        sp = jnp.where(mask, sp, 0.0)
    hi, lo = _split_hi_lo(sp)
    return _dot(jnp.concatenate([hi, lo], axis=1), tri2), jnp.sum(sp, axis=-1, keepdims=True)


def _sb_weights(z, suffix, r_run, mask):
    a = jnp.exp(z - suffix - r_run)
    if mask is not None:
        a = jnp.where(mask, a, 0.0)
    return a.astype(BF16)


def _sb_prompt_kernel(q_ref, k_ref, v_ref, o_ref, kb_ref, vb_ref, acc_ref, r_ref, qh_ref, tri_ref,
                      *, tq, scale):
    qi = pl.program_id(2)
    kbn = LANES

    @pl.when(qi == 0)
    def _():
        kb_ref[...] = k_ref[...].astype(BF16)
        vb_ref[...] = v_ref[...].astype(BF16)

    lane = lax.broadcasted_iota(jnp.int32, (tq, LANES), 1)
    first = lane < HEAD_DIM_B
    q = q_ref[...] * scale
    qh_ref[0] = jnp.where(first, q, 0.0).astype(BF16)
    qh_ref[1] = jnp.where(first, 0.0, q).astype(BF16)
    tri_ref[...] = _suffix_sum_matrix(kbn)
    row = lax.broadcasted_iota(jnp.int32, (tq, kbn), 0)
    col = lax.broadcasted_iota(jnp.int32, (tq, kbn), 1)
    acc_ref[...] = jnp.zeros(acc_ref.shape, F32)
    r_ref[...] = jnp.zeros(r_ref.shape, F32)

    def block(kb2, masks):
        start = pl.multiple_of(kb2 * 2 * kbn, 2 * kbn)
        k2 = kb_ref[pl.ds(start, 2 * kbn), :]
        v2 = vb_ref[pl.ds(start, 2 * kbn), :]
        pieces = []
        for h in range(2):
            z = _dot_nt(qh_ref[h], k2)
            pieces += [z[:, kbn:], z[:, :kbn]]
        z = jnp.concatenate(pieces, axis=0)
        mask = None if masks is None else jnp.concatenate(list(masks) * 2, axis=0)
        suffix, rs = _sb_softplus_sums(z, tri_ref[...], mask)
        r_parts = []
        for h in range(2):
            r_in = r_ref[h]
            r_mid = r_in + rs[2 * h * tq:(2 * h + 1) * tq]
            r_parts += [r_in, r_mid]
            r_ref[h] = r_mid + rs[(2 * h + 1) * tq:(2 * h + 2) * tq]
        a = _sb_weights(z, suffix, jnp.concatenate(r_parts, axis=0), mask)
        for h in range(2):
            a_h = jnp.concatenate([a[(2 * h + 1) * tq:(2 * h + 2) * tq], a[2 * h * tq:(2 * h + 1) * tq]], axis=1)
            acc_ref[h] += _dot(a_h, v2)

    n_diag = tq // (2 * kbn)
    for d in reversed(range(n_diag)):
        block(qi * n_diag + d, (col + (2 * d + 1) * kbn < row, col + 2 * d * kbn < row))
    lax.fori_loop(0, qi * n_diag, lambda it, c: (block(qi * n_diag - 1 - it, None), c)[1], 0)
    o_ref[...] = jnp.where(first, acc_ref[0], acc_ref[1]).astype(BF16)


def _sb_prompt(u3, *, tq):
    b, t, _ = u3.shape
    assert t % tq == 0 and tq % (2 * LANES) == 0
    pairs = W_B // LANES
    q_blk, k_blk, v_blk = (3 * W_A) // LANES, (3 * W_A + W_B) // LANES, (3 * W_A + 2 * W_B) // LANES
    return pl.pallas_call(
        functools.partial(_sb_prompt_kernel, tq=tq, scale=HEAD_DIM_B ** -0.5),
        grid=(b, pairs, t // tq),
        in_specs=[
            pl.BlockSpec((None, tq, LANES), lambda bi, hp, qi: (bi, qi, q_blk + hp)),
            pl.BlockSpec((None, t, LANES), lambda bi, hp, qi: (bi, 0, k_blk + hp)),
            pl.BlockSpec((None, t, LANES), lambda bi, hp, qi: (bi, 0, v_blk + hp)),
        ],
        out_specs=pl.BlockSpec((None, tq, LANES), lambda bi, hp, qi: (bi, qi, hp)),
        out_shape=jax.ShapeDtypeStruct((b, t, W_B), BF16),
        scratch_shapes=[
            pltpu.VMEM((t, LANES), BF16), pltpu.VMEM((t, LANES), BF16),
            pltpu.VMEM((2, tq, LANES), F32), pltpu.VMEM((2, tq, LANES), F32),
            pltpu.VMEM((2, tq, LANES), BF16), pltpu.VMEM((2 * LANES, LANES), BF16),
        ],
        compiler_params=_cparams("parallel", "parallel", "arbitrary"),
        name="stickbreak_prompt",
    )(u3, u3, u3)


def _sb_sample_kernel(pt_ref, q_ref, ko_ref, vo_ref, *refs, n_new, page, pps, scale):
    del pt_ref
    k_refs, v_refs = refs[:pps], refs[pps:2 * pps]
    o_ref, qbd_ref, acc_ref, r_ref, own_ref = refs[2 * pps:]
    j = pl.program_id(1)
    nh, hd = N_HEADS_B, HEAD_DIM_B
    rows = nh * n_new
    tri2 = _suffix_sum_matrix(page)

    @pl.when(j == 0)
    def _():
        qt = jnp.concatenate([q_ref[...] * scale] * nh, axis=0)
        rh = lax.broadcasted_iota(jnp.int32, (rows, W_B), 0) // n_new
        ch = lax.broadcasted_iota(jnp.int32, (rows, W_B), 1) // hd
        qbd_ref[...] = jnp.where(rh == ch, qt, 0.0).astype(BF16)
        r_ref[...] = jnp.zeros(r_ref.shape, F32)
        own_ref[...] = jnp.zeros(own_ref.shape, BF16)
        own_ref[0, 0:n_new, :] = ko_ref[...].astype(BF16)
        own_ref[1, 0:n_new, :] = vo_ref[...].astype(BF16)
        qpos = lax.broadcasted_iota(jnp.int32, (rows, page), 0) % n_new
        kpos = lax.broadcasted_iota(jnp.int32, (rows, page), 1)
        z = _dot_nt(qbd_ref[...], own_ref[0])
        suffix, rs = _sb_softplus_sums(z, tri2, kpos < qpos)
        r_ref[...] = jnp.broadcast_to(rs, r_ref.shape)
        acc_ref[...] = _dot(_sb_weights(z, suffix, 0.0, kpos < qpos), own_ref[1])

    def lanes(refs_):
        return jnp.concatenate([r[...].reshape(nh * hd, page).astype(BF16) for r in refs_], axis=1)

    z = _dot(qbd_ref[...], lanes(k_refs))
    z = jnp.concatenate([z[:, i * page:(i + 1) * page] for i in range(pps)], axis=0)
    suffix, rs = _sb_softplus_sums(z, tri2, None)
    r = r_ref[...]
    r_parts = []
    for i in range(pps):
        r_parts.append(r)
        r = r + rs[i * rows:(i + 1) * rows]
    r_ref[...] = r
    a = _sb_weights(z, suffix, jnp.concatenate(r_parts, axis=0), None)
    a = jnp.concatenate([a[i * rows:(i + 1) * rows] for i in range(pps)], axis=1)
    acc_ref[...] += _dot_nt(a, lanes(v_refs))

    @pl.when(j == pl.num_programs(1) - 1)
    def _():
        acc = acc_ref[...]
        ch = lax.broadcasted_iota(jnp.int32, (n_new, W_B), 1) // hd
        out = jnp.zeros((n_new, W_B), F32)
        for h in range(nh):
            out = out + jnp.where(ch == h, acc[h * n_new:(h + 1) * n_new, :], 0.0)
        o_ref[...] = out.astype(BF16)


def _sb_sample(u3, cache_kt, cache_vt, page_table, layer, *, pps):
    db, n_new, _ = u3.shape
    _, _, nh, hd, page = cache_kt.shape
    n_pages = page_table.shape[1]
    assert nh == N_HEADS_B and hd == HEAD_DIM_B and n_new % SUBLANES == 0 and n_pages % pps == 0
    blk = (3 * W_A) // W_B

    def page_spec(i):
        return pl.BlockSpec((None, None, nh, hd, page),
                            lambda bi, j, pt: (layer, pt[bi, n_pages - 1 - (j * pps + i)], 0, 0, 0))

    new_spec = lambda k: pl.BlockSpec((None, n_new, W_B), lambda bi, j, pt: (bi, 0, blk + k))
    grid_spec = pltpu.PrefetchScalarGridSpec(
        num_scalar_prefetch=1,
        grid=(db, n_pages // pps),
        in_specs=[new_spec(0), new_spec(1), new_spec(2)] + [page_spec(i) for i in range(pps)] * 2,
        out_specs=pl.BlockSpec((None, n_new, W_B), lambda bi, j, pt: (bi, 0, 0)),
        scratch_shapes=[
            pltpu.VMEM((nh * n_new, W_B), BF16),
            pltpu.VMEM((nh * n_new, W_B), F32),
            pltpu.VMEM((nh * n_new, LANES), F32),
            pltpu.VMEM((2, page, W_B), BF16),
        ],
    )
    return pl.pallas_call(
        functools.partial(_sb_sample_kernel, n_new=n_new, page=page, pps=pps, scale=hd ** -0.5),
        grid_spec=grid_spec,
        out_shape=jax.ShapeDtypeStruct((db, n_new, W_B), BF16),
        compiler_params=_cparams("parallel", "arbitrary"),
        name="stickbreak_sample",
    )(page_table, u3, u3, u3, *([cache_kt] * pps), *([cache_vt] * pps))


def _dot3(a_split, b_split):
    a_hi, a_lo = a_split
    b_hi, b_lo = b_split
    return _dot(jnp.concatenate([a_hi, a_hi, a_lo], axis=1), jnp.concatenate([b_hi, b_lo, b_hi], axis=0))


def _delta_kernel(q_ref, k_ref, v_ref, z_ref, bd_ref, buf_ref, cw_ref, gp_ref, nc_ref, s0_ref,
                  y_ref, s_ref, p_ref, *, tin, tt, chunk, width):
    h = pl.program_id(1)
    t = pl.program_id(2)
    lo = SUBLANES - (width - 1)
    dk = DK_C

    @pl.when(t == 0)
    def _():
        if tin < tt:
            p_ref[...] = jnp.zeros(p_ref.shape, F32)
        for i in range(3):
            p_ref[i, lo:SUBLANES, :] = buf_ref[i]
        s_ref[...] = s0_ref[...]

    @pl.when(t > 0)
    def _():
        for i in range(3):
            p_ref[i, 0:SUBLANES, :] = p_ref[i, tt:tt + SUBLANES, :]

    conv = []
    for i, ref in enumerate((q_ref, k_ref, v_ref)):
        p_ref[i, SUBLANES:SUBLANES + tin, :] = ref[...]
        y = p_ref[i, lo:lo + tt, :] * cw_ref[i, 0:1, :]
        for w in range(1, width):
            y = y + p_ref[i, lo + w:lo + w + tt, :] * cw_ref[i, w:w + 1, :]
        conv.append(y * _sigmoid(y))
    qc, kc, vc = conv
    qn = qc * lax.rsqrt(jnp.sum(qc * qc, axis=-1, keepdims=True) + L2_EPS) * (dk ** -0.5)
    kn = kc * lax.rsqrt(jnp.sum(kc * kc, axis=-1, keepdims=True) + L2_EPS)

    neg_a = -jnp.exp(gp_ref[0:1, 0:1])
    dt_b = gp_ref[1:2, 0:1]
    lane = lax.broadcasted_iota(jnp.int32, (tin, LANES), 1)
    bd = bd_ref[...]
    b_col = jnp.sum(jnp.where(lane == h, bd, 0.0), axis=-1, keepdims=True)
    a_col = jnp.sum(jnp.where(lane == h + N_HEADS_C, bd, 0.0), axis=-1, keepdims=True)
    beta_col = _sigmoid(b_col)
    g_col = neg_a * _softplus(a_col + dt_b)
    if tin < tt:
        pad = jnp.zeros((tt - tin, 1), F32)
        beta_col = jnp.concatenate([beta_col, pad], axis=0)
        g_col = jnp.concatenate([g_col, pad], axis=0)

    ri = lax.broadcasted_iota(jnp.int32, (chunk, chunk), 0)
    ci = lax.broadcasted_iota(jnp.int32, (chunk, chunk), 1)
    incl = ri >= ci
    strict = ri > ci
    eye = jnp.where(ri == ci, 1.0, 0.0)
    n_doubling = (min(tin, chunk) - 1).bit_length() - 1

    def prepare(c):
        sl = slice(c * chunk, (c + 1) * chunk)
        qk, kk_, vk, gc_col, bc = qn[sl], kn[sl], vc[sl], g_col[sl], beta_col[sl]
        g_lanes = jnp.transpose(jnp.broadcast_to(gc_col, (chunk, chunk)))
        gcum_col = jnp.sum(jnp.where(incl, g_lanes, 0.0), axis=1, keepdims=True)
        gcum_row = jnp.sum(jnp.where(ri <= ci, gc_col, 0.0), axis=0, keepdims=True)
        dec_incl = jnp.where(incl, jnp.exp(jnp.where(incl, gcum_col - gcum_row, 0.0)), 0.0)
        k_bf = kk_.astype(BF16)
        e_col = jnp.exp(gcum_col)
        g_last = gcum_col[chunk - 1:chunk, :]
        return dict(
            m=bc * _dot_nt(k_bf, k_bf) * jnp.where(strict, dec_incl, 0.0),
            rhs=_split_hi_lo(jnp.concatenate([kk_ * (bc * e_col), vk * bc], axis=-1)),
            aqk=(_dot_nt(qk.astype(BF16), k_bf) * dec_incl).astype(BF16),
            q_dec=(qk * e_col).astype(BF16),
            k_dec=(kk_ * jnp.exp(g_last - gcum_col)).astype(BF16),
            g_end=jnp.exp(g_last))

    pre = [prepare(c) for c in range(tt // chunk)]
    pw_s = [_split_hi_lo(-p["m"]) for p in pre]
    inv = [eye - p["m"] for p in pre]
    for _ in range(n_doubling):
        pw_s = [_split_hi_lo(_dot3(s_, s_)) for s_ in pw_s]
        inv = [iv + _dot3(_split_hi_lo(iv), s_) for iv, s_ in zip(inv, pw_s)]
    sols = [_dot3(_split_hi_lo(iv), p["rhs"]) for iv, p in zip(inv, pre)]

    outs = []
    for p, sol in zip(pre, sols):
        s = s_ref[...]
        s_bf = s.astype(BF16)
        u_bf = (sol[:, dk:] - _dot(sol[:, :dk].astype(BF16), s_bf)).astype(BF16)
        outs.append(_dot(p["q_dec"], s_bf) + _dot(p["aqk"], u_bf))
        s_ref[...] = p["g_end"] * s + _dot_tn(p["k_dec"], u_bf)
    o = (jnp.concatenate(outs, axis=0) if len(outs) > 1 else outs[0])[:tin]
    zg = z_ref[...]
    y_ref[...] = (_rms_scale(o) * nc_ref[...] * (zg * _sigmoid(zg))).astype(BF16)


def _delta(u3, buf, conv_w, gate_par, norm_c, s0, *, tin, tt, chunk):
    b, t, _ = u3.shape
    width = conv_w.shape[0]
    assert t % tin == 0 and tt % chunk == 0 and (tin == tt or t == tin) and tin >= width - 1
    nh = N_HEADS_C
    qb = (3 * W_A + 3 * W_B) // LANES
    zb = (3 * W_A + 3 * W_B + W_QKV_C) // LANES
    bdb = COL_BD // LANES
    cw3 = conv_w.reshape(width, 3, nh, LANES).transpose(1, 2, 0, 3)
    buf3 = buf.reshape(b, width - 1, 3, nh, LANES).transpose(0, 3, 2, 1, 4)
    col = lambda k: pl.BlockSpec((None, tin, LANES), lambda bi, hi, ti: (bi, ti, k + hi))
    return pl.pallas_call(
        functools.partial(_delta_kernel, tin=tin, tt=tt, chunk=chunk, width=width),
        grid=(b, nh, t // tin),
        in_specs=[
            col(qb), col(qb + nh), col(qb + 2 * nh), col(zb),
            pl.BlockSpec((None, tin, LANES), lambda bi, hi, ti: (bi, ti, bdb)),
            pl.BlockSpec((None, None, 3, width - 1, LANES), lambda bi, hi, ti: (bi, hi, 0, 0, 0)),
            pl.BlockSpec((3, None, width, LANES), lambda bi, hi, ti: (0, hi, 0, 0)),
            pl.BlockSpec((None, 2, LANES), lambda bi, hi, ti: (hi, 0, 0)),
            pl.BlockSpec((1, DV_C), lambda bi, hi, ti: (0, 0)),
            pl.BlockSpec((None, None, DK_C, DV_C), lambda bi, hi, ti: (bi, hi, 0, 0)),
        ],
        out_specs=[
            pl.BlockSpec((None, tin, LANES), lambda bi, hi, ti: (bi, ti, hi)),
            pl.BlockSpec((None, None, DK_C, DV_C), lambda bi, hi, ti: (bi, hi, 0, 0)),
        ],
        out_shape=[
            jax.ShapeDtypeStruct((b, t, nh * DV_C), BF16),
            jax.ShapeDtypeStruct((b, nh, DK_C, DV_C), F32),
        ],
        scratch_shapes=[pltpu.VMEM((3, tt + SUBLANES, LANES), F32)],
        compiler_params=_cparams("parallel", "parallel", "arbitrary"),
        name="gated_delta",
    )(u3, u3, u3, u3, u3, buf3, cw3, gate_par, norm_c, s0)


def _merge_kernel(x_ref, a_ref, b_ref, c_ref, g0_ref, g1_ref, g2_ref, wa_ref, wb_ref, wc_ref, wo_ref, o_ref):
    merged = (_sigmoid(g0_ref[...]) * _dot(a_ref[...], wa_ref[...])
              + _sigmoid(g1_ref[...]) * _dot(b_ref[...], wb_ref[...])
              + _sigmoid(g2_ref[...]) * _dot(c_ref[...], wc_ref[...]))
    o_ref[...] = x_ref[...] + _dot(merged.astype(BF16), wo_ref[...])


def _merge(x, act_a, act_b, act_c, u, wa, wb, wc, wo, *, tm):
    m, d = x.shape
    gb = COL_GATES // d
    row = lambda width, k=0: pl.BlockSpec((tm, width), lambda i: (i, k))
    full = lambda w: pl.BlockSpec(w.shape, lambda i: (0, 0))
    return pl.pallas_call(
        _merge_kernel,
        grid=(m // tm,),
        in_specs=[
            row(d), row(act_a.shape[1]), row(act_b.shape[1]), row(act_c.shape[1]),
            row(d, gb), row(d, gb + 1), row(d, gb + 2),
            full(wa), full(wb), full(wc), full(wo),
        ],
        out_specs=row(d),
        out_shape=jax.ShapeDtypeStruct((m, d), F32),
        compiler_params=_cparams("parallel"),
        name="merge_out_proj",
    )(x, act_a, act_b, act_c, u, u, u, wa, wb, wc, wo)


def _finish(x_ref, total, gf_ref, o_ref, final_norm):
    y = x_ref[...] + total
    if final_norm:
        y = _rms_scale(y) * gf_ref[...]
    o_ref[...] = y


def _ffn_kernel(x_ref, g_ref, gf_ref, wg_ref, wu_ref, wd_ref, o_ref, hn_ref, acc_ref, *, final_norm):
    f = pl.program_id(1)

    @pl.when(f == 0)
    def _():
        hn_ref[...] = (_rms_scale(x_ref[...]) * g_ref[...]).astype(BF16)
        acc_ref[...] = jnp.zeros(acc_ref.shape, F32)

    hn = hn_ref[...]
    a = _dot(hn, wg_ref[...])
    hidden = (a * _sigmoid(a) * _dot(hn, wu_ref[...])).astype(BF16)
    acc_ref[...] += _dot(hidden, wd_ref[...])

    @pl.when(f == pl.num_programs(1) - 1)
    def _():
        _finish(x_ref, acc_ref[...], gf_ref, o_ref, final_norm)


def _ffn(x, g, gf, wg, wu, wd, *, tm, tf, final_norm):
    m, d = x.shape
    ff = wg.shape[1]
    return pl.pallas_call(
        functools.partial(_ffn_kernel, final_norm=final_norm),
        grid=(m // tm, ff // tf),
        in_specs=[
            pl.BlockSpec((tm, d), lambda i, f: (i, 0)),
            pl.BlockSpec((1, d), lambda i, f: (0, 0)),
            pl.BlockSpec((1, d), lambda i, f: (0, 0)),
            pl.BlockSpec((d, tf), lambda i, f: (0, f)),
            pl.BlockSpec((d, tf), lambda i, f: (0, f)),
            pl.BlockSpec((tf, d), lambda i, f: (f, 0)),
        ],
        out_specs=pl.BlockSpec((tm, d), lambda i, f: (i, 0)),
        out_shape=jax.ShapeDtypeStruct((m, d), F32),
        scratch_shapes=[pltpu.VMEM((tm, d), BF16), pltpu.VMEM((tm, d), F32)],
        compiler_params=_cparams("parallel", "arbitrary"),
        name="dense_swiglu",
    )(x, g, gf, wg, wu, wd)


def _top2_gates(logits):
    lane = lax.broadcasted_iota(jnp.int32, logits.shape, 1).astype(F32)
    neg = -jnp.inf
    lg = jnp.where(lane < N_EXPERTS, logits, neg)
    m1 = jnp.max(lg, axis=-1, keepdims=True)
    i1 = jnp.min(jnp.where(lg == m1, lane, float(LANES)), axis=-1, keepdims=True)
    lg2 = jnp.where(lane == i1, neg, lg)
    m2 = jnp.max(lg2, axis=-1, keepdims=True)
    i2 = jnp.min(jnp.where(lg2 == m2, lane, float(LANES)), axis=-1, keepdims=True)
    e2 = jnp.exp(m2 - m1)
    w1 = 1.0 / (1.0 + e2)
    return jnp.where(lane == i1, w1, jnp.where(lane == i2, e2 * w1, 0.0))


def _moe_kernel(x_ref, g_ref, gf_ref, wr_ref, wg_ref, wu_ref, wd_ref, o_ref, hn_ref, gate_ref, acc_ref, tot_ref,
                *, final_norm):
    e = pl.program_id(1)
    f = pl.program_id(2)
    last_f = pl.num_programs(2) - 1

    @pl.when((e == 0) & (f == 0))
    def _():
        hn = _rms_scale(x_ref[...]) * g_ref[...]
        hn_ref[...] = hn.astype(BF16)
        gate_ref[...] = _top2_gates(_dot_hi(hn, wr_ref[...]))
        tot_ref[...] = jnp.zeros(tot_ref.shape, F32)

    @pl.when(f == 0)
    def _():
        acc_ref[...] = jnp.zeros(acc_ref.shape, F32)

    hn = hn_ref[...]
    a = _dot(hn, wg_ref[...])
    hidden = (a * _sigmoid(a) * _dot(hn, wu_ref[...])).astype(BF16)
    acc_ref[...] += _dot(hidden, wd_ref[...])

    @pl.when(f == last_f)
    def _():
        lane = lax.broadcasted_iota(jnp.int32, gate_ref.shape, 1)
        gate_e = jnp.sum(jnp.where(lane == e, gate_ref[...], 0.0), axis=-1, keepdims=True)
        tot_ref[...] += gate_e * acc_ref[...]

    @pl.when((e == pl.num_programs(1) - 1) & (f == last_f))
    def _():
        _finish(x_ref, tot_ref[...], gf_ref, o_ref, final_norm)


def _moe(x, g, gf, w_router, wg, wu, wd, *, tm, tf, final_norm):
    m, d = x.shape
    ne, _, ff = wg.shape
    return pl.pallas_call(
        functools.partial(_moe_kernel, final_norm=final_norm),
        grid=(m // tm, ne, ff // tf),
        in_specs=[
            pl.BlockSpec((tm, d), lambda i, e, f: (i, 0)),
            pl.BlockSpec((1, d), lambda i, e, f: (0, 0)),
            pl.BlockSpec((1, d), lambda i, e, f: (0, 0)),
            pl.BlockSpec((d, LANES), lambda i, e, f: (0, 0)),
            pl.BlockSpec((None, d, tf), lambda i, e, f: (e, 0, f)),
            pl.BlockSpec((None, d, tf), lambda i, e, f: (e, 0, f)),
            pl.BlockSpec((None, tf, d), lambda i, e, f: (e, f, 0)),
        ],
        out_specs=pl.BlockSpec((tm, d), lambda i, e, f: (i, 0)),
        out_shape=jax.ShapeDtypeStruct((m, d), F32),
        scratch_shapes=[
            pltpu.VMEM((tm, d), BF16), pltpu.VMEM((tm, LANES), F32),
            pltpu.VMEM((tm, d), F32), pltpu.VMEM((tm, d), F32),
        ],
        compiler_params=_cparams("parallel", "arbitrary", "arbitrary"),
        name="moe_swiglu",
    )(x, g, gf, w_router, wg, wu, wd)


def _tile(n, pref):
    return pref if n % pref == 0 else n


def _trunk(x3, attend, bufs_a, bufs_c, states, p):
    b, t, d = x3.shape
    m = b * t
    depth = p["w_proj"].shape[0]
    x = x3.reshape(m, d)
    tm = _tile(m, 1024)
    ks, vs, bas, bcs, ss = [], [], [], [], []
    for l in range(depth):
        u = _norm_matmul(x, p["norm_mix"][l], p["w_proj"][l], p["b_proj"][l], tm=tm, tn=1152)
        u3 = u.reshape(b, t, N_PROJ)
        act_a, nbuf_a = _mixer_a(u3, bufs_a[l], p["conv_a_w"][l], tt=_tile(t, 512))
        act_b = attend(u3, l)
        tin = min(t, 512)
        act_c, s_new = _delta(u3, bufs_c[l], p["conv_c_w"][l], p["gate_par"][l], p["norm_c"][l], states[l],
                              tin=tin, tt=max(tin, DELTA_CHUNK), chunk=DELTA_CHUNK)
        act_c = act_c.reshape(m, -1)
        x = _merge(x, act_a.reshape(m, -1), act_b.reshape(m, -1), act_c, u,
                   p["w_br_a"][l], p["w_br_b"][l], p["w_br_c"][l], p["w_out"][l], tm=_tile(m, 256))
        final = l == depth - 1
        if l % 2 == 0:
            x = _ffn(x, p["norm_ffn"][l], p["norm_final"], p["w_ffn_gate"][l // 2], p["w_ffn_up"][l // 2],
                     p["w_ffn_down"][l // 2], tm=tm, tf=256, final_norm=final)
        else:
            x = _moe(x, p["norm_ffn"][l], p["norm_final"], p["w_router"][l // 2], p["w_exp_gate"][l // 2],
                     p["w_exp_up"][l // 2], p["w_exp_down"][l // 2], tm=tm, tf=512, final_norm=final)
        ks.append(u3[:, :, 3 * W_A + W_B:3 * W_A + 2 * W_B].reshape(b, t, N_HEADS_B, HEAD_DIM_B))
        vs.append(u3[:, :, 3 * W_A + 2 * W_B:3 * W_A + 3 * W_B].reshape(b, t, N_HEADS_B, HEAD_DIM_B))
        bas.append(nbuf_a)
        wc = p["conv_c_w"].shape[1]
        bcs.append(u3[:, t - (wc - 1):, 3 * W_A + 3 * W_B:3 * W_A + 3 * W_B + W_QKV_C])
        ss.append(s_new)
    return (x.reshape(b, t, d), jnp.stack(ks), jnp.stack(vs), jnp.stack(bas), jnp.stack(bcs), jnp.stack(ss))


def kernel(x_prompt, x_sample, cache_k, cache_v, page_table, state_conv_a, state_conv_c, state_delta, norm_mix, w_in, b_in, conv_a_w, conv_c_w, a_log, dt_bias, norm_c, w_br_a, w_br_b, w_br_c, w_out, norm_ffn, w_ffn_gate, w_ffn_up, w_ffn_down, w_router, w_exp_gate, w_exp_up, w_exp_down, norm_final):
    depth, d, n_in = w_in.shape
    assert n_in == N_MAIN + 2 * N_HEADS_C + N_GATES
    col_gate_src = N_MAIN + 2 * N_HEADS_C
    pad = N_PROJ - n_in

    def reorder(a):
        return jnp.concatenate(
            [a[..., :N_MAIN], a[..., col_gate_src:], a[..., N_MAIN:col_gate_src],
             jnp.zeros(a.shape[:-1] + (pad,), a.dtype)], axis=-1)

    p = {
        "norm_mix": norm_mix[:, None, :],
        "w_proj": reorder(w_in).astype(BF16),
        "b_proj": reorder(b_in)[:, None, :],
        "conv_a_w": conv_a_w,
        "conv_c_w": conv_c_w,
        "gate_par": jnp.broadcast_to(jnp.stack([a_log, dt_bias], axis=-1)[..., None], (depth, N_HEADS_C, 2, LANES)),
        "norm_c": norm_c[:, None, :],
        "w_br_a": w_br_a.astype(BF16), "w_br_b": w_br_b.astype(BF16), "w_br_c": w_br_c.astype(BF16),
        "w_out": w_out.astype(BF16),
        "norm_ffn": norm_ffn[:, None, :],
        "norm_final": norm_final[None, :],
        "w_ffn_gate": w_ffn_gate.astype(BF16), "w_ffn_up": w_ffn_up.astype(BF16),
        "w_ffn_down": w_ffn_down.astype(BF16),
        "w_router": jnp.pad(w_router, ((0, 0), (0, 0), (0, LANES - w_router.shape[-1]))),
        "w_exp_gate": w_exp_gate.astype(BF16), "w_exp_up": w_exp_up.astype(BF16),
        "w_exp_down": w_exp_down.astype(BF16),
    }

    bp = x_prompt.shape[0]
    zero_a = jnp.zeros((depth, bp) + state_conv_a.shape[2:], F32)
    zero_c = jnp.zeros((depth, bp) + state_conv_c.shape[2:], F32)
    zero_s = jnp.zeros((depth, bp) + state_delta.shape[2:], F32)
    y_p, k_p, v_p, ca_p, cc_p, s_p = _trunk(
        x_prompt, lambda u3, l: _sb_prompt(u3, tq=512), zero_a, zero_c, zero_s, p)

    cache_kt = jnp.transpose(cache_k, (0, 1, 3, 4, 2))
    cache_vt = jnp.transpose(cache_v, (0, 1, 3, 4, 2))
    y_s, k_s, v_s, ca_s, cc_s, s_s = _trunk(
        x_sample, lambda u3, l: _sb_sample(u3, cache_kt, cache_vt, page_table, l, pps=8),
        state_conv_a, state_conv_c, state_delta, p)
    return (y_p, y_s, k_p, v_p, k_s, v_s, ca_p, ca_s, cc_p, cc_s, s_p, s_s)
```

```python
import functools

import jax
import jax.numpy as jnp
from jax import lax
from jax.experimental import pallas as pl
from jax.experimental.pallas import tpu as pltpu

F32 = jnp.float32
BF16 = jnp.bfloat16

RMS_EPS = 1e-6
L2_EPS = 1e-6

N_HEADS_B = 8
HEAD_DIM_B = 64
N_HEADS_C = 8
DK_C = 128
DV_C = 128
N_EXPERTS = 8
DELTA_CHUNK = 128
MOE_ROWS = 128

LANES = 128
SUBLANES = 8
VMEM_LIMIT_BYTES = 56 * 1024 * 1024

W_A = 512
W_B = 512
W_QKV_C = 3072
W_Z_C = 1024
N_MAIN = 3 * W_A + 3 * W_B + W_QKV_C + W_Z_C
N_GATES = 3072
COL_GATES = N_MAIN
COL_BD = N_MAIN + N_GATES
N_PROJ = COL_BD + LANES


def _cparams(*sem):
    return pltpu.CompilerParams(dimension_semantics=sem, vmem_limit_bytes=VMEM_LIMIT_BYTES)


def _sigmoid(x):
    return 1.0 / (1.0 + jnp.exp(-x))


def _softplus(x):
    return jnp.maximum(x, 0.0) + jnp.log1p(jnp.exp(-jnp.abs(x)))


def _dot(a, b):
    return jnp.dot(a, b, preferred_element_type=F32)


def _dot_nt(a, b):
    return lax.dot_general(a, b, (((1,), (1,)), ((), ())), preferred_element_type=F32)


def _dot_tn(a, b):
    return lax.dot_general(a, b, (((0,), (0,)), ((), ())), preferred_element_type=F32)


def _dot_hi(a, b):
    return jnp.dot(a, b, preferred_element_type=F32, precision=lax.Precision.HIGHEST)


def _rms_scale(x):
    return x * lax.rsqrt(jnp.mean(x * x, axis=-1, keepdims=True) + RMS_EPS)


def _split_hi_lo(x):
    bits = lax.bitcast_convert_type(x, jnp.uint32) & jnp.uint32(0xFFFF0000)
    hi = lax.bitcast_convert_type(bits, F32)
    return hi.astype(BF16), (x - hi).astype(BF16)


def _norm_matmul_kernel(x_ref, g_ref, w_ref, b_ref, o_ref, xn_ref):
    @pl.when(pl.program_id(1) == 0)
    def _():
        xn_ref[...] = (_rms_scale(x_ref[...]) * g_ref[...]).astype(BF16)

    o_ref[...] = _dot(xn_ref[...], w_ref[...]) + b_ref[...]


def _norm_matmul(x, g, w, b, *, tm, tn):
    m, d = x.shape
    n = w.shape[1]
    return pl.pallas_call(
        _norm_matmul_kernel,
        grid=(m // tm, n // tn),
        in_specs=[
            pl.BlockSpec((tm, d), lambda i, j: (i, 0)),
            pl.BlockSpec((1, d), lambda i, j: (0, 0)),
            pl.BlockSpec((d, tn), lambda i, j: (0, j)),
            pl.BlockSpec((1, tn), lambda i, j: (0, j)),
        ],
        out_specs=pl.BlockSpec((tm, tn), lambda i, j: (i, j)),
        out_shape=jax.ShapeDtypeStruct((m, n), F32),
        scratch_shapes=[pltpu.VMEM((tm, d), BF16)],
        compiler_params=_cparams("parallel", "arbitrary"),
        name="norm_in_proj",
    )(x, g, w, b)


def _mixer_a_kernel(h_ref, gb_ref, gc_ref, buf_ref, w_ref, act_ref, nbuf_ref, p_ref, *, tt, width):
    t = pl.program_id(1)
    lo = SUBLANES - (width - 1)

    @pl.when(t == 0)
    def _():
        p_ref[lo:SUBLANES, :] = buf_ref[...]

    @pl.when(t > 0)
    def _():
        p_ref[0:SUBLANES, :] = p_ref[tt:tt + SUBLANES, :]

    p_ref[SUBLANES:SUBLANES + tt, :] = gc_ref[...] * h_ref[...]
    y = p_ref[lo:lo + tt, :] * w_ref[0:1, :]
    for i in range(1, width):
        y = y + p_ref[lo + i:lo + i + tt, :] * w_ref[i:i + 1, :]
    act_ref[...] = (gb_ref[...] * y).astype(BF16)
    nbuf_ref[...] = p_ref[SUBLANES + tt - (width - 1):SUBLANES + tt, :]


def _mixer_a(u3, buf, w, *, tt):
    b, t, _ = u3.shape
    width, c = w.shape
    assert t % tt == 0 and t >= width - 1 and c == W_A
    col = lambda k: pl.BlockSpec((None, tt, c), lambda bi, ti: (bi, ti, k))
    return pl.pallas_call(
        functools.partial(_mixer_a_kernel, tt=tt, width=width),
        grid=(b, t // tt),
        in_specs=[
            col(0), col(1), col(2),
            pl.BlockSpec((None, width - 1, c), lambda bi, ti: (bi, 0, 0)),
            pl.BlockSpec((width, c), lambda bi, ti: (0, 0)),
        ],
        out_specs=[
            pl.BlockSpec((None, tt, c), lambda bi, ti: (bi, ti, 0)),
            pl.BlockSpec((None, width - 1, c), lambda bi, ti: (bi, 0, 0)),
        ],
        out_shape=[
            jax.ShapeDtypeStruct((b, t, c), BF16),
            jax.ShapeDtypeStruct((b, width - 1, c), F32),
        ],
        scratch_shapes=[pltpu.VMEM((tt + SUBLANES, c), F32)],
        compiler_params=_cparams("parallel", "arbitrary"),
        name="mixer_a_conv",
    )(u3, u3, u3, buf, w)


def _suffix_sum_matrix(n, passes):
    s = lax.broadcasted_iota(jnp.int32, (passes * n, n), 0)
    s = jnp.where(s >= n, s - n, s)
    j = lax.broadcasted_iota(jnp.int32, (passes * n, n), 1)
    return jnp.where(s >= j, 1.0, 0.0).astype(BF16)


def _sb_softplus_sums(z, tri, mask):
    neg_abs = lax.bitcast_convert_type(lax.bitcast_convert_type(z, jnp.uint32) | jnp.uint32(0x80000000), F32)
    sp = jnp.maximum(z, 0.0) + jnp.log(1.0 + jnp.exp(neg_abs))
    if mask is not None:
        sp = jnp.where(mask, sp, 0.0)
    if tri.shape[0] == 2 * tri.shape[1]:
        addends = jnp.concatenate(_split_hi_lo(sp), axis=1)
    else:
        addends = sp.astype(BF16)
    return _dot(addends, tri), jnp.sum(sp, axis=-1, keepdims=True)


def _sb_weights(z, suffix, r_run, mask):
    a = jnp.exp(z - suffix - r_run)
    if mask is not None:
        a = jnp.where(mask, a, 0.0)
    return a.astype(BF16)


def _sb_prompt_kernel(q_ref, k_ref, v_ref, o_ref, kb_ref, vb_ref, acc_ref, r_ref, qh_ref, tri_ref,
                      za_ref, zb_ref, aa_ref, ab_ref, *, tq, scale):
    qi = pl.program_id(2)
    z_refs, a_refs = (za_ref, zb_ref), (aa_ref, ab_ref)
    kbn = LANES

    @pl.when(qi == 0)
    def _():
        kb_ref[...] = k_ref[...].astype(BF16)
        vb_ref[...] = v_ref[...].astype(BF16)

    lane = lax.broadcasted_iota(jnp.int32, (tq, LANES), 1)
    first = lane < HEAD_DIM_B
    q = q_ref[...] * scale
    qh_ref[0] = jnp.where(first, q, 0.0).astype(BF16)
    qh_ref[1] = jnp.where(first, 0.0, q).astype(BF16)
    tri_ref[...] = _suffix_sum_matrix(kbn, 1)
    row = lax.broadcasted_iota(jnp.int32, (tq, kbn), 0)
    col = lax.broadcasted_iota(jnp.int32, (tq, kbn), 1)
    acc_ref[...] = jnp.zeros(acc_ref.shape, F32)
    r_ref[...] = jnp.zeros(r_ref.shape, F32)

    def logits(kb2):
        k2 = kb_ref[pl.ds(pl.multiple_of(kb2 * 2 * kbn, 2 * kbn), 2 * kbn), :]
        pieces = []
        for h in range(2):
            z = _dot_nt(qh_ref[h], k2)
            pieces += [z[:, kbn:], z[:, :kbn]]
        return jnp.concatenate(pieces, axis=0)

    def weights(z, masks):
        mask = None if masks is None else jnp.concatenate(list(masks) * 2, axis=0)
        suffix, rs = _sb_softplus_sums(z, tri_ref[...], mask)
        r_parts = []
        for h in range(2):
            r_in = r_ref[h]
            r_mid = r_in + rs[2 * h * tq:(2 * h + 1) * tq]
            r_parts += [r_in, r_mid]
            r_ref[h] = r_mid + rs[(2 * h + 1) * tq:(2 * h + 2) * tq]
        return _sb_weights(z, suffix, jnp.concatenate(r_parts, axis=0), mask)

    def accumulate(a, kb2):
        v2 = vb_ref[pl.ds(pl.multiple_of(kb2 * 2 * kbn, 2 * kbn), 2 * kbn), :]
        for h in range(2):
            a_h = jnp.concatenate([a[(2 * h + 1) * tq:(2 * h + 2) * tq], a[2 * h * tq:(2 * h + 1) * tq]], axis=1)
            acc_ref[h] += _dot(a_h, v2)

    n_diag = tq // (2 * kbn)
    assert n_diag % 2 == 0
    for d in reversed(range(n_diag)):
        a = weights(logits(qi * n_diag + d), (col + (2 * d + 1) * kbn < row, col + 2 * d * kbn < row))
        accumulate(a, qi * n_diag + d)

    n_off = qi * n_diag
    blk = lambda s: jnp.clip(n_off - 1 - s, 0, jnp.maximum(n_off - 1, 0))

    @pl.when(n_off > 0)
    def _():
        z_refs[0][...] = logits(blk(0))
        a_refs[1][...] = jnp.zeros(a_refs[1].shape, BF16)

    def trip(j, carry):
        for half in range(2):
            s = 2 * j + half
            cur, nxt = half, 1 - half
            z_refs[nxt][...] = logits(blk(s + 1))
            accumulate(a_refs[nxt][...], blk(s - 1))
            a_refs[cur][...] = weights(z_refs[cur][...], None)
        return carry

    lax.fori_loop(0, n_off // 2, trip, 0)

    @pl.when(n_off > 0)
    def _():
        accumulate(a_refs[1][...], blk(n_off - 1))
    o_ref[...] = jnp.where(first, acc_ref[0], acc_ref[1]).astype(BF16)


def _sb_prompt(u3, *, tq):
    b, t, _ = u3.shape
    assert t % tq == 0 and tq % (2 * LANES) == 0
    pairs = W_B // LANES
    q_blk, k_blk, v_blk = (3 * W_A) // LANES, (3 * W_A + W_B) // LANES, (3 * W_A + 2 * W_B) // LANES
    return pl.pallas_call(
        functools.partial(_sb_prompt_kernel, tq=tq, scale=HEAD_DIM_B ** -0.5),
        grid=(b, pairs, t // tq),
        in_specs=[
            pl.BlockSpec((None, tq, LANES), lambda bi, hp, qi: (bi, qi, q_blk + hp)),
            pl.BlockSpec((None, t, LANES), lambda bi, hp, qi: (bi, 0, k_blk + hp)),
            pl.BlockSpec((None, t, LANES), lambda bi, hp, qi: (bi, 0, v_blk + hp)),
        ],
        out_specs=pl.BlockSpec((None, tq, LANES), lambda bi, hp, qi: (bi, qi, hp)),
        out_shape=jax.ShapeDtypeStruct((b, t, W_B), BF16),
        scratch_shapes=[
            pltpu.VMEM((t, LANES), BF16), pltpu.VMEM((t, LANES), BF16),
            pltpu.VMEM((2, tq, LANES), F32), pltpu.VMEM((2, tq, LANES), F32),
            pltpu.VMEM((2, tq, LANES), BF16), pltpu.VMEM((LANES, LANES), BF16),
            pltpu.VMEM((4 * tq, LANES), F32), pltpu.VMEM((4 * tq, LANES), F32),
            pltpu.VMEM((4 * tq, LANES), BF16), pltpu.VMEM((4 * tq, LANES), BF16),
        ],
        compiler_params=_cparams("parallel", "parallel", "arbitrary"),
        name="stickbreak_prompt",
    )(u3, u3, u3)


def _sb_sample_kernel(pt_ref, q_ref, ko_ref, vo_ref, *refs, n_new, page, pps, scale):
    del pt_ref
    k_refs, v_refs = refs[:pps], refs[pps:2 * pps]
    o_ref, qbd_ref, acc_ref, r_ref, own_ref = refs[2 * pps:]
    j = pl.program_id(1)
    nh, hd = N_HEADS_B, HEAD_DIM_B
    rows = nh * n_new
    tri2 = _suffix_sum_matrix(page, 2)

    @pl.when(j == 0)
    def _():
        qt = jnp.concatenate([q_ref[...] * scale] * nh, axis=0)
        rh = lax.broadcasted_iota(jnp.int32, (rows, W_B), 0) // n_new
        ch = lax.broadcasted_iota(jnp.int32, (rows, W_B), 1) // hd
        qbd_ref[...] = jnp.where(rh == ch, qt, 0.0).astype(BF16)
        r_ref[...] = jnp.zeros(r_ref.shape, F32)
        own_ref[...] = jnp.zeros(own_ref.shape, BF16)
        own_ref[0, 0:n_new, :] = ko_ref[...].astype(BF16)
        own_ref[1, 0:n_new, :] = vo_ref[...].astype(BF16)
        qpos = lax.broadcasted_iota(jnp.int32, (rows, page), 0) % n_new
        kpos = lax.broadcasted_iota(jnp.int32, (rows, page), 1)
        z = _dot_nt(qbd_ref[...], own_ref[0])
        suffix, rs = _sb_softplus_sums(z, tri2, kpos < qpos)
        r_ref[...] = jnp.broadcast_to(rs, r_ref.shape)
        acc_ref[...] = _dot(_sb_weights(z, suffix, 0.0, kpos < qpos), own_ref[1])

    def lanes(refs_):
        return jnp.concatenate([r[...].reshape(nh * hd, page).astype(BF16) for r in refs_], axis=1)

    z = _dot(qbd_ref[...], lanes(k_refs))
    z = jnp.concatenate([z[:, i * page:(i + 1) * page] for i in range(pps)], axis=0)
    suffix, rs = _sb_softplus_sums(z, tri2, None)
    r = r_ref[...]
    r_parts = []
    for i in range(pps):
        r_parts.append(r)
        r = r + rs[i * rows:(i + 1) * rows]
    r_ref[...] = r
    a = _sb_weights(z, suffix, jnp.concatenate(r_parts, axis=0), None)
    a = jnp.concatenate([a[i * rows:(i + 1) * rows] for i in range(pps)], axis=1)
    acc_ref[...] += _dot_nt(a, lanes(v_refs))

    @pl.when(j == pl.num_programs(1) - 1)
    def _():
        acc = acc_ref[...]
        ch = lax.broadcasted_iota(jnp.int32, (n_new, W_B), 1) // hd
        out = jnp.zeros((n_new, W_B), F32)
        for h in range(nh):
            out = out + jnp.where(ch == h, acc[h * n_new:(h + 1) * n_new, :], 0.0)
        o_ref[...] = out.astype(BF16)


def _sb_sample(u3, cache_kt, cache_vt, page_table, layer, *, pps):
    db, n_new, _ = u3.shape
    _, _, nh, hd, page = cache_kt.shape
    n_pages = page_table.shape[1]
    assert nh == N_HEADS_B and hd == HEAD_DIM_B and n_new % SUBLANES == 0 and n_pages % pps == 0
    blk = (3 * W_A) // W_B

    def page_spec(i):
        return pl.BlockSpec((None, None, nh, hd, page),
                            lambda bi, j, pt: (layer, pt[bi, n_pages - 1 - (j * pps + i)], 0, 0, 0))

    new_spec = lambda k: pl.BlockSpec((None, n_new, W_B), lambda bi, j, pt: (bi, 0, blk + k))
    grid_spec = pltpu.PrefetchScalarGridSpec(
        num_scalar_prefetch=1,
        grid=(db, n_pages // pps),
        in_specs=[new_spec(0), new_spec(1), new_spec(2)] + [page_spec(i) for i in range(pps)] * 2,
        out_specs=pl.BlockSpec((None, n_new, W_B), lambda bi, j, pt: (bi, 0, 0)),
        scratch_shapes=[
            pltpu.VMEM((nh * n_new, W_B), BF16),
            pltpu.VMEM((nh * n_new, W_B), F32),
            pltpu.VMEM((nh * n_new, LANES), F32),
            pltpu.VMEM((2, page, W_B), BF16),
        ],
    )
    return pl.pallas_call(
        functools.partial(_sb_sample_kernel, n_new=n_new, page=page, pps=pps, scale=hd ** -0.5),
        grid_spec=grid_spec,
        out_shape=jax.ShapeDtypeStruct((db, n_new, W_B), BF16),
        compiler_params=_cparams("parallel", "arbitrary"),
        name="stickbreak_sample",
    )(page_table, u3, u3, u3, *([cache_kt] * pps), *([cache_vt] * pps))


def _dot3(a_split, b_split):
    a_hi, a_lo = a_split
    b_hi, b_lo = b_split
    return _dot(jnp.concatenate([a_hi, a_hi, a_lo], axis=1), jnp.concatenate([b_hi, b_lo, b_hi], axis=0))


def _delta_kernel(q_ref, k_ref, v_ref, z_ref, bd_ref, buf_ref, cw_ref, gp_ref, nc_ref, s0_ref,
                  y_ref, s_ref, p_ref, *, tin, tt, chunk, width, hps):
    hg = pl.program_id(1)
    t = pl.program_id(2)
    lo = SUBLANES - (width - 1)
    dk = DK_C

    @pl.when(t == 0)
    def _():
        if tin < tt:
            p_ref[...] = jnp.zeros(p_ref.shape, F32)
        for hh in range(hps):
            for i in range(3):
                p_ref[hh, i, lo:SUBLANES, :] = buf_ref[hh, i]
        s_ref[...] = s0_ref[...]

    @pl.when(t > 0)
    def _():
        for hh in range(hps):
            for i in range(3):
                p_ref[hh, i, 0:SUBLANES, :] = p_ref[hh, i, tt:tt + SUBLANES, :]

    lane = lax.broadcasted_iota(jnp.int32, (tin, LANES), 1)
    bd = bd_ref[...]

    def head_inputs(hh):
        cols = slice(hh * LANES, (hh + 1) * LANES)
        conv = []
        for i, ref in enumerate((q_ref, k_ref, v_ref)):
            p_ref[hh, i, SUBLANES:SUBLANES + tin, :] = ref[:, cols]
            y = p_ref[hh, i, lo:lo + tt, :] * cw_ref[i, hh, 0:1, :]
            for w in range(1, width):
                y = y + p_ref[hh, i, lo + w:lo + w + tt, :] * cw_ref[i, hh, w:w + 1, :]
            conv.append(y * _sigmoid(y))
        qc, kc, vc = conv
        qn = qc * lax.rsqrt(jnp.sum(qc * qc, axis=-1, keepdims=True) + L2_EPS) * (dk ** -0.5)
        kn = kc * lax.rsqrt(jnp.sum(kc * kc, axis=-1, keepdims=True) + L2_EPS)
        h = hg * hps + hh
        neg_a = -jnp.exp(gp_ref[hh, 0:1, 0:1])
        dt_b = gp_ref[hh, 1:2, 0:1]
        b_col = jnp.sum(jnp.where(lane == h, bd, 0.0), axis=-1, keepdims=True)
        a_col = jnp.sum(jnp.where(lane == h + N_HEADS_C, bd, 0.0), axis=-1, keepdims=True)
        beta_col = _sigmoid(b_col)
        g_col = neg_a * _softplus(a_col + dt_b)
        if tin < tt:
            pad = jnp.zeros((tt - tin, 1), F32)
            beta_col = jnp.concatenate([beta_col, pad], axis=0)
            g_col = jnp.concatenate([g_col, pad], axis=0)
        return qn, kn, vc, g_col, beta_col

    ri = lax.broadcasted_iota(jnp.int32, (chunk, chunk), 0)
    ci = lax.broadcasted_iota(jnp.int32, (chunk, chunk), 1)
    incl = ri >= ci
    strict = ri > ci
    eye = jnp.where(ri == ci, 1.0, 0.0)
    n_doubling = (min(tin, chunk) - 1).bit_length() - 1

    def prepare(inputs, c):
        sl = slice(c * chunk, (c + 1) * chunk)
        qk, kk_, vk, gc_col, bc = (a[sl] for a in inputs)
        g_lanes = jnp.transpose(jnp.broadcast_to(gc_col, (chunk, chunk)))
        gcum_col = jnp.sum(jnp.where(incl, g_lanes, 0.0), axis=1, keepdims=True)
        gcum_row = jnp.sum(jnp.where(ri <= ci, gc_col, 0.0), axis=0, keepdims=True)
        dec_incl = jnp.where(incl, jnp.exp(jnp.where(incl, gcum_col - gcum_row, 0.0)), 0.0)
        k_bf = kk_.astype(BF16)
        e_col = jnp.exp(gcum_col)
        g_last = gcum_col[chunk - 1:chunk, :]
        return dict(
            m=bc * _dot_nt(k_bf, k_bf) * jnp.where(strict, dec_incl, 0.0),
            rhs=_split_hi_lo(jnp.concatenate([kk_ * (bc * e_col), vk * bc], axis=-1)),
            aqk=(_dot_nt(qk.astype(BF16), k_bf) * dec_incl).astype(BF16),
            q_dec=(qk * e_col).astype(BF16),
            k_dec=(kk_ * jnp.exp(g_last - gcum_col)).astype(BF16),
            g_end=jnp.exp(g_last))

    n_chunks = tt // chunk
    pre = []
    for hh in range(hps):
        inputs = head_inputs(hh)
        pre += [prepare(inputs, c) for c in range(n_chunks)]
    pw_s = [_split_hi_lo(-p["m"]) for p in pre]
    inv = [eye - p["m"] for p in pre]
    for _ in range(n_doubling):
        pw_s = [_split_hi_lo(_dot3(s_, s_)) for s_ in pw_s]
        inv = [iv + _dot3(_split_hi_lo(iv), s_) for iv, s_ in zip(inv, pw_s)]
    sols = [_dot3(_split_hi_lo(iv), p["rhs"]) for iv, p in zip(inv, pre)]

    outs = [[] for _ in range(hps)]
    for c in range(n_chunks):
        for hh in range(hps):
            p, sol = pre[hh * n_chunks + c], sols[hh * n_chunks + c]
            s = s_ref[hh]
            s_bf = s.astype(BF16)
            u_bf = (sol[:, dk:] - _dot(sol[:, :dk].astype(BF16), s_bf)).astype(BF16)
            outs[hh].append(_dot(p["q_dec"], s_bf) + _dot(p["aqk"], u_bf))
            s_ref[hh] = p["g_end"] * s + _dot_tn(p["k_dec"], u_bf)
    for hh in range(hps):
        cols = slice(hh * LANES, (hh + 1) * LANES)
        o = (jnp.concatenate(outs[hh], axis=0) if n_chunks > 1 else outs[hh][0])[:tin]
        zg = z_ref[:, cols]
        y_ref[:, cols] = (_rms_scale(o) * nc_ref[...] * (zg * _sigmoid(zg))).astype(BF16)


def _delta(u3, buf, conv_w, gate_par, norm_c, s0, *, tin, tt, chunk, hps):
    b, t, _ = u3.shape
    width = conv_w.shape[0]
    nh = N_HEADS_C
    assert t % tin == 0 and tt % chunk == 0 and (tin == tt or t == tin) and tin >= width - 1 and nh % hps == 0
    qb = (3 * W_A + 3 * W_B) // LANES
    zb = (3 * W_A + 3 * W_B + W_QKV_C) // LANES
    bdb = COL_BD // LANES
    assert qb % hps == 0 and zb % hps == 0
    cw3 = conv_w.reshape(width, 3, nh, LANES).transpose(1, 2, 0, 3)
    buf3 = buf.reshape(b, width - 1, 3, nh, LANES).transpose(0, 3, 2, 1, 4)
    col = lambda k: pl.BlockSpec((None, tin, hps * LANES), lambda bi, hi, ti: (bi, ti, k // hps + hi))
    return pl.pallas_call(
        functools.partial(_delta_kernel, tin=tin, tt=tt, chunk=chunk, width=width, hps=hps),
        grid=(b, nh // hps, t // tin),
        in_specs=[
            col(qb), col(qb + nh), col(qb + 2 * nh), col(zb),
            pl.BlockSpec((None, tin, LANES), lambda bi, hi, ti: (bi, ti, bdb)),
            pl.BlockSpec((None, hps, 3, width - 1, LANES), lambda bi, hi, ti: (bi, hi, 0, 0, 0)),
            pl.BlockSpec((3, hps, width, LANES), lambda bi, hi, ti: (0, hi, 0, 0)),
            pl.BlockSpec((hps, 2, LANES), lambda bi, hi, ti: (hi, 0, 0)),
            pl.BlockSpec((1, DV_C), lambda bi, hi, ti: (0, 0)),
            pl.BlockSpec((None, hps, DK_C, DV_C), lambda bi, hi, ti: (bi, hi, 0, 0)),
        ],
        out_specs=[
            pl.BlockSpec((None, tin, hps * LANES), lambda bi, hi, ti: (bi, ti, hi)),
            pl.BlockSpec((None, hps, DK_C, DV_C), lambda bi, hi, ti: (bi, hi, 0, 0)),
        ],
        out_shape=[
            jax.ShapeDtypeStruct((b, t, nh * DV_C), BF16),
            jax.ShapeDtypeStruct((b, nh, DK_C, DV_C), F32),
        ],
        scratch_shapes=[pltpu.VMEM((hps, 3, tt + SUBLANES, LANES), F32)],
        compiler_params=_cparams("parallel", "parallel", "arbitrary"),
        name="gated_delta",
    )(u3, u3, u3, u3, u3, buf3, cw3, gate_par, norm_c, s0)


def _merge_kernel(x_ref, a_ref, b_ref, c_ref, g0_ref, g1_ref, g2_ref, wa_ref, wb_ref, wc_ref, wo_ref, o_ref):
    merged = (_sigmoid(g0_ref[...]) * _dot(a_ref[...], wa_ref[...])
              + _sigmoid(g1_ref[...]) * _dot(b_ref[...], wb_ref[...])
              + _sigmoid(g2_ref[...]) * _dot(c_ref[...], wc_ref[...]))
    o_ref[...] = x_ref[...] + _dot(merged.astype(BF16), wo_ref[...])


def _merge(x, act_a, act_b, act_c, u, wa, wb, wc, wo, *, tm):
    m, d = x.shape
    gb = COL_GATES // d
    row = lambda width, k=0: pl.BlockSpec((tm, width), lambda i: (i, k))
    full = lambda w: pl.BlockSpec(w.shape, lambda i: (0, 0))
    return pl.pallas_call(
        _merge_kernel,
        grid=(m // tm,),
        in_specs=[
            row(d), row(act_a.shape[1]), row(act_b.shape[1]), row(act_c.shape[1]),
            row(d, gb), row(d, gb + 1), row(d, gb + 2),
            full(wa), full(wb), full(wc), full(wo),
        ],
        out_specs=row(d),
        out_shape=jax.ShapeDtypeStruct((m, d), F32),
        compiler_params=_cparams("parallel"),
        name="merge_out_proj",
    )(x, act_a, act_b, act_c, u, u, u, wa, wb, wc, wo)


def _finish(x_ref, total, gf_ref, o_ref, final_norm):
    y = x_ref[...] + total
    if final_norm:
        y = _rms_scale(y) * gf_ref[...]
    o_ref[...] = y


def _ffn_kernel(x_ref, g_ref, gf_ref, wg_ref, wu_ref, wd_ref, o_ref, hn_ref, acc_ref, *, final_norm):
    f = pl.program_id(1)

    @pl.when(f == 0)
    def _():
        hn_ref[...] = (_rms_scale(x_ref[...]) * g_ref[...]).astype(BF16)
        acc_ref[...] = jnp.zeros(acc_ref.shape, F32)

    hn = hn_ref[...]
    a = _dot(hn, wg_ref[...])
    hidden = (a * _sigmoid(a) * _dot(hn, wu_ref[...])).astype(BF16)
    acc_ref[...] += _dot(hidden, wd_ref[...])

    @pl.when(f == pl.num_programs(1) - 1)
    def _():
        _finish(x_ref, acc_ref[...], gf_ref, o_ref, final_norm)


def _ffn(x, g, gf, wg, wu, wd, *, tm, tf, final_norm):
    m, d = x.shape
    ff = wg.shape[1]
    return pl.pallas_call(
        functools.partial(_ffn_kernel, final_norm=final_norm),
        grid=(m // tm, ff // tf),
        in_specs=[
            pl.BlockSpec((tm, d), lambda i, f: (i, 0)),
            pl.BlockSpec((1, d), lambda i, f: (0, 0)),
            pl.BlockSpec((1, d), lambda i, f: (0, 0)),
            pl.BlockSpec((d, tf), lambda i, f: (0, f)),
            pl.BlockSpec((d, tf), lambda i, f: (0, f)),
            pl.BlockSpec((tf, d), lambda i, f: (f, 0)),
        ],
        out_specs=pl.BlockSpec((tm, d), lambda i, f: (i, 0)),
        out_shape=jax.ShapeDtypeStruct((m, d), F32),
        scratch_shapes=[pltpu.VMEM((tm, d), BF16), pltpu.VMEM((tm, d), F32)],
        compiler_params=_cparams("parallel", "arbitrary"),
        name="dense_swiglu",
    )(x, g, gf, wg, wu, wd)


def _top2_gates(logits):
    lane = lax.broadcasted_iota(jnp.int32, logits.shape, 1).astype(F32)
    neg = -jnp.inf
    lg = jnp.where(lane < N_EXPERTS, logits, neg)
    m1 = jnp.max(lg, axis=-1, keepdims=True)
    i1 = jnp.min(jnp.where(lg == m1, lane, float(LANES)), axis=-1, keepdims=True)
    lg2 = jnp.where(lane == i1, neg, lg)
    m2 = jnp.max(lg2, axis=-1, keepdims=True)
    i2 = jnp.min(jnp.where(lg2 == m2, lane, float(LANES)), axis=-1, keepdims=True)
    e2 = jnp.exp(m2 - m1)
    w1 = 1.0 / (1.0 + e2)
    gate = jnp.where(lane == i1, w1, jnp.where(lane == i2, e2 * w1, 0.0))
    return gate, jnp.where(lane == i1, 1.0, jnp.where(lane == i2, 1.0, 0.0))


GATE, RANK, SEL, TABLE_ROWS = 0, N_EXPERTS, 2 * N_EXPERTS, 4 * N_EXPERTS


def _router_kernel(x_ref, g_ref, wr_ref, hn_ref, col_ref, row_ref, cnt_ref):
    tm = x_ref.shape[0]
    hn = _rms_scale(x_ref[...]) * g_ref[...]
    hn_ref[...] = hn.astype(BF16)
    gate, sel = _top2_gates(_dot_hi(hn, wr_ref[...]))
    gate_t = jnp.transpose(gate)[0:N_EXPERTS]
    sel_t = jnp.transpose(sel)[0:N_EXPERTS]
    earlier = jnp.where(lax.broadcasted_iota(jnp.int32, (tm, tm), 0) < lax.broadcasted_iota(jnp.int32, (tm, tm), 1),
                        1.0, 0.0).astype(BF16)
    rank_t = _dot(sel_t.astype(BF16), earlier)
    table = jnp.concatenate([gate_t, rank_t, sel_t, jnp.zeros((LANES - 3 * N_EXPERTS, tm), F32)], axis=0)
    row_ref[...] = table[0:TABLE_ROWS]
    col_ref[...] = jnp.transpose(table)
    cnt_ref[...] = jnp.broadcast_to(jnp.sum(sel, axis=0, keepdims=True), cnt_ref.shape).astype(jnp.int32)


def _router(x, g, w_router, *, tm):
    m, d = x.shape
    nt = m // tm
    return pl.pallas_call(
        _router_kernel,
        grid=(nt,),
        in_specs=[
            pl.BlockSpec((tm, d), lambda i: (i, 0)),
            pl.BlockSpec((1, d), lambda i: (0, 0)),
            pl.BlockSpec((d, LANES), lambda i: (0, 0)),
        ],
        out_specs=[
            pl.BlockSpec((tm, d), lambda i: (i, 0)),
            pl.BlockSpec((tm, LANES), lambda i: (i, 0)),
            pl.BlockSpec((None, TABLE_ROWS, tm), lambda i: (i, 0, 0)),
            pl.BlockSpec((None, SUBLANES, LANES), lambda i: (i, 0, 0)),
        ],
        out_shape=[
            jax.ShapeDtypeStruct((m, d), BF16),
            jax.ShapeDtypeStruct((m, LANES), F32),
            jax.ShapeDtypeStruct((nt, TABLE_ROWS, tm), F32),
            jax.ShapeDtypeStruct((nt, SUBLANES, LANES), jnp.int32),
        ],
        compiler_params=_cparams("parallel"),
        name="moe_router",
    )(x, g, w_router)


def _experts_kernel(cnt_ref, x_ref, hn_ref, col_ref, row_ref, gf_ref, wg_ref, wu_ref, wd_ref, o_ref,
                    hc_ref, yacc_ref, tot_ref, ecol_ref, *, rows, final_norm):
    i = pl.program_id(0)
    e = pl.program_id(1)
    f = pl.program_id(2)
    last_f = pl.num_programs(2) - 1
    tm = x_ref.shape[0]
    n_sub = tm // rows
    count = cnt_ref[i, e]

    @pl.when((e == 0) & (f == 0))
    def _():
        tot_ref[...] = jnp.zeros(tot_ref.shape, F32)

    @pl.when(f == last_f)
    def _():
        lane = lax.broadcasted_iota(jnp.int32, (tm, LANES), 1)
        col = col_ref[...]
        for n, k in enumerate((GATE, RANK, SEL)):
            ecol_ref[n] = jnp.broadcast_to(
                jnp.sum(jnp.where(lane == k + e, col, 0.0), axis=-1, keepdims=True), (tm, LANES))

    for sb in range(n_sub):
        @pl.when(sb * rows < count)
        def _(sb=sb):
            @pl.when(f == 0)
            def _():
                rank_row = row_ref[pl.ds(RANK + e, 1), :]
                sel_row = row_ref[pl.ds(SEL + e, 1), :]
                slot = (lax.broadcasted_iota(jnp.int32, (rows, tm), 0) + sb * rows).astype(F32)
                pick = jnp.where(rank_row == slot, sel_row, 0.0).astype(BF16)
                hc_ref[sb] = _dot(pick, hn_ref[...]).astype(BF16)
                yacc_ref[sb] = jnp.zeros((rows, yacc_ref.shape[2]), F32)

            hc = hc_ref[sb]
            a = _dot(hc, wg_ref[...])
            hidden = (a * _sigmoid(a) * _dot(hc, wu_ref[...])).astype(BF16)
            yacc_ref[sb] += _dot(hidden, wd_ref[...])

            @pl.when(f == last_f)
            def _():
                slot = (lax.broadcasted_iota(jnp.int32, (tm, rows), 1) + sb * rows).astype(F32)
                place = jnp.where(ecol_ref[1] == slot, ecol_ref[2], 0.0).astype(BF16)
                y_hi, y_lo = _split_hi_lo(yacc_ref[sb])
                back = _dot(jnp.concatenate([place, place], axis=1), jnp.concatenate([y_hi, y_lo], axis=0))
                tot_ref[...] += ecol_ref[0][:, 0:1] * back

    @pl.when((e == pl.num_programs(1) - 1) & (f == last_f))
    def _():
        _finish(x_ref, tot_ref[...], gf_ref, o_ref, final_norm)


def _moe(x, g, gf, w_router, wg, wu, wd, *, tm, tf, final_norm):
    m, d = x.shape
    ne, _, ff = wg.shape
    rows = MOE_ROWS
    assert rows == LANES and tm % rows == 0
    hn, col, row, cnt = _router(x, g, w_router, tm=tm)
    grid_spec = pltpu.PrefetchScalarGridSpec(
        num_scalar_prefetch=1,
        grid=(m // tm, ne, ff // tf),
        in_specs=[
            pl.BlockSpec((tm, d), lambda i, e, f, c: (i, 0)),
            pl.BlockSpec((tm, d), lambda i, e, f, c: (i, 0)),
            pl.BlockSpec((tm, LANES), lambda i, e, f, c: (i, 0)),
            pl.BlockSpec((None, TABLE_ROWS, tm), lambda i, e, f, c: (i, 0, 0)),
            pl.BlockSpec((1, d), lambda i, e, f, c: (0, 0)),
            pl.BlockSpec((None, d, tf), lambda i, e, f, c: (e, 0, f)),
            pl.BlockSpec((None, d, tf), lambda i, e, f, c: (e, 0, f)),
            pl.BlockSpec((None, tf, d), lambda i, e, f, c: (e, f, 0)),
        ],
        out_specs=pl.BlockSpec((tm, d), lambda i, e, f, c: (i, 0)),
        scratch_shapes=[
            pltpu.VMEM((tm // rows, rows, d), BF16), pltpu.VMEM((tm // rows, rows, d), F32),
            pltpu.VMEM((tm, d), F32), pltpu.VMEM((3, tm, LANES), F32),
        ],
    )
    return pl.pallas_call(
        functools.partial(_experts_kernel, rows=rows, final_norm=final_norm),
        grid_spec=grid_spec,
        out_shape=jax.ShapeDtypeStruct((m, d), F32),
        compiler_params=_cparams("parallel", "arbitrary", "arbitrary"),
        name="moe_experts",
    )(cnt[:, 0, :ne], x, hn, col, row, gf, wg, wu, wd)


def _tile(n, pref):
    return pref if n % pref == 0 else n


def _trunk(x3, attend, bufs_a, bufs_c, states, p):
    b, t, d = x3.shape
    m = b * t
    depth = p["w_proj"].shape[0]
    x = x3.reshape(m, d)
    tm = _tile(m, 1024)
    ks, vs, bas, bcs, ss = [], [], [], [], []
    for l in range(depth):
        u = _norm_matmul(x, p["norm_mix"][l], p["w_proj"][l], p["b_proj"][l], tm=tm, tn=1152)
        u3 = u.reshape(b, t, N_PROJ)
        act_a, nbuf_a = _mixer_a(u3, bufs_a[l], p["conv_a_w"][l], tt=_tile(t, 512))
        act_b = attend(u3, l)
        tin = min(t, 512)
        act_c, s_new = _delta(u3, bufs_c[l], p["conv_c_w"][l], p["gate_par"][l], p["norm_c"][l], states[l],
                              tin=tin, tt=max(tin, DELTA_CHUNK), chunk=DELTA_CHUNK,
                              hps=2 if tin >= DELTA_CHUNK else N_HEADS_C)
        act_c = act_c.reshape(m, -1)
        x = _merge(x, act_a.reshape(m, -1), act_b.reshape(m, -1), act_c, u,
                   p["w_br_a"][l], p["w_br_b"][l], p["w_br_c"][l], p["w_out"][l], tm=_tile(m, 256))
        final = l == depth - 1
        if l % 2 == 0:
            x = _ffn(x, p["norm_ffn"][l], p["norm_final"], p["w_ffn_gate"][l // 2], p["w_ffn_up"][l // 2],
                     p["w_ffn_down"][l // 2], tm=tm, tf=256, final_norm=final)
        else:
            x = _moe(x, p["norm_ffn"][l], p["norm_final"], p["w_router"][l // 2], p["w_exp_gate"][l // 2],
                     p["w_exp_up"][l // 2], p["w_exp_down"][l // 2], tm=tm, tf=896, final_norm=final)
        ks.append(u3[:, :, 3 * W_A + W_B:3 * W_A + 2 * W_B].reshape(b, t, N_HEADS_B, HEAD_DIM_B))
        vs.append(u3[:, :, 3 * W_A + 2 * W_B:3 * W_A + 3 * W_B].reshape(b, t, N_HEADS_B, HEAD_DIM_B))
        bas.append(nbuf_a)
        wc = p["conv_c_w"].shape[1]
        bcs.append(u3[:, t - (wc - 1):, 3 * W_A + 3 * W_B:3 * W_A + 3 * W_B + W_QKV_C])
        ss.append(s_new)
    return (x.reshape(b, t, d), jnp.stack(ks), jnp.stack(vs), jnp.stack(bas), jnp.stack(bcs), jnp.stack(ss))


def kernel(x_prompt, x_sample, cache_k, cache_v, page_table, state_conv_a, state_conv_c, state_delta, norm_mix, w_in, b_in, conv_a_w, conv_c_w, a_log, dt_bias, norm_c, w_br_a, w_br_b, w_br_c, w_out, norm_ffn, w_ffn_gate, w_ffn_up, w_ffn_down, w_router, w_exp_gate, w_exp_up, w_exp_down, norm_final):
    depth, d, n_in = w_in.shape
    assert n_in == N_MAIN + 2 * N_HEADS_C + N_GATES
    col_gate_src = N_MAIN + 2 * N_HEADS_C
    pad = N_PROJ - n_in

    def reorder(a):
        return jnp.concatenate(
            [a[..., :N_MAIN], a[..., col_gate_src:], a[..., N_MAIN:col_gate_src],
             jnp.zeros(a.shape[:-1] + (pad,), a.dtype)], axis=-1)

    p = {
        "norm_mix": norm_mix[:, None, :],
        "w_proj": reorder(w_in).astype(BF16),
        "b_proj": reorder(b_in)[:, None, :],
        "conv_a_w": conv_a_w,
        "conv_c_w": conv_c_w,
        "gate_par": jnp.broadcast_to(jnp.stack([a_log, dt_bias], axis=-1)[..., None], (depth, N_HEADS_C, 2, LANES)),
        "norm_c": norm_c[:, None, :],
        "w_br_a": w_br_a.astype(BF16), "w_br_b": w_br_b.astype(BF16), "w_br_c": w_br_c.astype(BF16),
        "w_out": w_out.astype(BF16),
        "norm_ffn": norm_ffn[:, None, :],
        "norm_final": norm_final[None, :],
        "w_ffn_gate": w_ffn_gate.astype(BF16), "w_ffn_up": w_ffn_up.astype(BF16),
        "w_ffn_down": w_ffn_down.astype(BF16),
        "w_router": jnp.pad(w_router, ((0, 0), (0, 0), (0, LANES - w_router.shape[-1]))),
        "w_exp_gate": w_exp_gate.astype(BF16), "w_exp_up": w_exp_up.astype(BF16),
        "w_exp_down": w_exp_down.astype(BF16),
    }

    bp = x_prompt.shape[0]
    zero_a = jnp.zeros((depth, bp) + state_conv_a.shape[2:], F32)
    zero_c = jnp.zeros((depth, bp) + state_conv_c.shape[2:], F32)
    zero_s = jnp.zeros((depth, bp) + state_delta.shape[2:], F32)
    y_p, k_p, v_p, ca_p, cc_p, s_p = _trunk(
        x_prompt, lambda u3, l: _sb_prompt(u3, tq=512), zero_a, zero_c, zero_s, p)

    cache_kt = jnp.transpose(cache_k, (0, 1, 3, 4, 2))
    cache_vt = jnp.transpose(cache_v, (0, 1, 3, 4, 2))
    y_s, k_s, v_s, ca_s, cc_s, s_s = _trunk(
        x_sample, lambda u3, l: _sb_sample(u3, cache_kt, cache_vt, page_table, l, pps=8),
        state_conv_a, state_conv_c, state_delta, p)
    return (y_p, y_s, k_p, v_p, k_s, v_s, ca_p, ca_s, cc_p, cc_s, s_p, s_s)
```

```python
import functools

import jax
import jax.numpy as jnp
from jax import lax
from jax.experimental import pallas as pl
from jax.experimental.pallas import tpu as pltpu

F32 = jnp.float32
BF16 = jnp.bfloat16

RMS_EPS = 1e-6
L2_EPS = 1e-6

N_HEADS_B = 8
HEAD_DIM_B = 64
N_HEADS_C = 8
DK_C = 128
DV_C = 128
N_EXPERTS = 8
DELTA_CHUNK = 128
MOE_ROWS = 128

LANES = 128
SUBLANES = 8
VMEM_LIMIT_BYTES = 56 * 1024 * 1024

W_A = 512
W_B = 512
W_QKV_C = 3072
W_Z_C = 1024
N_MAIN = 3 * W_A + 3 * W_B + W_QKV_C + W_Z_C
N_GATES = 3072
COL_GATES = N_MAIN
COL_BD = N_MAIN + N_GATES
N_PROJ = COL_BD + LANES


def _cparams(*sem):
    return pltpu.CompilerParams(dimension_semantics=sem, vmem_limit_bytes=VMEM_LIMIT_BYTES)


def _sigmoid(x):
    return 1.0 / (1.0 + jnp.exp(-x))


def _softplus(x):
    return jnp.maximum(x, 0.0) + jnp.log1p(jnp.exp(-jnp.abs(x)))


def _dot(a, b):
    return jnp.dot(a, b, preferred_element_type=F32)


def _dot_nt(a, b):
    return lax.dot_general(a, b, (((1,), (1,)), ((), ())), preferred_element_type=F32)


def _dot_tn(a, b):
    return lax.dot_general(a, b, (((0,), (0,)), ((), ())), preferred_element_type=F32)


def _dot_hi(a, b):
    return jnp.dot(a, b, preferred_element_type=F32, precision=lax.Precision.HIGHEST)


def _rms_scale(x):
    return x * lax.rsqrt(jnp.mean(x * x, axis=-1, keepdims=True) + RMS_EPS)


def _split_hi_lo(x):
    bits = lax.bitcast_convert_type(x, jnp.uint32) & jnp.uint32(0xFFFF0000)
    hi = lax.bitcast_convert_type(bits, F32)
    return hi.astype(BF16), (x - hi).astype(BF16)


def _norm_matmul_kernel(x_ref, g_ref, w_ref, b_ref, o_ref, xn_ref):
    @pl.when(pl.program_id(1) == 0)
    def _():
        xn_ref[...] = (_rms_scale(x_ref[...]) * g_ref[...]).astype(BF16)

    o_ref[...] = _dot(xn_ref[...], w_ref[...]) + b_ref[...]


def _norm_matmul(x, g, w, b, *, tm, tn):
    m, d = x.shape
    n = w.shape[1]
    return pl.pallas_call(
        _norm_matmul_kernel,
        grid=(m // tm, n // tn),
        in_specs=[
            pl.BlockSpec((tm, d), lambda i, j: (i, 0)),
            pl.BlockSpec((1, d), lambda i, j: (0, 0)),
            pl.BlockSpec((d, tn), lambda i, j: (0, j)),
            pl.BlockSpec((1, tn), lambda i, j: (0, j)),
        ],
        out_specs=pl.BlockSpec((tm, tn), lambda i, j: (i, j)),
        out_shape=jax.ShapeDtypeStruct((m, n), F32),
        scratch_shapes=[pltpu.VMEM((tm, d), BF16)],
        compiler_params=_cparams("parallel", "arbitrary"),
        name="norm_in_proj",
    )(x, g, w, b)


def _mixer_a_kernel(h_ref, gb_ref, gc_ref, buf_ref, w_ref, act_ref, nbuf_ref, p_ref, *, tt, width):
    t = pl.program_id(1)
    lo = SUBLANES - (width - 1)

    @pl.when(t == 0)
    def _():
        p_ref[lo:SUBLANES, :] = buf_ref[...]

    @pl.when(t > 0)
    def _():
        p_ref[0:SUBLANES, :] = p_ref[tt:tt + SUBLANES, :]

    p_ref[SUBLANES:SUBLANES + tt, :] = gc_ref[...] * h_ref[...]
    y = p_ref[lo:lo + tt, :] * w_ref[0:1, :]
    for i in range(1, width):
        y = y + p_ref[lo + i:lo + i + tt, :] * w_ref[i:i + 1, :]
    act_ref[...] = (gb_ref[...] * y).astype(BF16)
    nbuf_ref[...] = p_ref[SUBLANES + tt - (width - 1):SUBLANES + tt, :]


def _mixer_a(u3, buf, w, *, tt):
    b, t, _ = u3.shape
    width, c = w.shape
    assert t % tt == 0 and t >= width - 1 and c == W_A
    col = lambda k: pl.BlockSpec((None, tt, c), lambda bi, ti: (bi, ti, k))
    return pl.pallas_call(
        functools.partial(_mixer_a_kernel, tt=tt, width=width),
        grid=(b, t // tt),
        in_specs=[
            col(0), col(1), col(2),
            pl.BlockSpec((None, width - 1, c), lambda bi, ti: (bi, 0, 0)),
            pl.BlockSpec((width, c), lambda bi, ti: (0, 0)),
        ],
        out_specs=[
            pl.BlockSpec((None, tt, c), lambda bi, ti: (bi, ti, 0)),
            pl.BlockSpec((None, width - 1, c), lambda bi, ti: (bi, 0, 0)),
        ],
        out_shape=[
            jax.ShapeDtypeStruct((b, t, c), BF16),
            jax.ShapeDtypeStruct((b, width - 1, c), F32),
        ],
        scratch_shapes=[pltpu.VMEM((tt + SUBLANES, c), F32)],
        compiler_params=_cparams("parallel", "arbitrary"),
        name="mixer_a_conv",
    )(u3, u3, u3, buf, w)


def _suffix_sum_matrix(n, passes):
    s = lax.broadcasted_iota(jnp.int32, (passes * n, n), 0)
    s = jnp.where(s >= n, s - n, s)
    j = lax.broadcasted_iota(jnp.int32, (passes * n, n), 1)
    return jnp.where(s >= j, 1.0, 0.0).astype(BF16)


def _sb_softplus_sums(z, tri, mask):
    neg_abs = lax.bitcast_convert_type(lax.bitcast_convert_type(z, jnp.uint32) | jnp.uint32(0x80000000), F32)
    sp = jnp.maximum(z, 0.0) + jnp.log(1.0 + jnp.exp(neg_abs))
    if mask is not None:
        sp = jnp.where(mask, sp, 0.0)
    if tri.shape[0] == 2 * tri.shape[1]:
        addends = jnp.concatenate(_split_hi_lo(sp), axis=1)
    else:
        addends = sp.astype(BF16)
    return _dot(addends, tri), jnp.sum(sp, axis=-1, keepdims=True)


def _sb_weights(z, suffix, r_run, mask):
    a = jnp.exp(z - suffix - r_run)
    if mask is not None:
        a = jnp.where(mask, a, 0.0)
    return a.astype(BF16)


def _sb_prompt_kernel(q_ref, k_ref, v_ref, o_ref, kb_ref, vb_ref, acc_ref, r_ref, qh_ref, tri_ref,
                      za_ref, zb_ref, aa_ref, ab_ref, *, tq, scale):
    qi = pl.program_id(2)
    z_refs, a_refs = (za_ref, zb_ref), (aa_ref, ab_ref)
    kbn = LANES

    @pl.when(qi == 0)
    def _():
        kb_ref[...] = k_ref[...].astype(BF16)
        vb_ref[...] = v_ref[...].astype(BF16)

    lane = lax.broadcasted_iota(jnp.int32, (tq, LANES), 1)
    first = lane < HEAD_DIM_B
    q = q_ref[...] * scale
    qh_ref[0] = jnp.where(first, q, 0.0).astype(BF16)
    qh_ref[1] = jnp.where(first, 0.0, q).astype(BF16)
    tri_ref[...] = _suffix_sum_matrix(kbn, 1)
    row = lax.broadcasted_iota(jnp.int32, (tq, kbn), 0)
    col = lax.broadcasted_iota(jnp.int32, (tq, kbn), 1)
    acc_ref[...] = jnp.zeros(acc_ref.shape, F32)
    r_ref[...] = jnp.zeros(r_ref.shape, F32)

    def logits(kb2):
        k2 = kb_ref[pl.ds(pl.multiple_of(kb2 * 2 * kbn, 2 * kbn), 2 * kbn), :]
        pieces = []
        for h in range(2):
            z = _dot_nt(qh_ref[h], k2)
            pieces += [z[:, kbn:], z[:, :kbn]]
        return jnp.concatenate(pieces, axis=0)

    def weights(z, masks):
        mask = None if masks is None else jnp.concatenate(list(masks) * 2, axis=0)
        suffix, rs = _sb_softplus_sums(z, tri_ref[...], mask)
        r_parts = []
        for h in range(2):
            r_in = r_ref[h]
            r_mid = r_in + rs[2 * h * tq:(2 * h + 1) * tq]
            r_parts += [r_in, r_mid]
            r_ref[h] = r_mid + rs[(2 * h + 1) * tq:(2 * h + 2) * tq]
        return _sb_weights(z, suffix, jnp.concatenate(r_parts, axis=0), mask)

    def accumulate(a, kb2):
        v2 = vb_ref[pl.ds(pl.multiple_of(kb2 * 2 * kbn, 2 * kbn), 2 * kbn), :]
        for h in range(2):
            a_h = jnp.concatenate([a[(2 * h + 1) * tq:(2 * h + 2) * tq], a[2 * h * tq:(2 * h + 1) * tq]], axis=1)
            acc_ref[h] += _dot(a_h, v2)

    assert tq == 4 * kbn
    n_all = 2 * (qi + 1)
    blk = lambda s: jnp.maximum(n_all - 1 - s, 0)

    def step(s, half, masks, first_step=False):
        cur, nxt = half, 1 - half
        z_refs[nxt][...] = logits(blk(s + 1))
        if not first_step:
            accumulate(a_refs[nxt][...], blk(s - 1))
        a_refs[cur][...] = weights(z_refs[cur][...], masks)

    z_refs[0][...] = logits(blk(0))
    step(0, 0, (col + 3 * kbn < row, col + 2 * kbn < row), first_step=True)
    step(1, 1, (col + kbn < row, col < row))

    def trip(j, carry):
        for half in range(2):
            step(2 * j + 2 + half, half, None)
        return carry

    lax.fori_loop(0, qi, trip, 0)
    accumulate(a_refs[1][...], blk(n_all - 1))
    o_ref[...] = jnp.where(first, acc_ref[0], acc_ref[1]).astype(BF16)


def _sb_prompt(u3, *, tq):
    b, t, _ = u3.shape
    assert t % tq == 0 and tq % (2 * LANES) == 0
    pairs = W_B // LANES
    q_blk, k_blk, v_blk = (3 * W_A) // LANES, (3 * W_A + W_B) // LANES, (3 * W_A + 2 * W_B) // LANES
    return pl.pallas_call(
        functools.partial(_sb_prompt_kernel, tq=tq, scale=HEAD_DIM_B ** -0.5),
        grid=(b, pairs, t // tq),
        in_specs=[
            pl.BlockSpec((None, tq, LANES), lambda bi, hp, qi: (bi, qi, q_blk + hp)),
            pl.BlockSpec((None, t, LANES), lambda bi, hp, qi: (bi, 0, k_blk + hp)),
            pl.BlockSpec((None, t, LANES), lambda bi, hp, qi: (bi, 0, v_blk + hp)),
        ],
        out_specs=pl.BlockSpec((None, tq, LANES), lambda bi, hp, qi: (bi, qi, hp)),
        out_shape=jax.ShapeDtypeStruct((b, t, W_B), BF16),
        scratch_shapes=[
            pltpu.VMEM((t, LANES), BF16), pltpu.VMEM((t, LANES), BF16),
            pltpu.VMEM((2, tq, LANES), F32), pltpu.VMEM((2, tq, LANES), F32),
            pltpu.VMEM((2, tq, LANES), BF16), pltpu.VMEM((LANES, LANES), BF16),
            pltpu.VMEM((4 * tq, LANES), F32), pltpu.VMEM((4 * tq, LANES), F32),
            pltpu.VMEM((4 * tq, LANES), BF16), pltpu.VMEM((4 * tq, LANES), BF16),
        ],
        compiler_params=_cparams("parallel", "parallel", "arbitrary"),
        name="stickbreak_prompt",
    )(u3, u3, u3)


def _sb_sample_kernel(pt_ref, q_ref, ko_ref, vo_ref, *refs, n_new, page, pps, scale):
    del pt_ref
    k_refs, v_refs = refs[:pps], refs[pps:2 * pps]
    o_ref, qbd_ref, acc_ref, r_ref, own_ref = refs[2 * pps:]
    j = pl.program_id(1)
    nh, hd = N_HEADS_B, HEAD_DIM_B
    rows = nh * n_new
    tri2 = _suffix_sum_matrix(page, 2)

    @pl.when(j == 0)
    def _():
        qt = jnp.concatenate([q_ref[...] * scale] * nh, axis=0)
        rh = lax.broadcasted_iota(jnp.int32, (rows, W_B), 0) // n_new
        ch = lax.broadcasted_iota(jnp.int32, (rows, W_B), 1) // hd
        qbd_ref[...] = jnp.where(rh == ch, qt, 0.0).astype(BF16)
        r_ref[...] = jnp.zeros(r_ref.shape, F32)
        own_ref[...] = jnp.zeros(own_ref.shape, BF16)
        own_ref[0, 0:n_new, :] = ko_ref[...].astype(BF16)
        own_ref[1, 0:n_new, :] = vo_ref[...].astype(BF16)
        qpos = lax.broadcasted_iota(jnp.int32, (rows, page), 0) % n_new
        kpos = lax.broadcasted_iota(jnp.int32, (rows, page), 1)
        z = _dot_nt(qbd_ref[...], own_ref[0])
        suffix, rs = _sb_softplus_sums(z, tri2, kpos < qpos)
        r_ref[...] = jnp.broadcast_to(rs, r_ref.shape)
        acc_ref[...] = _dot(_sb_weights(z, suffix, 0.0, kpos < qpos), own_ref[1])

    def lanes(refs_):
        return jnp.concatenate([r[...].reshape(nh * hd, page).astype(BF16) for r in refs_], axis=1)

    z = _dot(qbd_ref[...], lanes(k_refs))
    z = jnp.concatenate([z[:, i * page:(i + 1) * page] for i in range(pps)], axis=0)
    suffix, rs = _sb_softplus_sums(z, tri2, None)
    r = r_ref[...]
    r_parts = []
    for i in range(pps):
        r_parts.append(r)
        r = r + rs[i * rows:(i + 1) * rows]
    r_ref[...] = r
    a = _sb_weights(z, suffix, jnp.concatenate(r_parts, axis=0), None)
    a = jnp.concatenate([a[i * rows:(i + 1) * rows] for i in range(pps)], axis=1)
    acc_ref[...] += _dot_nt(a, lanes(v_refs))

    @pl.when(j == pl.num_programs(1) - 1)
    def _():
        acc = acc_ref[...]
        ch = lax.broadcasted_iota(jnp.int32, (n_new, W_B), 1) // hd
        out = jnp.zeros((n_new, W_B), F32)
        for h in range(nh):
            out = out + jnp.where(ch == h, acc[h * n_new:(h + 1) * n_new, :], 0.0)
        o_ref[...] = out.astype(BF16)


def _sb_sample(u3, cache_kt, cache_vt, page_table, layer, *, pps):
    db, n_new, _ = u3.shape
    _, _, nh, hd, page = cache_kt.shape
    n_pages = page_table.shape[1]
    assert nh == N_HEADS_B and hd == HEAD_DIM_B and n_new % SUBLANES == 0 and n_pages % pps == 0
    blk = (3 * W_A) // W_B

    def page_spec(i):
        return pl.BlockSpec((None, None, nh, hd, page),
                            lambda bi, j, pt: (layer, pt[bi, n_pages - 1 - (j * pps + i)], 0, 0, 0))

    new_spec = lambda k: pl.BlockSpec((None, n_new, W_B), lambda bi, j, pt: (bi, 0, blk + k))
    grid_spec = pltpu.PrefetchScalarGridSpec(
        num_scalar_prefetch=1,
        grid=(db, n_pages // pps),
        in_specs=[new_spec(0), new_spec(1), new_spec(2)] + [page_spec(i) for i in range(pps)] * 2,
        out_specs=pl.BlockSpec((None, n_new, W_B), lambda bi, j, pt: (bi, 0, 0)),
        scratch_shapes=[
            pltpu.VMEM((nh * n_new, W_B), BF16),
            pltpu.VMEM((nh * n_new, W_B), F32),
            pltpu.VMEM((nh * n_new, LANES), F32),
            pltpu.VMEM((2, page, W_B), BF16),
        ],
    )
    return pl.pallas_call(
        functools.partial(_sb_sample_kernel, n_new=n_new, page=page, pps=pps, scale=hd ** -0.5),
        grid_spec=grid_spec,
        out_shape=jax.ShapeDtypeStruct((db, n_new, W_B), BF16),
        compiler_params=_cparams("parallel", "arbitrary"),
        name="stickbreak_sample",
    )(page_table, u3, u3, u3, *([cache_kt] * pps), *([cache_vt] * pps))


def _dot3(a_split, b_split):
    a_hi, a_lo = a_split
    b_hi, b_lo = b_split
    return _dot(jnp.concatenate([a_hi, a_hi, a_lo], axis=1), jnp.concatenate([b_hi, b_lo, b_hi], axis=0))


def _delta_kernel(q_ref, k_ref, v_ref, z_ref, bd_ref, buf_ref, cw_ref, gp_ref, nc_ref, s0_ref,
                  y_ref, s_ref, p_ref, *, tin, tt, chunk, width, hps):
    hg = pl.program_id(1)
    t = pl.program_id(2)
    lo = SUBLANES - (width - 1)
    dk = DK_C

    @pl.when(t == 0)
    def _():
        if tin < tt:
            p_ref[...] = jnp.zeros(p_ref.shape, F32)
        for hh in range(hps):
            for i in range(3):
                p_ref[hh, i, lo:SUBLANES, :] = buf_ref[hh, i]
        s_ref[...] = s0_ref[...]

    @pl.when(t > 0)
    def _():
        for hh in range(hps):
            for i in range(3):
                p_ref[hh, i, 0:SUBLANES, :] = p_ref[hh, i, tt:tt + SUBLANES, :]

    lane = lax.broadcasted_iota(jnp.int32, (tin, LANES), 1)
    bd = bd_ref[...]

    def head_inputs(hh):
        cols = slice(hh * LANES, (hh + 1) * LANES)
        conv = []
        for i, ref in enumerate((q_ref, k_ref, v_ref)):
            p_ref[hh, i, SUBLANES:SUBLANES + tin, :] = ref[:, cols]
            y = p_ref[hh, i, lo:lo + tt, :] * cw_ref[i, hh, 0:1, :]
            for w in range(1, width):
                y = y + p_ref[hh, i, lo + w:lo + w + tt, :] * cw_ref[i, hh, w:w + 1, :]
            conv.append(y * _sigmoid(y))
        qc, kc, vc = conv
        qn = qc * lax.rsqrt(jnp.sum(qc * qc, axis=-1, keepdims=True) + L2_EPS) * (dk ** -0.5)
        kn = kc * lax.rsqrt(jnp.sum(kc * kc, axis=-1, keepdims=True) + L2_EPS)
        h = hg * hps + hh
        neg_a = -jnp.exp(gp_ref[hh, 0:1, 0:1])
        dt_b = gp_ref[hh, 1:2, 0:1]
        b_col = jnp.sum(jnp.where(lane == h, bd, 0.0), axis=-1, keepdims=True)
        a_col = jnp.sum(jnp.where(lane == h + N_HEADS_C, bd, 0.0), axis=-1, keepdims=True)
        beta_col = _sigmoid(b_col)
        g_col = neg_a * _softplus(a_col + dt_b)
        if tin < tt:
            pad = jnp.zeros((tt - tin, 1), F32)
            beta_col = jnp.concatenate([beta_col, pad], axis=0)
            g_col = jnp.concatenate([g_col, pad], axis=0)
        return qn, kn, vc, g_col, beta_col

    ri = lax.broadcasted_iota(jnp.int32, (chunk, chunk), 0)
    ci = lax.broadcasted_iota(jnp.int32, (chunk, chunk), 1)
    incl = ri >= ci
    strict = ri > ci
    eye = jnp.where(ri == ci, 1.0, 0.0)
    n_doubling = (min(tin, chunk) - 1).bit_length() - 1

    def prepare(inputs, c):
        sl = slice(c * chunk, (c + 1) * chunk)
        qk, kk_, vk, gc_col, bc = (a[sl] for a in inputs)
        g_lanes = jnp.transpose(jnp.broadcast_to(gc_col, (chunk, chunk)))
        gcum_col = jnp.sum(jnp.where(incl, g_lanes, 0.0), axis=1, keepdims=True)
        gcum_row = jnp.sum(jnp.where(ri <= ci, gc_col, 0.0), axis=0, keepdims=True)
        dec_incl = jnp.where(incl, jnp.exp(jnp.where(incl, gcum_col - gcum_row, 0.0)), 0.0)
        k_bf = kk_.astype(BF16)
        e_col = jnp.exp(gcum_col)
        g_last = gcum_col[chunk - 1:chunk, :]
        return dict(
            m=bc * _dot_nt(k_bf, k_bf) * jnp.where(strict, dec_incl, 0.0),
            rhs=_split_hi_lo(jnp.concatenate([kk_ * (bc * e_col), vk * bc], axis=-1)),
            aqk=(_dot_nt(qk.astype(BF16), k_bf) * dec_incl).astype(BF16),
            q_dec=(qk * e_col).astype(BF16),
            k_dec=(kk_ * jnp.exp(g_last - gcum_col)).astype(BF16),
            g_end=jnp.exp(g_last))

    n_chunks = tt // chunk
    pre = []
    for hh in range(hps):
        inputs = head_inputs(hh)
        pre += [prepare(inputs, c) for c in range(n_chunks)]
    pw_s = [_split_hi_lo(-p["m"]) for p in pre]
    inv = [eye - p["m"] for p in pre]
    for _ in range(n_doubling):
        pw_s = [_split_hi_lo(_dot3(s_, s_)) for s_ in pw_s]
        inv = [iv + _dot3(_split_hi_lo(iv), s_) for iv, s_ in zip(inv, pw_s)]
    sols = [_dot3(_split_hi_lo(iv), p["rhs"]) for iv, p in zip(inv, pre)]

    outs = [[] for _ in range(hps)]
    for c in range(n_chunks):
        for hh in range(hps):
            p, sol = pre[hh * n_chunks + c], sols[hh * n_chunks + c]
            s = s_ref[hh]
            s_bf = s.astype(BF16)
            u_bf = (sol[:, dk:] - _dot(sol[:, :dk].astype(BF16), s_bf)).astype(BF16)
            outs[hh].append(_dot(p["q_dec"], s_bf) + _dot(p["aqk"], u_bf))
            s_ref[hh] = p["g_end"] * s + _dot_tn(p["k_dec"], u_bf)
    for hh in range(hps):
        cols = slice(hh * LANES, (hh + 1) * LANES)
        o = (jnp.concatenate(outs[hh], axis=0) if n_chunks > 1 else outs[hh][0])[:tin]
        zg = z_ref[:, cols]
        y_ref[:, cols] = (_rms_scale(o) * nc_ref[...] * (zg * _sigmoid(zg))).astype(BF16)


def _delta(u3, buf, conv_w, gate_par, norm_c, s0, *, tin, tt, chunk, hps):
    b, t, _ = u3.shape
    width = conv_w.shape[0]
    nh = N_HEADS_C
    assert t % tin == 0 and tt % chunk == 0 and (tin == tt or t == tin) and tin >= width - 1 and nh % hps == 0
    qb = (3 * W_A + 3 * W_B) // LANES
    zb = (3 * W_A + 3 * W_B + W_QKV_C) // LANES
    bdb = COL_BD // LANES
    assert qb % hps == 0 and zb % hps == 0
    cw3 = conv_w.reshape(width, 3, nh, LANES).transpose(1, 2, 0, 3)
    buf3 = buf.reshape(b, width - 1, 3, nh, LANES).transpose(0, 3, 2, 1, 4)
    col = lambda k: pl.BlockSpec((None, tin, hps * LANES), lambda bi, hi, ti: (bi, ti, k // hps + hi))
    return pl.pallas_call(
        functools.partial(_delta_kernel, tin=tin, tt=tt, chunk=chunk, width=width, hps=hps),
        grid=(b, nh // hps, t // tin),
        in_specs=[
            col(qb), col(qb + nh), col(qb + 2 * nh), col(zb),
            pl.BlockSpec((None, tin, LANES), lambda bi, hi, ti: (bi, ti, bdb)),
            pl.BlockSpec((None, hps, 3, width - 1, LANES), lambda bi, hi, ti: (bi, hi, 0, 0, 0)),
            pl.BlockSpec((3, hps, width, LANES), lambda bi, hi, ti: (0, hi, 0, 0)),
            pl.BlockSpec((hps, 2, LANES), lambda bi, hi, ti: (hi, 0, 0)),
            pl.BlockSpec((1, DV_C), lambda bi, hi, ti: (0, 0)),
            pl.BlockSpec((None, hps, DK_C, DV_C), lambda bi, hi, ti: (bi, hi, 0, 0)),
        ],
        out_specs=[
            pl.BlockSpec((None, tin, hps * LANES), lambda bi, hi, ti: (bi, ti, hi)),
            pl.BlockSpec((None, hps, DK_C, DV_C), lambda bi, hi, ti: (bi, hi, 0, 0)),
        ],
        out_shape=[
            jax.ShapeDtypeStruct((b, t, nh * DV_C), BF16),
            jax.ShapeDtypeStruct((b, nh, DK_C, DV_C), F32),
        ],
        scratch_shapes=[pltpu.VMEM((hps, 3, tt + SUBLANES, LANES), F32)],
        compiler_params=_cparams("parallel", "parallel", "arbitrary"),
        name="gated_delta",
    )(u3, u3, u3, u3, u3, buf3, cw3, gate_par, norm_c, s0)


def _merge_kernel(x_ref, a_ref, b_ref, c_ref, g0_ref, g1_ref, g2_ref, wa_ref, wb_ref, wc_ref, wo_ref, o_ref):
    merged = (_sigmoid(g0_ref[...]) * _dot(a_ref[...], wa_ref[...])
              + _sigmoid(g1_ref[...]) * _dot(b_ref[...], wb_ref[...])
              + _sigmoid(g2_ref[...]) * _dot(c_ref[...], wc_ref[...]))
    o_ref[...] = x_ref[...] + _dot(merged.astype(BF16), wo_ref[...])


def _merge(x, act_a, act_b, act_c, u, wa, wb, wc, wo, *, tm):
    m, d = x.shape
    gb = COL_GATES // d
    row = lambda width, k=0: pl.BlockSpec((tm, width), lambda i: (i, k))
    full = lambda w: pl.BlockSpec(w.shape, lambda i: (0, 0))
    return pl.pallas_call(
        _merge_kernel,
        grid=(m // tm,),
        in_specs=[
            row(d), row(act_a.shape[1]), row(act_b.shape[1]), row(act_c.shape[1]),
            row(d, gb), row(d, gb + 1), row(d, gb + 2),
            full(wa), full(wb), full(wc), full(wo),
        ],
        out_specs=row(d),
        out_shape=jax.ShapeDtypeStruct((m, d), F32),
        compiler_params=_cparams("parallel"),
        name="merge_out_proj",
    )(x, act_a, act_b, act_c, u, u, u, wa, wb, wc, wo)


def _finish(x_ref, total, gf_ref, o_ref, final_norm):
    y = x_ref[...] + total
    if final_norm:
        y = _rms_scale(y) * gf_ref[...]
    o_ref[...] = y


def _ffn_kernel(x_ref, g_ref, gf_ref, wg_ref, wu_ref, wd_ref, o_ref, hn_ref, acc_ref, *, final_norm):
    f = pl.program_id(1)

    @pl.when(f == 0)
    def _():
        hn_ref[...] = (_rms_scale(x_ref[...]) * g_ref[...]).astype(BF16)
        acc_ref[...] = jnp.zeros(acc_ref.shape, F32)

    hn = hn_ref[...]
    a = _dot(hn, wg_ref[...])
    hidden = (a * _sigmoid(a) * _dot(hn, wu_ref[...])).astype(BF16)
    acc_ref[...] += _dot(hidden, wd_ref[...])

    @pl.when(f == pl.num_programs(1) - 1)
    def _():
        _finish(x_ref, acc_ref[...], gf_ref, o_ref, final_norm)


def _ffn(x, g, gf, wg, wu, wd, *, tm, tf, final_norm):
    m, d = x.shape
    ff = wg.shape[1]
    return pl.pallas_call(
        functools.partial(_ffn_kernel, final_norm=final_norm),
        grid=(m // tm, ff // tf),
        in_specs=[
            pl.BlockSpec((tm, d), lambda i, f: (i, 0)),
            pl.BlockSpec((1, d), lambda i, f: (0, 0)),
            pl.BlockSpec((1, d), lambda i, f: (0, 0)),
            pl.BlockSpec((d, tf), lambda i, f: (0, f)),
            pl.BlockSpec((d, tf), lambda i, f: (0, f)),
            pl.BlockSpec((tf, d), lambda i, f: (f, 0)),
        ],
        out_specs=pl.BlockSpec((tm, d), lambda i, f: (i, 0)),
        out_shape=jax.ShapeDtypeStruct((m, d), F32),
        scratch_shapes=[pltpu.VMEM((tm, d), BF16), pltpu.VMEM((tm, d), F32)],
        compiler_params=_cparams("parallel", "arbitrary"),
        name="dense_swiglu",
    )(x, g, gf, wg, wu, wd)


def _top2_gates(logits):
    lane = lax.broadcasted_iota(jnp.int32, logits.shape, 1).astype(F32)
    neg = -jnp.inf
    lg = jnp.where(lane < N_EXPERTS, logits, neg)
    m1 = jnp.max(lg, axis=-1, keepdims=True)
    i1 = jnp.min(jnp.where(lg == m1, lane, float(LANES)), axis=-1, keepdims=True)
    lg2 = jnp.where(lane == i1, neg, lg)
    m2 = jnp.max(lg2, axis=-1, keepdims=True)
    i2 = jnp.min(jnp.where(lg2 == m2, lane, float(LANES)), axis=-1, keepdims=True)
    e2 = jnp.exp(m2 - m1)
    w1 = 1.0 / (1.0 + e2)
    gate = jnp.where(lane == i1, w1, jnp.where(lane == i2, e2 * w1, 0.0))
    return gate, jnp.where(lane == i1, 1.0, jnp.where(lane == i2, 1.0, 0.0))


GATE, RANK, SEL, TABLE_ROWS = 0, N_EXPERTS, 2 * N_EXPERTS, 4 * N_EXPERTS


def _router_kernel(x_ref, g_ref, wr_ref, hn_ref, col_ref, row_ref, cnt_ref):
    tm = x_ref.shape[0]
    hn = _rms_scale(x_ref[...]) * g_ref[...]
    hn_ref[...] = hn.astype(BF16)
    gate, sel = _top2_gates(_dot_hi(hn, wr_ref[...]))
    gate_t = jnp.transpose(gate)[0:N_EXPERTS]
    sel_t = jnp.transpose(sel)[0:N_EXPERTS]
    earlier = jnp.where(lax.broadcasted_iota(jnp.int32, (tm, tm), 0) < lax.broadcasted_iota(jnp.int32, (tm, tm), 1),
                        1.0, 0.0).astype(BF16)
    rank_t = _dot(sel_t.astype(BF16), earlier)
    table = jnp.concatenate([gate_t, rank_t, sel_t, jnp.zeros((LANES - 3 * N_EXPERTS, tm), F32)], axis=0)
    row_ref[...] = table[0:TABLE_ROWS]
    col_ref[...] = jnp.transpose(table)
    cnt_ref[...] = jnp.broadcast_to(jnp.sum(sel, axis=0, keepdims=True), cnt_ref.shape).astype(jnp.int32)


def _router(x, g, w_router, *, tm):
    m, d = x.shape
    nt = m // tm
    return pl.pallas_call(
        _router_kernel,
        grid=(nt,),
        in_specs=[
            pl.BlockSpec((tm, d), lambda i: (i, 0)),
            pl.BlockSpec((1, d), lambda i: (0, 0)),
            pl.BlockSpec((d, LANES), lambda i: (0, 0)),
        ],
        out_specs=[
            pl.BlockSpec((tm, d), lambda i: (i, 0)),
            pl.BlockSpec((tm, LANES), lambda i: (i, 0)),
            pl.BlockSpec((None, TABLE_ROWS, tm), lambda i: (i, 0, 0)),
            pl.BlockSpec((None, SUBLANES, LANES), lambda i: (i, 0, 0)),
        ],
        out_shape=[
            jax.ShapeDtypeStruct((m, d), BF16),
            jax.ShapeDtypeStruct((m, LANES), F32),
            jax.ShapeDtypeStruct((nt, TABLE_ROWS, tm), F32),
            jax.ShapeDtypeStruct((nt, SUBLANES, LANES), jnp.int32),
        ],
        compiler_params=_cparams("parallel"),
        name="moe_router",
    )(x, g, w_router)


def _experts_kernel(cnt_ref, x_ref, hn_ref, col_ref, row_ref, gf_ref, wg_ref, wu_ref, wd_ref, o_ref,
                    hc_ref, yacc_ref, tot_ref, ecol_ref, *, rows, final_norm):
    i = pl.program_id(0)
    e = pl.program_id(1)
    f = pl.program_id(2)
    last_f = pl.num_programs(2) - 1
    tm = x_ref.shape[0]
    n_sub = tm // rows
    count = cnt_ref[i, e]

    @pl.when((e == 0) & (f == 0))
    def _():
        tot_ref[...] = jnp.zeros(tot_ref.shape, F32)

    @pl.when(f == last_f)
    def _():
        lane = lax.broadcasted_iota(jnp.int32, (tm, LANES), 1)
        col = col_ref[...]
        for n, k in enumerate((GATE, RANK, SEL)):
            ecol_ref[n] = jnp.broadcast_to(
                jnp.sum(jnp.where(lane == k + e, col, 0.0), axis=-1, keepdims=True), (tm, LANES))

    for sb in range(n_sub):
        @pl.when(sb * rows < count)
        def _(sb=sb):
            @pl.when(f == 0)
            def _():
                rank_row = row_ref[pl.ds(RANK + e, 1), :]
                sel_row = row_ref[pl.ds(SEL + e, 1), :]
                slot = (lax.broadcasted_iota(jnp.int32, (rows, tm), 0) + sb * rows).astype(F32)
                pick = jnp.where(rank_row == slot, sel_row, 0.0).astype(BF16)
                hc_ref[sb] = _dot(pick, hn_ref[...]).astype(BF16)
                yacc_ref[sb] = jnp.zeros((rows, yacc_ref.shape[2]), F32)

            hc = hc_ref[sb]
            a = _dot(hc, wg_ref[...])
            hidden = (a * _sigmoid(a) * _dot(hc, wu_ref[...])).astype(BF16)
            yacc_ref[sb] += _dot(hidden, wd_ref[...])

            @pl.when(f == last_f)
            def _():
                slot = (lax.broadcasted_iota(jnp.int32, (tm, rows), 1) + sb * rows).astype(F32)
                place = jnp.where(ecol_ref[1] == slot, ecol_ref[2], 0.0).astype(BF16)
                tot_ref[...] += ecol_ref[0][:, 0:1] * _dot(place, yacc_ref[sb].astype(BF16))

    @pl.when((e == pl.num_programs(1) - 1) & (f == last_f))
    def _():
        _finish(x_ref, tot_ref[...], gf_ref, o_ref, final_norm)


def _moe(x, g, gf, w_router, wg, wu, wd, *, tm, tf, final_norm):
    m, d = x.shape
    ne, _, ff = wg.shape
    rows = MOE_ROWS
    assert rows == LANES and tm % rows == 0
    hn, col, row, cnt = _router(x, g, w_router, tm=tm)
    grid_spec = pltpu.PrefetchScalarGridSpec(
        num_scalar_prefetch=1,
        grid=(m // tm, ne, ff // tf),
        in_specs=[
            pl.BlockSpec((tm, d), lambda i, e, f, c: (i, 0)),
            pl.BlockSpec((tm, d), lambda i, e, f, c: (i, 0)),
            pl.BlockSpec((tm, LANES), lambda i, e, f, c: (i, 0)),
            pl.BlockSpec((None, TABLE_ROWS, tm), lambda i, e, f, c: (i, 0, 0)),
            pl.BlockSpec((1, d), lambda i, e, f, c: (0, 0)),
            pl.BlockSpec((None, d, tf), lambda i, e, f, c: (e, 0, f)),
            pl.BlockSpec((None, d, tf), lambda i, e, f, c: (e, 0, f)),
            pl.BlockSpec((None, tf, d), lambda i, e, f, c: (e, f, 0)),
        ],
        out_specs=pl.BlockSpec((tm, d), lambda i, e, f, c: (i, 0)),
        scratch_shapes=[
            pltpu.VMEM((tm // rows, rows, d), BF16), pltpu.VMEM((tm // rows, rows, d), F32),
            pltpu.VMEM((tm, d), F32), pltpu.VMEM((3, tm, LANES), F32),
        ],
    )
    return pl.pallas_call(
        functools.partial(_experts_kernel, rows=rows, final_norm=final_norm),
        grid_spec=grid_spec,
        out_shape=jax.ShapeDtypeStruct((m, d), F32),
        compiler_params=_cparams("parallel", "arbitrary", "arbitrary"),
        name="moe_experts",
    )(cnt[:, 0, :ne], x, hn, col, row, gf, wg, wu, wd)


def _tile(n, pref):
    return pref if n % pref == 0 else n


def _trunk(x3, attend, bufs_a, bufs_c, states, p):
    b, t, d = x3.shape
    m = b * t
    depth = p["w_proj"].shape[0]
    x = x3.reshape(m, d)
    tm = _tile(m, 1024)
    ks, vs, bas, bcs, ss = [], [], [], [], []
    for l in range(depth):
        u = _norm_matmul(x, p["norm_mix"][l], p["w_proj"][l], p["b_proj"][l], tm=tm, tn=1152)
        u3 = u.reshape(b, t, N_PROJ)
        act_a, nbuf_a = _mixer_a(u3, bufs_a[l], p["conv_a_w"][l], tt=_tile(t, 512))
        act_b = attend(u3, l)
        tin = min(t, 512)
        act_c, s_new = _delta(u3, bufs_c[l], p["conv_c_w"][l], p["gate_par"][l], p["norm_c"][l], states[l],
                              tin=tin, tt=max(tin, DELTA_CHUNK), chunk=DELTA_CHUNK,
                              hps=2 if tin >= DELTA_CHUNK else N_HEADS_C)
        act_c = act_c.reshape(m, -1)
        x = _merge(x, act_a.reshape(m, -1), act_b.reshape(m, -1), act_c, u,
                   p["w_br_a"][l], p["w_br_b"][l], p["w_br_c"][l], p["w_out"][l], tm=_tile(m, 256))
        final = l == depth - 1
        if l % 2 == 0:
            x = _ffn(x, p["norm_ffn"][l], p["norm_final"], p["w_ffn_gate"][l // 2], p["w_ffn_up"][l // 2],
                     p["w_ffn_down"][l // 2], tm=tm, tf=256, final_norm=final)
        else:
            x = _moe(x, p["norm_ffn"][l], p["norm_final"], p["w_router"][l // 2], p["w_exp_gate"][l // 2],
                     p["w_exp_up"][l // 2], p["w_exp_down"][l // 2], tm=tm, tf=896, final_norm=final)
        ks.append(u3[:, :, 3 * W_A + W_B:3 * W_A + 2 * W_B].reshape(b, t, N_HEADS_B, HEAD_DIM_B))
        vs.append(u3[:, :, 3 * W_A + 2 * W_B:3 * W_A + 3 * W_B].reshape(b, t, N_HEADS_B, HEAD_DIM_B))
        bas.append(nbuf_a)
        wc = p["conv_c_w"].shape[1]
        bcs.append(u3[:, t - (wc - 1):, 3 * W_A + 3 * W_B:3 * W_A + 3 * W_B + W_QKV_C])
        ss.append(s_new)
    return (x.reshape(b, t, d), jnp.stack(ks), jnp.stack(vs), jnp.stack(bas), jnp.stack(bcs), jnp.stack(ss))


def kernel(x_prompt, x_sample, cache_k, cache_v, page_table, state_conv_a, state_conv_c, state_delta, norm_mix, w_in, b_in, conv_a_w, conv_c_w, a_log, dt_bias, norm_c, w_br_a, w_br_b, w_br_c, w_out, norm_ffn, w_ffn_gate, w_ffn_up, w_ffn_down, w_router, w_exp_gate, w_exp_up, w_exp_down, norm_final):
    depth, d, n_in = w_in.shape
    assert n_in == N_MAIN + 2 * N_HEADS_C + N_GATES
    col_gate_src = N_MAIN + 2 * N_HEADS_C
    pad = N_PROJ - n_in

    def reorder(a):
        return jnp.concatenate(
            [a[..., :N_MAIN], a[..., col_gate_src:], a[..., N_MAIN:col_gate_src],
             jnp.zeros(a.shape[:-1] + (pad,), a.dtype)], axis=-1)

    p = {
        "norm_mix": norm_mix[:, None, :],
        "w_proj": reorder(w_in).astype(BF16),
        "b_proj": reorder(b_in)[:, None, :],
        "conv_a_w": conv_a_w,
        "conv_c_w": conv_c_w,
        "gate_par": jnp.broadcast_to(jnp.stack([a_log, dt_bias], axis=-1)[..., None], (depth, N_HEADS_C, 2, LANES)),
        "norm_c": norm_c[:, None, :],
        "w_br_a": w_br_a.astype(BF16), "w_br_b": w_br_b.astype(BF16), "w_br_c": w_br_c.astype(BF16),
        "w_out": w_out.astype(BF16),
        "norm_ffn": norm_ffn[:, None, :],
        "norm_final": norm_final[None, :],
        "w_ffn_gate": w_ffn_gate.astype(BF16), "w_ffn_up": w_ffn_up.astype(BF16),
        "w_ffn_down": w_ffn_down.astype(BF16),
        "w_router": jnp.pad(w_router, ((0, 0), (0, 0), (0, LANES - w_router.shape[-1]))),
        "w_exp_gate": w_exp_gate.astype(BF16), "w_exp_up": w_exp_up.astype(BF16),
        "w_exp_down": w_exp_down.astype(BF16),
    }

    bp = x_prompt.shape[0]
    zero_a = jnp.zeros((depth, bp) + state_conv_a.shape[2:], F32)
    zero_c = jnp.zeros((depth, bp) + state_conv_c.shape[2:], F32)
    zero_s = jnp.zeros((depth, bp) + state_delta.shape[2:], F32)
    y_p, k_p, v_p, ca_p, cc_p, s_p = _trunk(
        x_prompt, lambda u3, l: _sb_prompt(u3, tq=512), zero_a, zero_c, zero_s, p)

    cache_kt = jnp.transpose(cache_k, (0, 1, 3, 4, 2))
    cache_vt = jnp.transpose(cache_v, (0, 1, 3, 4, 2))
    y_s, k_s, v_s, ca_s, cc_s, s_s = _trunk(
        x_sample, lambda u3, l: _sb_sample(u3, cache_kt, cache_vt, page_table, l, pps=16),
        state_conv_a, state_conv_c, state_delta, p)
    return (y_p, y_s, k_p, v_p, k_s, v_s, ca_p, ca_s, cc_p, cc_s, s_p, s_s)
```

```python
import functools

import jax
import jax.numpy as jnp
from jax import lax
from jax.experimental import pallas as pl
from jax.experimental.pallas import tpu as pltpu

F32 = jnp.float32
BF16 = jnp.bfloat16

RMS_EPS = 1e-6
L2_EPS = 1e-6

N_HEADS_B = 8
HEAD_DIM_B = 64
N_HEADS_C = 8
DK_C = 128
DV_C = 128
N_EXPERTS = 8
DELTA_CHUNK = 128
DELTA_CHUNK_SHORT = 64
MOE_ROWS = 128

LANES = 128
SUBLANES = 8
VMEM_LIMIT_BYTES = 56 * 1024 * 1024

W_A = 512
W_B = 512
W_QKV_C = 3072
W_Z_C = 1024
N_MAIN = 3 * W_A + 3 * W_B + W_QKV_C + W_Z_C
N_GATES = 3072
COL_GATES = N_MAIN
COL_BD = N_MAIN + N_GATES
N_PROJ = COL_BD + LANES


def _cparams(*sem):
    return pltpu.CompilerParams(dimension_semantics=sem, vmem_limit_bytes=VMEM_LIMIT_BYTES)


def _sigmoid(x):
    return 1.0 / (1.0 + jnp.exp(-x))


def _softplus(x):
    return jnp.maximum(x, 0.0) + jnp.log1p(jnp.exp(-jnp.abs(x)))


def _dot(a, b):
    return jnp.dot(a, b, preferred_element_type=F32)


def _dot_nt(a, b):
    return lax.dot_general(a, b, (((1,), (1,)), ((), ())), preferred_element_type=F32)


def _dot_tn(a, b):
    return lax.dot_general(a, b, (((0,), (0,)), ((), ())), preferred_element_type=F32)


def _dot_hi(a, b):
    return jnp.dot(a, b, preferred_element_type=F32, precision=lax.Precision.HIGHEST)


def _rms_scale(x):
    return x * lax.rsqrt(jnp.mean(x * x, axis=-1, keepdims=True) + RMS_EPS)


def _split_hi_lo(x):
    bits = lax.bitcast_convert_type(x, jnp.uint32) & jnp.uint32(0xFFFF0000)
    hi = lax.bitcast_convert_type(bits, F32)
    return hi.astype(BF16), (x - hi).astype(BF16)


def _norm_matmul_kernel(x_ref, g_ref, w_ref, b_ref, o_ref, xn_ref):
    @pl.when(pl.program_id(1) == 0)
    def _():
        xn_ref[...] = (_rms_scale(x_ref[...]) * g_ref[...]).astype(BF16)

    o_ref[...] = _dot(xn_ref[...], w_ref[...]) + b_ref[...]


def _norm_matmul(x, g, w, b, *, tm, tn):
    m, d = x.shape
    n = w.shape[1]
    return pl.pallas_call(
        _norm_matmul_kernel,
        grid=(m // tm, n // tn),
        in_specs=[
            pl.BlockSpec((tm, d), lambda i, j: (i, 0)),
            pl.BlockSpec((1, d), lambda i, j: (0, 0)),
            pl.BlockSpec((d, tn), lambda i, j: (0, j)),
            pl.BlockSpec((1, tn), lambda i, j: (0, j)),
        ],
        out_specs=pl.BlockSpec((tm, tn), lambda i, j: (i, j)),
        out_shape=jax.ShapeDtypeStruct((m, n), F32),
        scratch_shapes=[pltpu.VMEM((tm, d), BF16)],
        compiler_params=_cparams("parallel", "arbitrary"),
        name="norm_in_proj",
    )(x, g, w, b)


def _mixer_a_kernel(h_ref, gb_ref, gc_ref, buf_ref, w_ref, act_ref, nbuf_ref, p_ref, *, tt, width):
    t = pl.program_id(1)
    lo = SUBLANES - (width - 1)

    @pl.when(t == 0)
    def _():
        p_ref[lo:SUBLANES, :] = buf_ref[...]

    @pl.when(t > 0)
    def _():
        p_ref[0:SUBLANES, :] = p_ref[tt:tt + SUBLANES, :]

    p_ref[SUBLANES:SUBLANES + tt, :] = gc_ref[...] * h_ref[...]
    y = p_ref[lo:lo + tt, :] * w_ref[0:1, :]
    for i in range(1, width):
        y = y + p_ref[lo + i:lo + i + tt, :] * w_ref[i:i + 1, :]
    act_ref[...] = (gb_ref[...] * y).astype(BF16)
    nbuf_ref[...] = p_ref[SUBLANES + tt - (width - 1):SUBLANES + tt, :]


def _mixer_a(u3, buf, w, *, tt):
    b, t, _ = u3.shape
    width, c = w.shape
    assert t % tt == 0 and t >= width - 1 and c == W_A
    col = lambda k: pl.BlockSpec((None, tt, c), lambda bi, ti: (bi, ti, k))
    return pl.pallas_call(
        functools.partial(_mixer_a_kernel, tt=tt, width=width),
        grid=(b, t // tt),
        in_specs=[
            col(0), col(1), col(2),
            pl.BlockSpec((None, width - 1, c), lambda bi, ti: (bi, 0, 0)),
            pl.BlockSpec((width, c), lambda bi, ti: (0, 0)),
        ],
        out_specs=[
            pl.BlockSpec((None, tt, c), lambda bi, ti: (bi, ti, 0)),
            pl.BlockSpec((None, width - 1, c), lambda bi, ti: (bi, 0, 0)),
        ],
        out_shape=[
            jax.ShapeDtypeStruct((b, t, c), BF16),
            jax.ShapeDtypeStruct((b, width - 1, c), F32),
        ],
        scratch_shapes=[pltpu.VMEM((tt + SUBLANES, c), F32)],
        compiler_params=_cparams("parallel", "arbitrary"),
        name="mixer_a_conv",
    )(u3, u3, u3, buf, w)


def _suffix_sum_matrix(n, passes):
    s = lax.broadcasted_iota(jnp.int32, (passes * n, n), 0)
    s = jnp.where(s >= n, s - n, s)
    j = lax.broadcasted_iota(jnp.int32, (passes * n, n), 1)
    return jnp.where(s >= j, 1.0, 0.0).astype(BF16)


def _sb_softplus_sums(z, tri, mask):
    neg_abs = lax.bitcast_convert_type(lax.bitcast_convert_type(z, jnp.uint32) | jnp.uint32(0x80000000), F32)
    sp = jnp.maximum(z, 0.0) + jnp.log(1.0 + jnp.exp(neg_abs))
    if mask is not None:
        sp = jnp.where(mask, sp, 0.0)
    if tri.shape[0] == 2 * tri.shape[1]:
        addends = jnp.concatenate(_split_hi_lo(sp), axis=1)
    else:
        addends = sp.astype(BF16)
    return _dot(addends, tri), jnp.sum(sp, axis=-1, keepdims=True)


def _sb_weights(z, suffix, r_run, mask):
    a = jnp.exp(z - suffix - r_run)
    if mask is not None:
        a = jnp.where(mask, a, 0.0)
    return a.astype(BF16)


def _sb_prompt_kernel(q_ref, k_ref, v_ref, o_ref, kb_ref, vb_ref, acc_ref, r_ref, qh_ref, tri_ref,
                      za_ref, zb_ref, aa_ref, ab_ref, *, tq, scale):
    qi = pl.program_id(2)
    z_refs, a_refs = (za_ref, zb_ref), (aa_ref, ab_ref)
    kbn = LANES

    @pl.when(qi == 0)
    def _():
        kb_ref[...] = k_ref[...].astype(BF16)
        vb_ref[...] = v_ref[...].astype(BF16)

    lane = lax.broadcasted_iota(jnp.int32, (tq, LANES), 1)
    first = lane < HEAD_DIM_B
    q = q_ref[...] * scale
    qh_ref[0] = jnp.where(first, q, 0.0).astype(BF16)
    qh_ref[1] = jnp.where(first, 0.0, q).astype(BF16)
    tri_ref[...] = _suffix_sum_matrix(kbn, 1)
    row = lax.broadcasted_iota(jnp.int32, (tq, kbn), 0)
    col = lax.broadcasted_iota(jnp.int32, (tq, kbn), 1)
    acc_ref[...] = jnp.zeros(acc_ref.shape, F32)
    r_ref[...] = jnp.zeros(r_ref.shape, F32)

    def logits(kb2):
        k2 = kb_ref[pl.ds(pl.multiple_of(kb2 * 2 * kbn, 2 * kbn), 2 * kbn), :]
        pieces = []
        for h in range(2):
            z = _dot_nt(qh_ref[h], k2)
            pieces += [z[:, kbn:], z[:, :kbn]]
        return jnp.concatenate(pieces, axis=0)

    def weights(z, masks):
        mask = None if masks is None else jnp.concatenate(list(masks) * 2, axis=0)
        suffix, rs = _sb_softplus_sums(z, tri_ref[...], mask)
        r_parts = []
        for h in range(2):
            r_in = r_ref[h]
            r_mid = r_in + rs[2 * h * tq:(2 * h + 1) * tq]
            r_parts += [r_in, r_mid]
            r_ref[h] = r_mid + rs[(2 * h + 1) * tq:(2 * h + 2) * tq]
        return _sb_weights(z, suffix, jnp.concatenate(r_parts, axis=0), mask)

    def accumulate(a, kb2):
        v2 = vb_ref[pl.ds(pl.multiple_of(kb2 * 2 * kbn, 2 * kbn), 2 * kbn), :]
        for h in range(2):
            a_h = jnp.concatenate([a[(2 * h + 1) * tq:(2 * h + 2) * tq], a[2 * h * tq:(2 * h + 1) * tq]], axis=1)
            acc_ref[h] += _dot(a_h, v2)

    assert tq == 4 * kbn
    n_all = 2 * (qi + 1)
    blk = lambda s: jnp.maximum(n_all - 1 - s, 0)

    def step(s, half, masks, first_step=False):
        cur, nxt = half, 1 - half
        z_refs[nxt][...] = logits(blk(s + 1))
        if not first_step:
            accumulate(a_refs[nxt][...], blk(s - 1))
        a_refs[cur][...] = weights(z_refs[cur][...], masks)

    z_refs[0][...] = logits(blk(0))
    step(0, 0, (col + 3 * kbn < row, col + 2 * kbn < row), first_step=True)
    step(1, 1, (col + kbn < row, col < row))

    def trip(j, carry):
        for half in range(2):
            step(2 * j + 2 + half, half, None)
        return carry

    lax.fori_loop(0, qi, trip, 0)
    accumulate(a_refs[1][...], blk(n_all - 1))
    o_ref[...] = jnp.where(first, acc_ref[0], acc_ref[1]).astype(BF16)


def _sb_prompt(u3, *, tq):
    b, t, _ = u3.shape
    assert t % tq == 0 and tq % (2 * LANES) == 0
    pairs = W_B // LANES
    q_blk, k_blk, v_blk = (3 * W_A) // LANES, (3 * W_A + W_B) // LANES, (3 * W_A + 2 * W_B) // LANES
    return pl.pallas_call(
        functools.partial(_sb_prompt_kernel, tq=tq, scale=HEAD_DIM_B ** -0.5),
        grid=(b, pairs, t // tq),
        in_specs=[
            pl.BlockSpec((None, tq, LANES), lambda bi, hp, qi: (bi, qi, q_blk + hp)),
            pl.BlockSpec((None, t, LANES), lambda bi, hp, qi: (bi, 0, k_blk + hp)),
            pl.BlockSpec((None, t, LANES), lambda bi, hp, qi: (bi, 0, v_blk + hp)),
        ],
        out_specs=pl.BlockSpec((None, tq, LANES), lambda bi, hp, qi: (bi, qi, hp)),
        out_shape=jax.ShapeDtypeStruct((b, t, W_B), BF16),
        scratch_shapes=[
            pltpu.VMEM((t, LANES), BF16), pltpu.VMEM((t, LANES), BF16),
            pltpu.VMEM((2, tq, LANES), F32), pltpu.VMEM((2, tq, LANES), F32),
            pltpu.VMEM((2, tq, LANES), BF16), pltpu.VMEM((LANES, LANES), BF16),
            pltpu.VMEM((4 * tq, LANES), F32), pltpu.VMEM((4 * tq, LANES), F32),
            pltpu.VMEM((4 * tq, LANES), BF16), pltpu.VMEM((4 * tq, LANES), BF16),
        ],
        compiler_params=_cparams("parallel", "parallel", "arbitrary"),
        name="stickbreak_prompt",
    )(u3, u3, u3)


def _sb_sample_kernel(pt_ref, q_ref, ko_ref, vo_ref, *refs, n_new, page, pps, scale):
    del pt_ref
    k_refs, v_refs = refs[:pps], refs[pps:2 * pps]
    o_ref, qbd_ref, acc_ref, r_ref, own_ref = refs[2 * pps:]
    j = pl.program_id(1)
    nh, hd = N_HEADS_B, HEAD_DIM_B
    rows = nh * n_new
    tri2 = _suffix_sum_matrix(page, 2)

    @pl.when(j == 0)
    def _():
        qt = jnp.concatenate([q_ref[...] * scale] * nh, axis=0)
        rh = lax.broadcasted_iota(jnp.int32, (rows, W_B), 0) // n_new
        ch = lax.broadcasted_iota(jnp.int32, (rows, W_B), 1) // hd
        qbd_ref[...] = jnp.where(rh == ch, qt, 0.0).astype(BF16)
        r_ref[...] = jnp.zeros(r_ref.shape, F32)
        own_ref[...] = jnp.zeros(own_ref.shape, BF16)
        own_ref[0, 0:n_new, :] = ko_ref[...].astype(BF16)
        own_ref[1, 0:n_new, :] = vo_ref[...].astype(BF16)
        qpos = lax.broadcasted_iota(jnp.int32, (rows, page), 0) % n_new
        kpos = lax.broadcasted_iota(jnp.int32, (rows, page), 1)
        z = _dot_nt(qbd_ref[...], own_ref[0])
        suffix, rs = _sb_softplus_sums(z, tri2, kpos < qpos)
        r_ref[...] = jnp.broadcast_to(rs, r_ref.shape)
        acc_ref[...] = _dot(_sb_weights(z, suffix, 0.0, kpos < qpos), own_ref[1])

    def lanes(refs_):
        return jnp.concatenate([r[...].reshape(nh * hd, page).astype(BF16) for r in refs_], axis=1)

    z = _dot(qbd_ref[...], lanes(k_refs))
    z = jnp.concatenate([z[:, i * page:(i + 1) * page] for i in range(pps)], axis=0)
    suffix, rs = _sb_softplus_sums(z, tri2, None)
    r = r_ref[...]
    r_parts = []
    for i in range(pps):
        r_parts.append(r)
        r = r + rs[i * rows:(i + 1) * rows]
    r_ref[...] = r
    a = _sb_weights(z, suffix, jnp.concatenate(r_parts, axis=0), None)
    a = jnp.concatenate([a[i * rows:(i + 1) * rows] for i in range(pps)], axis=1)
    acc_ref[...] += _dot_nt(a, lanes(v_refs))

    @pl.when(j == pl.num_programs(1) - 1)
    def _():
        acc = acc_ref[...]
        ch = lax.broadcasted_iota(jnp.int32, (n_new, W_B), 1) // hd
        out = jnp.zeros((n_new, W_B), F32)
        for h in range(nh):
            out = out + jnp.where(ch == h, acc[h * n_new:(h + 1) * n_new, :], 0.0)
        o_ref[...] = out.astype(BF16)


def _sb_sample(u3, cache_kt, cache_vt, page_table, layer, *, pps):
    db, n_new, _ = u3.shape
    _, _, nh, hd, page = cache_kt.shape
    n_pages = page_table.shape[1]
    assert nh == N_HEADS_B and hd == HEAD_DIM_B and n_new % SUBLANES == 0 and n_pages % pps == 0
    blk = (3 * W_A) // W_B

    def page_spec(i):
        return pl.BlockSpec((None, None, nh, hd, page),
                            lambda bi, j, pt: (layer, pt[bi, n_pages - 1 - (j * pps + i)], 0, 0, 0))

    new_spec = lambda k: pl.BlockSpec((None, n_new, W_B), lambda bi, j, pt: (bi, 0, blk + k))
    grid_spec = pltpu.PrefetchScalarGridSpec(
        num_scalar_prefetch=1,
        grid=(db, n_pages // pps),
        in_specs=[new_spec(0), new_spec(1), new_spec(2)] + [page_spec(i) for i in range(pps)] * 2,
        out_specs=pl.BlockSpec((None, n_new, W_B), lambda bi, j, pt: (bi, 0, 0)),
        scratch_shapes=[
            pltpu.VMEM((nh * n_new, W_B), BF16),
            pltpu.VMEM((nh * n_new, W_B), F32),
            pltpu.VMEM((nh * n_new, LANES), F32),
            pltpu.VMEM((2, page, W_B), BF16),
        ],
    )
    return pl.pallas_call(
        functools.partial(_sb_sample_kernel, n_new=n_new, page=page, pps=pps, scale=hd ** -0.5),
        grid_spec=grid_spec,
        out_shape=jax.ShapeDtypeStruct((db, n_new, W_B), BF16),
        compiler_params=_cparams("parallel", "arbitrary"),
        name="stickbreak_sample",
    )(page_table, u3, u3, u3, *([cache_kt] * pps), *([cache_vt] * pps))


def _dot3(a_split, b_split):
    a_hi, a_lo = a_split
    b_hi, b_lo = b_split
    return _dot(jnp.concatenate([a_hi, a_hi, a_lo], axis=1), jnp.concatenate([b_hi, b_lo, b_hi], axis=0))


def _delta_kernel(q_ref, k_ref, v_ref, z_ref, bd_ref, buf_ref, cw_ref, gp_ref, nc_ref, s0_ref,
                  y_ref, s_ref, p_ref, *, tin, tt, chunk, width, hps):
    hg = pl.program_id(1)
    t = pl.program_id(2)
    lo = SUBLANES - (width - 1)
    dk = DK_C

    @pl.when(t == 0)
    def _():
        if tin < tt:
            p_ref[...] = jnp.zeros(p_ref.shape, F32)
        for hh in range(hps):
            for i in range(3):
                p_ref[hh, i, lo:SUBLANES, :] = buf_ref[hh, i]
        s_ref[...] = s0_ref[...]

    @pl.when(t > 0)
    def _():
        for hh in range(hps):
            for i in range(3):
                p_ref[hh, i, 0:SUBLANES, :] = p_ref[hh, i, tt:tt + SUBLANES, :]

    lane = lax.broadcasted_iota(jnp.int32, (tin, LANES), 1)
    bd = bd_ref[...]

    def head_inputs(hh):
        cols = slice(hh * LANES, (hh + 1) * LANES)
        conv = []
        for i, ref in enumerate((q_ref, k_ref, v_ref)):
            p_ref[hh, i, SUBLANES:SUBLANES + tin, :] = ref[:, cols]
            y = p_ref[hh, i, lo:lo + tt, :] * cw_ref[i, hh, 0:1, :]
            for w in range(1, width):
                y = y + p_ref[hh, i, lo + w:lo + w + tt, :] * cw_ref[i, hh, w:w + 1, :]
            conv.append(y * _sigmoid(y))
        qc, kc, vc = conv
        qn = qc * lax.rsqrt(jnp.sum(qc * qc, axis=-1, keepdims=True) + L2_EPS) * (dk ** -0.5)
        kn = kc * lax.rsqrt(jnp.sum(kc * kc, axis=-1, keepdims=True) + L2_EPS)
        h = hg * hps + hh
        neg_a = -jnp.exp(gp_ref[hh, 0:1, 0:1])
        dt_b = gp_ref[hh, 1:2, 0:1]
        b_col = jnp.sum(jnp.where(lane == h, bd, 0.0), axis=-1, keepdims=True)
        a_col = jnp.sum(jnp.where(lane == h + N_HEADS_C, bd, 0.0), axis=-1, keepdims=True)
        beta_col = _sigmoid(b_col)
        g_col = neg_a * _softplus(a_col + dt_b)
        if tin < tt:
            pad = jnp.zeros((tt - tin, 1), F32)
            beta_col = jnp.concatenate([beta_col, pad], axis=0)
            g_col = jnp.concatenate([g_col, pad], axis=0)
        return qn, kn, vc, g_col, beta_col

    ri = lax.broadcasted_iota(jnp.int32, (chunk, chunk), 0)
    ci = lax.broadcasted_iota(jnp.int32, (chunk, chunk), 1)
    incl = ri >= ci
    strict = ri > ci
    eye = jnp.where(ri == ci, 1.0, 0.0)
    n_doubling = (min(tin, chunk) - 1).bit_length() - 1

    def prepare(inputs, c):
        sl = slice(c * chunk, (c + 1) * chunk)
        qk, kk_, vk, gc_col, bc = (a[sl] for a in inputs)
        g_lanes = jnp.transpose(jnp.broadcast_to(gc_col, (chunk, chunk)))
        gcum_col = jnp.sum(jnp.where(incl, g_lanes, 0.0), axis=1, keepdims=True)
        gcum_row = jnp.sum(jnp.where(ri <= ci, gc_col, 0.0), axis=0, keepdims=True)
        dec_incl = jnp.where(incl, jnp.exp(jnp.where(incl, gcum_col - gcum_row, 0.0)), 0.0)
        k_bf = kk_.astype(BF16)
        e_col = jnp.exp(gcum_col)
        g_last = gcum_col[chunk - 1:chunk, :]
        return dict(
            m=bc * _dot_nt(k_bf, k_bf) * jnp.where(strict, dec_incl, 0.0),
            rhs=_split_hi_lo(jnp.concatenate([kk_ * (bc * e_col), vk * bc], axis=-1)),
            aqk=(_dot_nt(qk.astype(BF16), k_bf) * dec_incl).astype(BF16),
            q_dec=qk * e_col,
            k_dec=(kk_ * jnp.exp(g_last - gcum_col)).astype(BF16),
            g_end=jnp.exp(g_last))

    n_chunks = tt // chunk
    pre = []
    for hh in range(hps):
        inputs = head_inputs(hh)
        pre += [prepare(inputs, c) for c in range(n_chunks)]
    pw_s = [_split_hi_lo(-p["m"]) for p in pre]
    inv = [eye - p["m"] for p in pre]
    for _ in range(n_doubling):
        pw_s = [_split_hi_lo(_dot3(s_, s_)) for s_ in pw_s]
        inv = [iv + _dot3(_split_hi_lo(iv), s_) for iv, s_ in zip(inv, pw_s)]
    sols = [_dot3(_split_hi_lo(iv), p["rhs"]) for iv, p in zip(inv, pre)]

    steps = []
    for p, sol in zip(pre, sols):
        w_bf, uv_bf = sol[:, :dk].astype(BF16), sol[:, dk:].astype(BF16)
        steps.append(dict(
            s_mix=_dot_tn(p["k_dec"], w_bf).astype(BF16), s_add=_dot_tn(p["k_dec"], uv_bf),
            o_mix=(p["q_dec"] - _dot(p["aqk"], w_bf)).astype(BF16), o_add=_dot(p["aqk"], uv_bf),
            g_end=p["g_end"]))

    outs = [[] for _ in range(hps)]
    for c in range(n_chunks):
        for hh in range(hps):
            st = steps[hh * n_chunks + c]
            s = s_ref[hh]
            s_bf = s.astype(BF16)
            outs[hh].append(_dot(st["o_mix"], s_bf) + st["o_add"])
            s_ref[hh] = st["g_end"] * s + st["s_add"] - _dot(st["s_mix"], s_bf)
    for hh in range(hps):
        cols = slice(hh * LANES, (hh + 1) * LANES)
        o = (jnp.concatenate(outs[hh], axis=0) if n_chunks > 1 else outs[hh][0])[:tin]
        zg = z_ref[:, cols]
        y_ref[:, cols] = (_rms_scale(o) * nc_ref[...] * (zg * _sigmoid(zg))).astype(BF16)


def _delta(u3, buf, conv_w, gate_par, norm_c, s0, *, tin, tt, chunk, hps):
    b, t, _ = u3.shape
    width = conv_w.shape[0]
    nh = N_HEADS_C
    assert t % tin == 0 and tt % chunk == 0 and (tin == tt or t == tin) and tin >= width - 1 and nh % hps == 0
    qb = (3 * W_A + 3 * W_B) // LANES
    zb = (3 * W_A + 3 * W_B + W_QKV_C) // LANES
    bdb = COL_BD // LANES
    assert qb % hps == 0 and zb % hps == 0
    cw3 = conv_w.reshape(width, 3, nh, LANES).transpose(1, 2, 0, 3)
    buf3 = buf.reshape(b, width - 1, 3, nh, LANES).transpose(0, 3, 2, 1, 4)
    col = lambda k: pl.BlockSpec((None, tin, hps * LANES), lambda bi, hi, ti: (bi, ti, k // hps + hi))
    return pl.pallas_call(
        functools.partial(_delta_kernel, tin=tin, tt=tt, chunk=chunk, width=width, hps=hps),
        grid=(b, nh // hps, t // tin),
        in_specs=[
            col(qb), col(qb + nh), col(qb + 2 * nh), col(zb),
            pl.BlockSpec((None, tin, LANES), lambda bi, hi, ti: (bi, ti, bdb)),
            pl.BlockSpec((None, hps, 3, width - 1, LANES), lambda bi, hi, ti: (bi, hi, 0, 0, 0)),
            pl.BlockSpec((3, hps, width, LANES), lambda bi, hi, ti: (0, hi, 0, 0)),
            pl.BlockSpec((hps, 2, LANES), lambda bi, hi, ti: (hi, 0, 0)),
            pl.BlockSpec((1, DV_C), lambda bi, hi, ti: (0, 0)),
            pl.BlockSpec((None, hps, DK_C, DV_C), lambda bi, hi, ti: (bi, hi, 0, 0)),
        ],
        out_specs=[
            pl.BlockSpec((None, tin, hps * LANES), lambda bi, hi, ti: (bi, ti, hi)),
            pl.BlockSpec((None, hps, DK_C, DV_C), lambda bi, hi, ti: (bi, hi, 0, 0)),
        ],
        out_shape=[
            jax.ShapeDtypeStruct((b, t, nh * DV_C), BF16),
            jax.ShapeDtypeStruct((b, nh, DK_C, DV_C), F32),
        ],
        scratch_shapes=[pltpu.VMEM((hps, 3, tt + SUBLANES, LANES), F32)],
        compiler_params=_cparams("parallel", "parallel", "arbitrary"),
        name="gated_delta",
    )(u3, u3, u3, u3, u3, buf3, cw3, gate_par, norm_c, s0)


def _merge_kernel(x_ref, a_ref, b_ref, c_ref, g0_ref, g1_ref, g2_ref, wa_ref, wb_ref, wc_ref, wo_ref, o_ref):
    merged = (_sigmoid(g0_ref[...]) * _dot(a_ref[...], wa_ref[...])
              + _sigmoid(g1_ref[...]) * _dot(b_ref[...], wb_ref[...])
              + _sigmoid(g2_ref[...]) * _dot(c_ref[...], wc_ref[...]))
    o_ref[...] = x_ref[...] + _dot(merged.astype(BF16), wo_ref[...])


def _merge(x, act_a, act_b, act_c, u, wa, wb, wc, wo, *, tm):
    m, d = x.shape
    gb = COL_GATES // d
    row = lambda width, k=0: pl.BlockSpec((tm, width), lambda i: (i, k))
    full = lambda w: pl.BlockSpec(w.shape, lambda i: (0, 0))
    return pl.pallas_call(
        _merge_kernel,
        grid=(m // tm,),
        in_specs=[
            row(d), row(act_a.shape[1]), row(act_b.shape[1]), row(act_c.shape[1]),
            row(d, gb), row(d, gb + 1), row(d, gb + 2),
            full(wa), full(wb), full(wc), full(wo),
        ],
        out_specs=row(d),
        out_shape=jax.ShapeDtypeStruct((m, d), F32),
        compiler_params=_cparams("parallel"),
        name="merge_out_proj",
    )(x, act_a, act_b, act_c, u, u, u, wa, wb, wc, wo)


def _finish(x_ref, total, gf_ref, o_ref, final_norm):
    y = x_ref[...] + total
    if final_norm:
        y = _rms_scale(y) * gf_ref[...]
    o_ref[...] = y


def _ffn_kernel(x_ref, g_ref, gf_ref, wg_ref, wu_ref, wd_ref, o_ref, hn_ref, acc_ref, *, final_norm):
    f = pl.program_id(1)

    @pl.when(f == 0)
    def _():
        hn_ref[...] = (_rms_scale(x_ref[...]) * g_ref[...]).astype(BF16)
        acc_ref[...] = jnp.zeros(acc_ref.shape, F32)

    hn = hn_ref[...]
    a = _dot(hn, wg_ref[...])
    hidden = (a * _sigmoid(a) * _dot(hn, wu_ref[...])).astype(BF16)
    acc_ref[...] += _dot(hidden, wd_ref[...])

    @pl.when(f == pl.num_programs(1) - 1)
    def _():
        _finish(x_ref, acc_ref[...], gf_ref, o_ref, final_norm)


def _ffn(x, g, gf, wg, wu, wd, *, tm, tf, final_norm):
    m, d = x.shape
    ff = wg.shape[1]
    return pl.pallas_call(
        functools.partial(_ffn_kernel, final_norm=final_norm),
        grid=(m // tm, ff // tf),
        in_specs=[
            pl.BlockSpec((tm, d), lambda i, f: (i, 0)),
            pl.BlockSpec((1, d), lambda i, f: (0, 0)),
            pl.BlockSpec((1, d), lambda i, f: (0, 0)),
            pl.BlockSpec((d, tf), lambda i, f: (0, f)),
            pl.BlockSpec((d, tf), lambda i, f: (0, f)),
            pl.BlockSpec((tf, d), lambda i, f: (f, 0)),
        ],
        out_specs=pl.BlockSpec((tm, d), lambda i, f: (i, 0)),
        out_shape=jax.ShapeDtypeStruct((m, d), F32),
        scratch_shapes=[pltpu.VMEM((tm, d), BF16), pltpu.VMEM((tm, d), F32)],
        compiler_params=_cparams("parallel", "arbitrary"),
        name="dense_swiglu",
    )(x, g, gf, wg, wu, wd)


def _top2_gates(logits):
    lane = lax.broadcasted_iota(jnp.int32, logits.shape, 1).astype(F32)
    neg = -jnp.inf
    lg = jnp.where(lane < N_EXPERTS, logits, neg)
    m1 = jnp.max(lg, axis=-1, keepdims=True)
    i1 = jnp.min(jnp.where(lg == m1, lane, float(LANES)), axis=-1, keepdims=True)
    lg2 = jnp.where(lane == i1, neg, lg)
    m2 = jnp.max(lg2, axis=-1, keepdims=True)
    i2 = jnp.min(jnp.where(lg2 == m2, lane, float(LANES)), axis=-1, keepdims=True)
    e2 = jnp.exp(m2 - m1)
    w1 = 1.0 / (1.0 + e2)
    gate = jnp.where(lane == i1, w1, jnp.where(lane == i2, e2 * w1, 0.0))
    return gate, jnp.where(lane == i1, 1.0, jnp.where(lane == i2, 1.0, 0.0))


GATE, RANK, SEL, TABLE_ROWS = 0, N_EXPERTS, 2 * N_EXPERTS, 4 * N_EXPERTS


def _router_kernel(x_ref, g_ref, wr_ref, hn_ref, col_ref, row_ref, cnt_ref):
    tm = x_ref.shape[0]
    hn = _rms_scale(x_ref[...]) * g_ref[...]
    hn_ref[...] = hn.astype(BF16)
    gate, sel = _top2_gates(_dot_hi(hn, wr_ref[...]))
    gate_t = jnp.transpose(gate)[0:N_EXPERTS]
    sel_t = jnp.transpose(sel)[0:N_EXPERTS]
    earlier = jnp.where(lax.broadcasted_iota(jnp.int32, (tm, tm), 0) < lax.broadcasted_iota(jnp.int32, (tm, tm), 1),
                        1.0, 0.0).astype(BF16)
    rank_t = _dot(sel_t.astype(BF16), earlier)
    table = jnp.concatenate([gate_t, rank_t, sel_t, jnp.zeros((LANES - 3 * N_EXPERTS, tm), F32)], axis=0)
    row_ref[...] = table[0:TABLE_ROWS]
    col_ref[...] = jnp.transpose(table)
    cnt_ref[...] = jnp.broadcast_to(jnp.sum(sel, axis=0, keepdims=True), cnt_ref.shape).astype(jnp.int32)


def _router(x, g, w_router, *, tm):
    m, d = x.shape
    nt = m // tm
    return pl.pallas_call(
        _router_kernel,
        grid=(nt,),
        in_specs=[
            pl.BlockSpec((tm, d), lambda i: (i, 0)),
            pl.BlockSpec((1, d), lambda i: (0, 0)),
            pl.BlockSpec((d, LANES), lambda i: (0, 0)),
        ],
        out_specs=[
            pl.BlockSpec((tm, d), lambda i: (i, 0)),
            pl.BlockSpec((tm, LANES), lambda i: (i, 0)),
            pl.BlockSpec((None, TABLE_ROWS, tm), lambda i: (i, 0, 0)),
            pl.BlockSpec((None, SUBLANES, LANES), lambda i: (i, 0, 0)),
        ],
        out_shape=[
            jax.ShapeDtypeStruct((m, d), BF16),
            jax.ShapeDtypeStruct((m, LANES), F32),
            jax.ShapeDtypeStruct((nt, TABLE_ROWS, tm), F32),
            jax.ShapeDtypeStruct((nt, SUBLANES, LANES), jnp.int32),
        ],
        compiler_params=_cparams("parallel"),
        name="moe_router",
    )(x, g, w_router)


def _experts_kernel(cnt_ref, x_ref, hn_ref, col_ref, row_ref, gf_ref, wg_ref, wu_ref, wd_ref, o_ref,
                    hc_ref, yacc_ref, tot_ref, ecol_ref, *, rows, final_norm):
    i = pl.program_id(0)
    e = pl.program_id(1)
    f = pl.program_id(2)
    last_f = pl.num_programs(2) - 1
    tm = x_ref.shape[0]
    n_sub = tm // rows
    count = cnt_ref[i, e]

    @pl.when((e == 0) & (f == 0))
    def _():
        tot_ref[...] = jnp.zeros(tot_ref.shape, F32)

    @pl.when(f == last_f)
    def _():
        lane = lax.broadcasted_iota(jnp.int32, (tm, LANES), 1)
        col = col_ref[...]
        for n, k in enumerate((GATE, RANK, SEL)):
            ecol_ref[n] = jnp.broadcast_to(
                jnp.sum(jnp.where(lane == k + e, col, 0.0), axis=-1, keepdims=True), (tm, LANES))

    for sb in range(n_sub):
        @pl.when(sb * rows < count)
        def _(sb=sb):
            @pl.when(f == 0)
            def _():
                rank_row = row_ref[pl.ds(RANK + e, 1), :]
                sel_row = row_ref[pl.ds(SEL + e, 1), :]
                slot = (lax.broadcasted_iota(jnp.int32, (rows, tm), 0) + sb * rows).astype(F32)
                pick = jnp.where(rank_row == slot, sel_row, 0.0).astype(BF16)
                hc_ref[sb] = _dot(pick, hn_ref[...]).astype(BF16)
                yacc_ref[sb] = jnp.zeros((rows, yacc_ref.shape[2]), F32)

            hc = hc_ref[sb]
            a = _dot(hc, wg_ref[...])
            hidden = (a * _sigmoid(a) * _dot(hc, wu_ref[...])).astype(BF16)
            yacc_ref[sb] += _dot(hidden, wd_ref[...])

            @pl.when(f == last_f)
            def _():
                slot = (lax.broadcasted_iota(jnp.int32, (tm, rows), 1) + sb * rows).astype(F32)
                place = jnp.where(ecol_ref[1] == slot, ecol_ref[2], 0.0).astype(BF16)
                tot_ref[...] += ecol_ref[0][:, 0:1] * _dot(place, yacc_ref[sb].astype(BF16))

    @pl.when((e == pl.num_programs(1) - 1) & (f == last_f))
    def _():
        _finish(x_ref, tot_ref[...], gf_ref, o_ref, final_norm)


def _moe(x, g, gf, w_router, wg, wu, wd, *, tm, tf, final_norm):
    m, d = x.shape
    ne, _, ff = wg.shape
    rows = MOE_ROWS
    assert rows == LANES and tm % rows == 0
    hn, col, row, cnt = _router(x, g, w_router, tm=tm)
    grid_spec = pltpu.PrefetchScalarGridSpec(
        num_scalar_prefetch=1,
        grid=(m // tm, ne, ff // tf),
        in_specs=[
            pl.BlockSpec((tm, d), lambda i, e, f, c: (i, 0)),
            pl.BlockSpec((tm, d), lambda i, e, f, c: (i, 0)),
            pl.BlockSpec((tm, LANES), lambda i, e, f, c: (i, 0)),
            pl.BlockSpec((None, TABLE_ROWS, tm), lambda i, e, f, c: (i, 0, 0)),
            pl.BlockSpec((1, d), lambda i, e, f, c: (0, 0)),
            pl.BlockSpec((None, d, tf), lambda i, e, f, c: (e, 0, f)),
            pl.BlockSpec((None, d, tf), lambda i, e, f, c: (e, 0, f)),
            pl.BlockSpec((None, tf, d), lambda i, e, f, c: (e, f, 0)),
        ],
        out_specs=pl.BlockSpec((tm, d), lambda i, e, f, c: (i, 0)),
        scratch_shapes=[
            pltpu.VMEM((tm // rows, rows, d), BF16), pltpu.VMEM((tm // rows, rows, d), F32),
            pltpu.VMEM((tm, d), F32), pltpu.VMEM((3, tm, LANES), F32),
        ],
    )
    return pl.pallas_call(
        functools.partial(_experts_kernel, rows=rows, final_norm=final_norm),
        grid_spec=grid_spec,
        out_shape=jax.ShapeDtypeStruct((m, d), F32),
        compiler_params=_cparams("parallel", "arbitrary", "arbitrary"),
        name="moe_experts",
    )(cnt[:, 0, :ne], x, hn, col, row, gf, wg, wu, wd)


def _tile(n, pref):
    return pref if n % pref == 0 else n


def _trunk(x3, attend, bufs_a, bufs_c, states, p):
    b, t, d = x3.shape
    m = b * t
    depth = p["w_proj"].shape[0]
    x = x3.reshape(m, d)
    tm = _tile(m, 1024)
    ks, vs, bas, bcs, ss = [], [], [], [], []
    for l in range(depth):
        u = _norm_matmul(x, p["norm_mix"][l], p["w_proj"][l], p["b_proj"][l], tm=tm, tn=1152)
        u3 = u.reshape(b, t, N_PROJ)
        act_a, nbuf_a = _mixer_a(u3, bufs_a[l], p["conv_a_w"][l], tt=_tile(t, 512))
        act_b = attend(u3, l)
        tin = min(t, 512)
        chunk = DELTA_CHUNK if tin >= DELTA_CHUNK else DELTA_CHUNK_SHORT
        act_c, s_new = _delta(u3, bufs_c[l], p["conv_c_w"][l], p["gate_par"][l], p["norm_c"][l], states[l],
                              tin=tin, tt=max(tin, chunk), chunk=chunk,
                              hps=2 if tin >= chunk else N_HEADS_C)
        act_c = act_c.reshape(m, -1)
        x = _merge(x, act_a.reshape(m, -1), act_b.reshape(m, -1), act_c, u,
                   p["w_br_a"][l], p["w_br_b"][l], p["w_br_c"][l], p["w_out"][l], tm=_tile(m, 256))
        final = l == depth - 1
        if l % 2 == 0:
            x = _ffn(x, p["norm_ffn"][l], p["norm_final"], p["w_ffn_gate"][l // 2], p["w_ffn_up"][l // 2],
                     p["w_ffn_down"][l // 2], tm=tm, tf=1408, final_norm=final)
        else:
            x = _moe(x, p["norm_ffn"][l], p["norm_final"], p["w_router"][l // 2], p["w_exp_gate"][l // 2],
                     p["w_exp_up"][l // 2], p["w_exp_down"][l // 2], tm=tm, tf=896, final_norm=final)
        ks.append(u3[:, :, 3 * W_A + W_B:3 * W_A + 2 * W_B].reshape(b, t, N_HEADS_B, HEAD_DIM_B))
        vs.append(u3[:, :, 3 * W_A + 2 * W_B:3 * W_A + 3 * W_B].reshape(b, t, N_HEADS_B, HEAD_DIM_B))
        bas.append(nbuf_a)
        wc = p["conv_c_w"].shape[1]
        bcs.append(u3[:, t - (wc - 1):, 3 * W_A + 3 * W_B:3 * W_A + 3 * W_B + W_QKV_C])
        ss.append(s_new)
    return (x.reshape(b, t, d), jnp.stack(ks), jnp.stack(vs), jnp.stack(bas), jnp.stack(bcs), jnp.stack(ss))


def kernel(x_prompt, x_sample, cache_k, cache_v, page_table, state_conv_a, state_conv_c, state_delta, norm_mix, w_in, b_in, conv_a_w, conv_c_w, a_log, dt_bias, norm_c, w_br_a, w_br_b, w_br_c, w_out, norm_ffn, w_ffn_gate, w_ffn_up, w_ffn_down, w_router, w_exp_gate, w_exp_up, w_exp_down, norm_final):
    depth, d, n_in = w_in.shape
    assert n_in == N_MAIN + 2 * N_HEADS_C + N_GATES
    col_gate_src = N_MAIN + 2 * N_HEADS_C
    pad = N_PROJ - n_in

    def reorder(a):
        return jnp.concatenate(
            [a[..., :N_MAIN], a[..., col_gate_src:], a[..., N_MAIN:col_gate_src],
             jnp.zeros(a.shape[:-1] + (pad,), a.dtype)], axis=-1)

    p = {
        "norm_mix": norm_mix[:, None, :],
        "w_proj": reorder(w_in).astype(BF16),
        "b_proj": reorder(b_in)[:, None, :],
        "conv_a_w": conv_a_w,
        "conv_c_w": conv_c_w,
        "gate_par": jnp.broadcast_to(jnp.stack([a_log, dt_bias], axis=-1)[..., None], (depth, N_HEADS_C, 2, LANES)),
        "norm_c": norm_c[:, None, :],
        "w_br_a": w_br_a.astype(BF16), "w_br_b": w_br_b.astype(BF16), "w_br_c": w_br_c.astype(BF16),
        "w_out": w_out.astype(BF16),
        "norm_ffn": norm_ffn[:, None, :],
        "norm_final": norm_final[None, :],
        "w_ffn_gate": w_ffn_gate.astype(BF16), "w_ffn_up": w_ffn_up.astype(BF16),
        "w_ffn_down": w_ffn_down.astype(BF16),
        "w_router": jnp.pad(w_router, ((0, 0), (0, 0), (0, LANES - w_router.shape[-1]))),
        "w_exp_gate": w_exp_gate.astype(BF16), "w_exp_up": w_exp_up.astype(BF16),
        "w_exp_down": w_exp_down.astype(BF16),
    }

    bp = x_prompt.shape[0]
    zero_a = jnp.zeros((depth, bp) + state_conv_a.shape[2:], F32)
    zero_c = jnp.zeros((depth, bp) + state_conv_c.shape[2:], F32)
    zero_s = jnp.zeros((depth, bp) + state_delta.shape[2:], F32)
    y_p, k_p, v_p, ca_p, cc_p, s_p = _trunk(
        x_prompt, lambda u3, l: _sb_prompt(u3, tq=512), zero_a, zero_c, zero_s, p)

    cache_kt = jnp.transpose(cache_k, (0, 1, 3, 4, 2))
    cache_vt = jnp.transpose(cache_v, (0, 1, 3, 4, 2))
    y_s, k_s, v_s, ca_s, cc_s, s_s = _trunk(
        x_sample, lambda u3, l: _sb_sample(u3, cache_kt, cache_vt, page_table, l, pps=16),
        state_conv_a, state_conv_c, state_delta, p)
    return (y_p, y_s, k_p, v_p, k_s, v_s, ca_p, ca_s, cc_p, cc_s, s_p, s_s)
```

```python
import functools

import jax
import jax.numpy as jnp
from jax import lax
from jax.experimental import pallas as pl
from jax.experimental.pallas import tpu as pltpu

F32 = jnp.float32
BF16 = jnp.bfloat16

RMS_EPS = 1e-6
L2_EPS = 1e-6

N_HEADS_B = 8
HEAD_DIM_B = 64
N_HEADS_C = 8
DK_C = 128
DV_C = 128
N_EXPERTS = 8
DELTA_CHUNK = 128
DELTA_CHUNK_SHORT = 64
MOE_ROWS = 128

LANES = 128
SUBLANES = 8
VMEM_LIMIT_BYTES = 56 * 1024 * 1024

W_A = 512
W_B = 512
W_QKV_C = 3072
W_Z_C = 1024
N_MAIN = 3 * W_A + 3 * W_B + W_QKV_C + W_Z_C
N_GATES = 3072
COL_GATES = N_MAIN
COL_BD = N_MAIN + N_GATES
N_PROJ = COL_BD + LANES


def _cparams(*sem):
    return pltpu.CompilerParams(dimension_semantics=sem, vmem_limit_bytes=VMEM_LIMIT_BYTES)


def _sigmoid(x):
    return 1.0 / (1.0 + jnp.exp(-x))


def _softplus(x):
    return jnp.maximum(x, 0.0) + jnp.log1p(jnp.exp(-jnp.abs(x)))


def _dot(a, b):
    return jnp.dot(a, b, preferred_element_type=F32)


def _dot_nt(a, b):
    return lax.dot_general(a, b, (((1,), (1,)), ((), ())), preferred_element_type=F32)


def _dot_tn(a, b):
    return lax.dot_general(a, b, (((0,), (0,)), ((), ())), preferred_element_type=F32)


def _dot_hi(a, b):
    return jnp.dot(a, b, preferred_element_type=F32, precision=lax.Precision.HIGHEST)


def _rms_scale(x):
    return x * lax.rsqrt(jnp.mean(x * x, axis=-1, keepdims=True) + RMS_EPS)


def _split_hi_lo(x):
    bits = lax.bitcast_convert_type(x, jnp.uint32) & jnp.uint32(0xFFFF0000)
    hi = lax.bitcast_convert_type(bits, F32)
    return hi.astype(BF16), (x - hi).astype(BF16)


def _norm_matmul_kernel(x_ref, g_ref, w_ref, b_ref, o_ref, xn_ref):
    @pl.when(pl.program_id(1) == 0)
    def _():
        xn_ref[...] = (_rms_scale(x_ref[...]) * g_ref[...]).astype(BF16)

    o_ref[...] = _dot(xn_ref[...], w_ref[...]) + b_ref[...]


def _norm_matmul(x, g, w, b, *, tm, tn):
    m, d = x.shape
    n = w.shape[1]
    return pl.pallas_call(
        _norm_matmul_kernel,
        grid=(m // tm, n // tn),
        in_specs=[
            pl.BlockSpec((tm, d), lambda i, j: (i, 0)),
            pl.BlockSpec((1, d), lambda i, j: (0, 0)),
            pl.BlockSpec((d, tn), lambda i, j: (0, j)),
            pl.BlockSpec((1, tn), lambda i, j: (0, j)),
        ],
        out_specs=pl.BlockSpec((tm, tn), lambda i, j: (i, j)),
        out_shape=jax.ShapeDtypeStruct((m, n), F32),
        scratch_shapes=[pltpu.VMEM((tm, d), BF16)],
        compiler_params=_cparams("parallel", "arbitrary"),
        name="norm_in_proj",
    )(x, g, w, b)


def _mixer_a_kernel(h_ref, gb_ref, gc_ref, buf_ref, w_ref, act_ref, nbuf_ref, p_ref, *, tt, width):
    t = pl.program_id(1)
    lo = SUBLANES - (width - 1)

    @pl.when(t == 0)
    def _():
        p_ref[lo:SUBLANES, :] = buf_ref[...]

    @pl.when(t > 0)
    def _():
        p_ref[0:SUBLANES, :] = p_ref[tt:tt + SUBLANES, :]

    p_ref[SUBLANES:SUBLANES + tt, :] = gc_ref[...] * h_ref[...]
    y = p_ref[lo:lo + tt, :] * w_ref[0:1, :]
    for i in range(1, width):
        y = y + p_ref[lo + i:lo + i + tt, :] * w_ref[i:i + 1, :]
    act_ref[...] = (gb_ref[...] * y).astype(BF16)
    nbuf_ref[...] = p_ref[SUBLANES + tt - (width - 1):SUBLANES + tt, :]


def _mixer_a(u3, buf, w, *, tt):
    b, t, _ = u3.shape
    width, c = w.shape
    assert t % tt == 0 and t >= width - 1 and c == W_A
    col = lambda k: pl.BlockSpec((None, tt, c), lambda bi, ti: (bi, ti, k))
    return pl.pallas_call(
        functools.partial(_mixer_a_kernel, tt=tt, width=width),
        grid=(b, t // tt),
        in_specs=[
            col(0), col(1), col(2),
            pl.BlockSpec((None, width - 1, c), lambda bi, ti: (bi, 0, 0)),
            pl.BlockSpec((width, c), lambda bi, ti: (0, 0)),
        ],
        out_specs=[
            pl.BlockSpec((None, tt, c), lambda bi, ti: (bi, ti, 0)),
            pl.BlockSpec((None, width - 1, c), lambda bi, ti: (bi, 0, 0)),
        ],
        out_shape=[
            jax.ShapeDtypeStruct((b, t, c), BF16),
            jax.ShapeDtypeStruct((b, width - 1, c), F32),
        ],
        scratch_shapes=[pltpu.VMEM((tt + SUBLANES, c), F32)],
        compiler_params=_cparams("parallel", "arbitrary"),
        name="mixer_a_conv",
    )(u3, u3, u3, buf, w)


def _suffix_sum_matrix(n, passes):
    s = lax.broadcasted_iota(jnp.int32, (passes * n, n), 0)
    s = jnp.where(s >= n, s - n, s)
    j = lax.broadcasted_iota(jnp.int32, (passes * n, n), 1)
    return jnp.where(s >= j, 1.0, 0.0).astype(BF16)


def _sb_softplus_sums(z, tri, mask):
    neg_abs = lax.bitcast_convert_type(lax.bitcast_convert_type(z, jnp.uint32) | jnp.uint32(0x80000000), F32)
    sp = jnp.maximum(z, 0.0) + jnp.log(1.0 + jnp.exp(neg_abs))
    if mask is not None:
        sp = jnp.where(mask, sp, 0.0)
    if tri.shape[0] == 2 * tri.shape[1]:
        addends = jnp.concatenate(_split_hi_lo(sp), axis=1)
    else:
        addends = sp.astype(BF16)
    return _dot(addends, tri), jnp.sum(sp, axis=-1, keepdims=True)


def _sb_weights(z, suffix, r_run, mask):
    a = jnp.exp(z - suffix - r_run)
    if mask is not None:
        a = jnp.where(mask, a, 0.0)
    return a.astype(BF16)


def _sb_prompt_kernel(q_ref, k_ref, v_ref, o_ref, kb_ref, vb_ref, acc_ref, r_ref, qh_ref, tri_ref,
                      za_ref, zb_ref, aa_ref, ab_ref, *, tq, scale):
    qi = pl.program_id(2)
    z_refs, a_refs = (za_ref, zb_ref), (aa_ref, ab_ref)
    kbn = LANES

    @pl.when(qi == 0)
    def _():
        kb_ref[...] = k_ref[...].astype(BF16)
        vb_ref[...] = v_ref[...].astype(BF16)

    lane = lax.broadcasted_iota(jnp.int32, (tq, LANES), 1)
    first = lane < HEAD_DIM_B
    q = q_ref[...] * scale
    qh_ref[0] = jnp.where(first, q, 0.0).astype(BF16)
    qh_ref[1] = jnp.where(first, 0.0, q).astype(BF16)
    tri_ref[...] = _suffix_sum_matrix(kbn, 1)
    row = lax.broadcasted_iota(jnp.int32, (tq, kbn), 0)
    col = lax.broadcasted_iota(jnp.int32, (tq, kbn), 1)
    acc_ref[...] = jnp.zeros(acc_ref.shape, F32)
    r_ref[...] = jnp.zeros(r_ref.shape, F32)

    def logits(kb2):
        k2 = kb_ref[pl.ds(pl.multiple_of(kb2 * 2 * kbn, 2 * kbn), 2 * kbn), :]
        pieces = []
        for h in range(2):
            z = _dot_nt(qh_ref[h], k2)
            pieces += [z[:, kbn:], z[:, :kbn]]
        return jnp.concatenate(pieces, axis=0)

    def weights(z, masks):
        mask = None if masks is None else jnp.concatenate(list(masks) * 2, axis=0)
        suffix, rs = _sb_softplus_sums(z, tri_ref[...], mask)
        r_parts = []
        for h in range(2):
            r_in = r_ref[h]
            r_mid = r_in + rs[2 * h * tq:(2 * h + 1) * tq]
            r_parts += [r_in, r_mid]
            r_ref[h] = r_mid + rs[(2 * h + 1) * tq:(2 * h + 2) * tq]
        return _sb_weights(z, suffix, jnp.concatenate(r_parts, axis=0), mask)

    def accumulate(a, kb2):
        v2 = vb_ref[pl.ds(pl.multiple_of(kb2 * 2 * kbn, 2 * kbn), 2 * kbn), :]
        for h in range(2):
            a_h = jnp.concatenate([a[(2 * h + 1) * tq:(2 * h + 2) * tq], a[2 * h * tq:(2 * h + 1) * tq]], axis=1)
            acc_ref[h] += _dot(a_h, v2)

    assert tq == 4 * kbn
    n_all = 2 * (qi + 1)
    blk = lambda s: jnp.maximum(n_all - 1 - s, 0)

    def step(s, half, masks, first_step=False):
        cur, nxt = half, 1 - half
        z_refs[nxt][...] = logits(blk(s + 1))
        if not first_step:
            accumulate(a_refs[nxt][...], blk(s - 1))
        a_refs[cur][...] = weights(z_refs[cur][...], masks)

    z_refs[0][...] = logits(blk(0))
    step(0, 0, (col + 3 * kbn < row, col + 2 * kbn < row), first_step=True)
    step(1, 1, (col + kbn < row, col < row))

    def trip(j, carry):
        for half in range(2):
            step(2 * j + 2 + half, half, None)
        return carry

    lax.fori_loop(0, qi, trip, 0)
    accumulate(a_refs[1][...], blk(n_all - 1))
    o_ref[...] = jnp.where(first, acc_ref[0], acc_ref[1]).astype(BF16)


def _sb_prompt(u3, *, tq):
    b, t, _ = u3.shape
    assert t % tq == 0 and tq % (2 * LANES) == 0
    pairs = W_B // LANES
    q_blk, k_blk, v_blk = (3 * W_A) // LANES, (3 * W_A + W_B) // LANES, (3 * W_A + 2 * W_B) // LANES
    return pl.pallas_call(
        functools.partial(_sb_prompt_kernel, tq=tq, scale=HEAD_DIM_B ** -0.5),
        grid=(b, pairs, t // tq),
        in_specs=[
            pl.BlockSpec((None, tq, LANES), lambda bi, hp, qi: (bi, qi, q_blk + hp)),
            pl.BlockSpec((None, t, LANES), lambda bi, hp, qi: (bi, 0, k_blk + hp)),
            pl.BlockSpec((None, t, LANES), lambda bi, hp, qi: (bi, 0, v_blk + hp)),
        ],
        out_specs=pl.BlockSpec((None, tq, LANES), lambda bi, hp, qi: (bi, qi, hp)),
        out_shape=jax.ShapeDtypeStruct((b, t, W_B), BF16),
        scratch_shapes=[
            pltpu.VMEM((t, LANES), BF16), pltpu.VMEM((t, LANES), BF16),
            pltpu.VMEM((2, tq, LANES), F32), pltpu.VMEM((2, tq, LANES), F32),
            pltpu.VMEM((2, tq, LANES), BF16), pltpu.VMEM((LANES, LANES), BF16),
            pltpu.VMEM((4 * tq, LANES), F32), pltpu.VMEM((4 * tq, LANES), F32),
            pltpu.VMEM((4 * tq, LANES), BF16), pltpu.VMEM((4 * tq, LANES), BF16),
        ],
        compiler_params=_cparams("parallel", "parallel", "arbitrary"),
        name="stickbreak_prompt",
    )(u3, u3, u3)


def _sb_sample_kernel(pt_ref, q_ref, ko_ref, vo_ref, *refs, n_new, page, pps, scale):
    del pt_ref
    k_refs, v_refs = refs[:pps], refs[pps:2 * pps]
    o_ref, qbd_ref, acc_ref, r_ref, own_ref = refs[2 * pps:]
    j = pl.program_id(1)
    nh, hd = N_HEADS_B, HEAD_DIM_B
    rows = nh * n_new
    tri2 = _suffix_sum_matrix(page, 2)

    @pl.when(j == 0)
    def _():
        qt = jnp.concatenate([q_ref[...] * scale] * nh, axis=0)
        rh = lax.broadcasted_iota(jnp.int32, (rows, W_B), 0) // n_new
        ch = lax.broadcasted_iota(jnp.int32, (rows, W_B), 1) // hd
        qbd_ref[...] = jnp.where(rh == ch, qt, 0.0).astype(BF16)
        r_ref[...] = jnp.zeros(r_ref.shape, F32)
        own_ref[...] = jnp.zeros(own_ref.shape, BF16)
        own_ref[0, 0:n_new, :] = ko_ref[...].astype(BF16)
        own_ref[1, 0:n_new, :] = vo_ref[...].astype(BF16)
        qpos = lax.broadcasted_iota(jnp.int32, (rows, page), 0) % n_new
        kpos = lax.broadcasted_iota(jnp.int32, (rows, page), 1)
        z = _dot_nt(qbd_ref[...], own_ref[0])
        suffix, rs = _sb_softplus_sums(z, tri2, kpos < qpos)
        r_ref[...] = jnp.broadcast_to(rs, r_ref.shape)
        acc_ref[...] = _dot(_sb_weights(z, suffix, 0.0, kpos < qpos), own_ref[1])

    def lanes(refs_):
        return jnp.concatenate([r[...].reshape(nh * hd, page).astype(BF16) for r in refs_], axis=1)

    z = _dot(qbd_ref[...], lanes(k_refs))
    z = jnp.concatenate([z[:, i * page:(i + 1) * page] for i in range(pps)], axis=0)
    suffix, rs = _sb_softplus_sums(z, tri2, None)
    r = r_ref[...]
    r_parts = []
    for i in range(pps):
        r_parts.append(r)
        r = r + rs[i * rows:(i + 1) * rows]
    r_ref[...] = r
    a = _sb_weights(z, suffix, jnp.concatenate(r_parts, axis=0), None)
    a = jnp.concatenate([a[i * rows:(i + 1) * rows] for i in range(pps)], axis=1)
    acc_ref[...] += _dot_nt(a, lanes(v_refs))

    @pl.when(j == pl.num_programs(1) - 1)
    def _():
        acc = acc_ref[...]
        ch = lax.broadcasted_iota(jnp.int32, (n_new, W_B), 1) // hd
        out = jnp.zeros((n_new, W_B), F32)
        for h in range(nh):
            out = out + jnp.where(ch == h, acc[h * n_new:(h + 1) * n_new, :], 0.0)
        o_ref[...] = out.astype(BF16)


def _sb_sample(u3, cache_kt, cache_vt, page_table, layer, *, pps):
    db, n_new, _ = u3.shape
    _, _, nh, hd, page = cache_kt.shape
    n_pages = page_table.shape[1]
    assert nh == N_HEADS_B and hd == HEAD_DIM_B and n_new % SUBLANES == 0 and n_pages % pps == 0
    blk = (3 * W_A) // W_B

    def page_spec(i):
        return pl.BlockSpec((None, None, nh, hd, page),
                            lambda bi, j, pt: (layer, pt[bi, n_pages - 1 - (j * pps + i)], 0, 0, 0))

    new_spec = lambda k: pl.BlockSpec((None, n_new, W_B), lambda bi, j, pt: (bi, 0, blk + k))
    grid_spec = pltpu.PrefetchScalarGridSpec(
        num_scalar_prefetch=1,
        grid=(db, n_pages // pps),
        in_specs=[new_spec(0), new_spec(1), new_spec(2)] + [page_spec(i) for i in range(pps)] * 2,
        out_specs=pl.BlockSpec((None, n_new, W_B), lambda bi, j, pt: (bi, 0, 0)),
        scratch_shapes=[
            pltpu.VMEM((nh * n_new, W_B), BF16),
            pltpu.VMEM((nh * n_new, W_B), F32),
            pltpu.VMEM((nh * n_new, LANES), F32),
            pltpu.VMEM((2, page, W_B), BF16),
        ],
    )
    return pl.pallas_call(
        functools.partial(_sb_sample_kernel, n_new=n_new, page=page, pps=pps, scale=hd ** -0.5),
        grid_spec=grid_spec,
        out_shape=jax.ShapeDtypeStruct((db, n_new, W_B), BF16),
        compiler_params=_cparams("parallel", "arbitrary"),
        name="stickbreak_sample",
    )(page_table, u3, u3, u3, *([cache_kt] * pps), *([cache_vt] * pps))


def _dot3(a_split, b_split):
    a_hi, a_lo = a_split
    b_hi, b_lo = b_split
    return _dot(jnp.concatenate([a_hi, a_hi, a_lo], axis=1), jnp.concatenate([b_hi, b_lo, b_hi], axis=0))


def _delta_kernel(q_ref, k_ref, v_ref, z_ref, bd_ref, buf_ref, cw_ref, gp_ref, nc_ref, s0_ref,
                  y_ref, s_ref, p_ref, *, tin, tt, chunk, width, hps):
    hg = pl.program_id(1)
    t = pl.program_id(2)
    lo = SUBLANES - (width - 1)
    dk = DK_C

    @pl.when(t == 0)
    def _():
        if tin < tt:
            p_ref[...] = jnp.zeros(p_ref.shape, F32)
        for hh in range(hps):
            for i in range(3):
                p_ref[hh, i, lo:SUBLANES, :] = buf_ref[hh, i]
        s_ref[...] = s0_ref[...]

    @pl.when(t > 0)
    def _():
        for hh in range(hps):
            for i in range(3):
                p_ref[hh, i, 0:SUBLANES, :] = p_ref[hh, i, tt:tt + SUBLANES, :]

    lane = lax.broadcasted_iota(jnp.int32, (tin, LANES), 1)
    bd = bd_ref[...]

    def head_inputs(hh):
        cols = slice(hh * LANES, (hh + 1) * LANES)
        conv = []
        for i, ref in enumerate((q_ref, k_ref, v_ref)):
            p_ref[hh, i, SUBLANES:SUBLANES + tin, :] = ref[:, cols]
            y = p_ref[hh, i, lo:lo + tt, :] * cw_ref[i, hh, 0:1, :]
            for w in range(1, width):
                y = y + p_ref[hh, i, lo + w:lo + w + tt, :] * cw_ref[i, hh, w:w + 1, :]
            conv.append(y * _sigmoid(y))
        qc, kc, vc = conv
        qn = qc * lax.rsqrt(jnp.sum(qc * qc, axis=-1, keepdims=True) + L2_EPS) * (dk ** -0.5)
        kn = kc * lax.rsqrt(jnp.sum(kc * kc, axis=-1, keepdims=True) + L2_EPS)
        h = hg * hps + hh
        neg_a = -jnp.exp(gp_ref[hh, 0:1, 0:1])
        dt_b = gp_ref[hh, 1:2, 0:1]
        b_col = jnp.sum(jnp.where(lane == h, bd, 0.0), axis=-1, keepdims=True)
        a_col = jnp.sum(jnp.where(lane == h + N_HEADS_C, bd, 0.0), axis=-1, keepdims=True)
        beta_col = _sigmoid(b_col)
        g_col = neg_a * _softplus(a_col + dt_b)
        if tin < tt:
            pad = jnp.zeros((tt - tin, 1), F32)
            beta_col = jnp.concatenate([beta_col, pad], axis=0)
            g_col = jnp.concatenate([g_col, pad], axis=0)
        return qn, kn, vc, g_col, beta_col

    ri = lax.broadcasted_iota(jnp.int32, (chunk, chunk), 0)
    ci = lax.broadcasted_iota(jnp.int32, (chunk, chunk), 1)
    incl = ri >= ci
    strict = ri > ci
    eye = jnp.where(ri == ci, 1.0, 0.0)
    n_doubling = (min(tin, chunk) - 1).bit_length() - 1

    def prepare(inputs, c):
        sl = slice(c * chunk, (c + 1) * chunk)
        qk, kk_, vk, gc_col, bc = (a[sl] for a in inputs)
        g_lanes = jnp.transpose(jnp.broadcast_to(gc_col, (chunk, chunk)))
        gcum_col = jnp.sum(jnp.where(incl, g_lanes, 0.0), axis=1, keepdims=True)
        gcum_row = jnp.sum(jnp.where(ri <= ci, gc_col, 0.0), axis=0, keepdims=True)
        dec_incl = jnp.where(incl, jnp.exp(jnp.where(incl, gcum_col - gcum_row, 0.0)), 0.0)
        k_bf = kk_.astype(BF16)
        e_col = jnp.exp(gcum_col)
        g_last = gcum_col[chunk - 1:chunk, :]
        return dict(
            m=bc * _dot_nt(k_bf, k_bf) * jnp.where(strict, dec_incl, 0.0),
            rhs=_split_hi_lo(jnp.concatenate([kk_ * (bc * e_col), vk * bc], axis=-1)),
            aqk=(_dot_nt(qk.astype(BF16), k_bf) * dec_incl).astype(BF16),
            q_dec=qk * e_col,
            k_dec=(kk_ * jnp.exp(g_last - gcum_col)).astype(BF16),
            g_end=jnp.exp(g_last))

    n_chunks = tt // chunk
    pre = []
    for hh in range(hps):
        inputs = head_inputs(hh)
        pre += [prepare(inputs, c) for c in range(n_chunks)]
    pw_s = [_split_hi_lo(-p["m"]) for p in pre]
    inv = [eye - p["m"] for p in pre]
    for _ in range(n_doubling):
        pw_s = [_split_hi_lo(_dot3(s_, s_)) for s_ in pw_s]
        inv = [iv + _dot3(_split_hi_lo(iv), s_) for iv, s_ in zip(inv, pw_s)]
    sols = [_dot3(_split_hi_lo(iv), p["rhs"]) for iv, p in zip(inv, pre)]

    steps = []
    for p, sol in zip(pre, sols):
        w_bf, uv_bf = sol[:, :dk].astype(BF16), sol[:, dk:].astype(BF16)
        steps.append(dict(
            s_mix=_dot_tn(p["k_dec"], w_bf).astype(BF16), s_add=_dot_tn(p["k_dec"], uv_bf),
            o_mix=(p["q_dec"] - _dot(p["aqk"], w_bf)).astype(BF16), o_add=_dot(p["aqk"], uv_bf),
            g_end=p["g_end"]))

    outs = [[] for _ in range(hps)]
    for c in range(n_chunks):
        for hh in range(hps):
            st = steps[hh * n_chunks + c]
            s = s_ref[hh]
            s_bf = s.astype(BF16)
            outs[hh].append(_dot(st["o_mix"], s_bf) + st["o_add"])
            s_ref[hh] = st["g_end"] * s + st["s_add"] - _dot(st["s_mix"], s_bf)
    for hh in range(hps):
        cols = slice(hh * LANES, (hh + 1) * LANES)
        o = (jnp.concatenate(outs[hh], axis=0) if n_chunks > 1 else outs[hh][0])[:tin]
        zg = z_ref[:, cols]
        y_ref[:, cols] = (_rms_scale(o) * nc_ref[...] * (zg * _sigmoid(zg))).astype(BF16)


def _delta(u3, buf, conv_w, gate_par, norm_c, s0, *, tin, tt, chunk, hps):
    b, t, _ = u3.shape
    width = conv_w.shape[0]
    nh = N_HEADS_C
    assert t % tin == 0 and tt % chunk == 0 and (tin == tt or t == tin) and tin >= width - 1 and nh % hps == 0
    qb = (3 * W_A + 3 * W_B) // LANES
    zb = (3 * W_A + 3 * W_B + W_QKV_C) // LANES
    bdb = COL_BD // LANES
    assert qb % hps == 0 and zb % hps == 0
    cw3 = conv_w.reshape(width, 3, nh, LANES).transpose(1, 2, 0, 3)
    buf3 = buf.reshape(b, width - 1, 3, nh, LANES).transpose(0, 3, 2, 1, 4)
    col = lambda k: pl.BlockSpec((None, tin, hps * LANES), lambda bi, hi, ti: (bi, ti, k // hps + hi))
    return pl.pallas_call(
        functools.partial(_delta_kernel, tin=tin, tt=tt, chunk=chunk, width=width, hps=hps),
        grid=(b, nh // hps, t // tin),
        in_specs=[
            col(qb), col(qb + nh), col(qb + 2 * nh), col(zb),
            pl.BlockSpec((None, tin, LANES), lambda bi, hi, ti: (bi, ti, bdb)),
            pl.BlockSpec((None, hps, 3, width - 1, LANES), lambda bi, hi, ti: (bi, hi, 0, 0, 0)),
            pl.BlockSpec((3, hps, width, LANES), lambda bi, hi, ti: (0, hi, 0, 0)),
            pl.BlockSpec((hps, 2, LANES), lambda bi, hi, ti: (hi, 0, 0)),
            pl.BlockSpec((1, DV_C), lambda bi, hi, ti: (0, 0)),
            pl.BlockSpec((None, hps, DK_C, DV_C), lambda bi, hi, ti: (bi, hi, 0, 0)),
        ],
        out_specs=[
            pl.BlockSpec((None, tin, hps * LANES), lambda bi, hi, ti: (bi, ti, hi)),
            pl.BlockSpec((None, hps, DK_C, DV_C), lambda bi, hi, ti: (bi, hi, 0, 0)),
        ],
        out_shape=[
            jax.ShapeDtypeStruct((b, t, nh * DV_C), BF16),
            jax.ShapeDtypeStruct((b, nh, DK_C, DV_C), F32),
        ],
        scratch_shapes=[pltpu.VMEM((hps, 3, tt + SUBLANES, LANES), F32)],
        compiler_params=_cparams("parallel", "parallel", "arbitrary"),
        name="gated_delta",
    )(u3, u3, u3, u3, u3, buf3, cw3, gate_par, norm_c, s0)


def _merge_kernel(x_ref, a_ref, b_ref, c_ref, g0_ref, g1_ref, g2_ref, wa_ref, wb_ref, wc_ref, wo_ref, o_ref):
    merged = (_sigmoid(g0_ref[...]) * _dot(a_ref[...], wa_ref[...])
              + _sigmoid(g1_ref[...]) * _dot(b_ref[...], wb_ref[...])
              + _sigmoid(g2_ref[...]) * _dot(c_ref[...], wc_ref[...]))
    o_ref[...] = x_ref[...] + _dot(merged.astype(BF16), wo_ref[...])


def _merge(x, act_a, act_b, act_c, u, wa, wb, wc, wo, *, tm):
    m, d = x.shape
    gb = COL_GATES // d
    row = lambda width, k=0: pl.BlockSpec((tm, width), lambda i: (i, k))
    full = lambda w: pl.BlockSpec(w.shape, lambda i: (0, 0))
    return pl.pallas_call(
        _merge_kernel,
        grid=(m // tm,),
        in_specs=[
            row(d), row(act_a.shape[1]), row(act_b.shape[1]), row(act_c.shape[1]),
            row(d, gb), row(d, gb + 1), row(d, gb + 2),
            full(wa), full(wb), full(wc), full(wo),
        ],
        out_specs=row(d),
        out_shape=jax.ShapeDtypeStruct((m, d), F32),
        compiler_params=_cparams("parallel"),
        name="merge_out_proj",
    )(x, act_a, act_b, act_c, u, u, u, wa, wb, wc, wo)


def _finish(x_ref, total, gf_ref, o_ref, final_norm):
    y = x_ref[...] + total
    if final_norm:
        y = _rms_scale(y) * gf_ref[...]
    o_ref[...] = y


def _ffn_kernel(x_ref, g_ref, gf_ref, wg_ref, wu_ref, wd_ref, o_ref, hn_ref, acc_ref, *, final_norm):
    f = pl.program_id(1)

    @pl.when(f == 0)
    def _():
        hn_ref[...] = (_rms_scale(x_ref[...]) * g_ref[...]).astype(BF16)
        acc_ref[...] = jnp.zeros(acc_ref.shape, F32)

    hn = hn_ref[...]
    a = _dot(hn, wg_ref[...])
    hidden = (a * _sigmoid(a) * _dot(hn, wu_ref[...])).astype(BF16)
    acc_ref[...] += _dot(hidden, wd_ref[...])

    @pl.when(f == pl.num_programs(1) - 1)
    def _():
        _finish(x_ref, acc_ref[...], gf_ref, o_ref, final_norm)


def _ffn(x, g, gf, wg, wu, wd, *, tm, tf, final_norm):
    m, d = x.shape
    ff = wg.shape[1]
    return pl.pallas_call(
        functools.partial(_ffn_kernel, final_norm=final_norm),
        grid=(m // tm, ff // tf),
        in_specs=[
            pl.BlockSpec((tm, d), lambda i, f: (i, 0)),
            pl.BlockSpec((1, d), lambda i, f: (0, 0)),
            pl.BlockSpec((1, d), lambda i, f: (0, 0)),
            pl.BlockSpec((d, tf), lambda i, f: (0, f)),
            pl.BlockSpec((d, tf), lambda i, f: (0, f)),
            pl.BlockSpec((tf, d), lambda i, f: (f, 0)),
        ],
        out_specs=pl.BlockSpec((tm, d), lambda i, f: (i, 0)),
        out_shape=jax.ShapeDtypeStruct((m, d), F32),
        scratch_shapes=[pltpu.VMEM((tm, d), BF16), pltpu.VMEM((tm, d), F32)],
        compiler_params=_cparams("parallel", "arbitrary"),
        name="dense_swiglu",
    )(x, g, gf, wg, wu, wd)


def _top2_gates(logits):
    lane = lax.broadcasted_iota(jnp.int32, logits.shape, 1).astype(F32)
    neg = -jnp.inf
    lg = jnp.where(lane < N_EXPERTS, logits, neg)
    m1 = jnp.max(lg, axis=-1, keepdims=True)
    i1 = jnp.min(jnp.where(lg == m1, lane, float(LANES)), axis=-1, keepdims=True)
    lg2 = jnp.where(lane == i1, neg, lg)
    m2 = jnp.max(lg2, axis=-1, keepdims=True)
    i2 = jnp.min(jnp.where(lg2 == m2, lane, float(LANES)), axis=-1, keepdims=True)
    e2 = jnp.exp(m2 - m1)
    w1 = 1.0 / (1.0 + e2)
    gate = jnp.where(lane == i1, w1, jnp.where(lane == i2, e2 * w1, 0.0))
    return gate, jnp.where(lane == i1, 1.0, jnp.where(lane == i2, 1.0, 0.0))


GATE, RANK, SEL, TABLE_ROWS = 0, N_EXPERTS, 2 * N_EXPERTS, 4 * N_EXPERTS


def _router_kernel(x_ref, g_ref, wr_ref, hn_ref, col_ref, row_ref, cnt_ref):
    tm = x_ref.shape[0]
    hn = _rms_scale(x_ref[...]) * g_ref[...]
    hn_ref[...] = hn.astype(BF16)
    gate, sel = _top2_gates(_dot_hi(hn, wr_ref[...]))
    gate_t = jnp.transpose(gate)[0:N_EXPERTS]
    sel_t = jnp.transpose(sel)[0:N_EXPERTS]
    earlier = jnp.where(lax.broadcasted_iota(jnp.int32, (tm, tm), 0) < lax.broadcasted_iota(jnp.int32, (tm, tm), 1),
                        1.0, 0.0).astype(BF16)
    rank_t = _dot(sel_t.astype(BF16), earlier)
    table = jnp.concatenate([gate_t, rank_t, sel_t, jnp.zeros((LANES - 3 * N_EXPERTS, tm), F32)], axis=0)
    row_ref[...] = table[0:TABLE_ROWS]
    col_ref[...] = jnp.transpose(table)
    cnt_ref[...] = jnp.broadcast_to(jnp.sum(sel, axis=0, keepdims=True), cnt_ref.shape).astype(jnp.int32)


def _router(x, g, w_router, *, tm):
    m, d = x.shape
    nt = m // tm
    return pl.pallas_call(
        _router_kernel,
        grid=(nt,),
        in_specs=[
            pl.BlockSpec((tm, d), lambda i: (i, 0)),
            pl.BlockSpec((1, d), lambda i: (0, 0)),
            pl.BlockSpec((d, LANES), lambda i: (0, 0)),
        ],
        out_specs=[
            pl.BlockSpec((tm, d), lambda i: (i, 0)),
            pl.BlockSpec((tm, LANES), lambda i: (i, 0)),
            pl.BlockSpec((None, TABLE_ROWS, tm), lambda i: (i, 0, 0)),
            pl.BlockSpec((None, SUBLANES, LANES), lambda i: (i, 0, 0)),
        ],
        out_shape=[
            jax.ShapeDtypeStruct((m, d), BF16),
            jax.ShapeDtypeStruct((m, LANES), F32),
            jax.ShapeDtypeStruct((nt, TABLE_ROWS, tm), F32),
            jax.ShapeDtypeStruct((nt, SUBLANES, LANES), jnp.int32),
        ],
        compiler_params=_cparams("parallel"),
        name="moe_router",
    )(x, g, w_router)


def _experts_kernel(cnt_ref, x_ref, hn_ref, col_ref, row_ref, gf_ref, wg_ref, wu_ref, wd_ref, o_ref,
                    hc_ref, yacc_ref, tot_ref, ecol_ref, *, sub_blocks, final_norm):
    i = pl.program_id(0)
    e = pl.program_id(1)
    f = pl.program_id(2)
    last_f = pl.num_programs(2) - 1
    tm = x_ref.shape[0]
    count = cnt_ref[i, e]

    @pl.when((e == 0) & (f == 0))
    def _():
        tot_ref[...] = jnp.zeros(tot_ref.shape, F32)

    @pl.when(f == last_f)
    def _():
        lane = lax.broadcasted_iota(jnp.int32, (tm, LANES), 1)
        col = col_ref[...]
        for n, k in enumerate((GATE, RANK, SEL)):
            ecol_ref[n] = jnp.broadcast_to(
                jnp.sum(jnp.where(lane == k + e, col, 0.0), axis=-1, keepdims=True), (tm, LANES))

    for start, rows in sub_blocks:
        @pl.when(start < count)
        def _(start=start, rows=rows):
            span = slice(start, start + rows)

            @pl.when(f == 0)
            def _():
                rank_row = row_ref[pl.ds(RANK + e, 1), :]
                sel_row = row_ref[pl.ds(SEL + e, 1), :]
                slot = (lax.broadcasted_iota(jnp.int32, (rows, tm), 0) + start).astype(F32)
                pick = jnp.where(rank_row == slot, sel_row, 0.0).astype(BF16)
                hc_ref[span, :] = _dot(pick, hn_ref[...]).astype(BF16)
                yacc_ref[span, :] = jnp.zeros((rows, yacc_ref.shape[1]), F32)

            hc = hc_ref[span, :]
            a = _dot(hc, wg_ref[...])
            hidden = (a * _sigmoid(a) * _dot(hc, wu_ref[...])).astype(BF16)
            yacc_ref[span, :] += _dot(hidden, wd_ref[...])

            @pl.when(f == last_f)
            def _():
                slot = (lax.broadcasted_iota(jnp.int32, (tm, rows), 1) + start).astype(F32)
                place = jnp.where(ecol_ref[1][:, 0:1] == slot, ecol_ref[2][:, 0:1], 0.0).astype(BF16)
                tot_ref[...] += ecol_ref[0][:, 0:1] * _dot(place, yacc_ref[span, :].astype(BF16))

    @pl.when((e == pl.num_programs(1) - 1) & (f == last_f))
    def _():
        _finish(x_ref, tot_ref[...], gf_ref, o_ref, final_norm)


def _moe(x, g, gf, w_router, wg, wu, wd, *, tm, tf, final_norm):
    m, d = x.shape
    ne, _, ff = wg.shape
    expected = (2 * tm) // ne
    sizes = [expected] if expected > MOE_ROWS else []
    sizes += [MOE_ROWS] * ((tm - sum(sizes)) // MOE_ROWS)
    assert sum(sizes) == tm and all(r % 16 == 0 for r in sizes)
    sub_blocks = tuple((sum(sizes[:n]), r) for n, r in enumerate(sizes))
    hn, col, row, cnt = _router(x, g, w_router, tm=tm)
    grid_spec = pltpu.PrefetchScalarGridSpec(
        num_scalar_prefetch=1,
        grid=(m // tm, ne, ff // tf),
        in_specs=[
            pl.BlockSpec((tm, d), lambda i, e, f, c: (i, 0)),
            pl.BlockSpec((tm, d), lambda i, e, f, c: (i, 0)),
            pl.BlockSpec((tm, LANES), lambda i, e, f, c: (i, 0)),
            pl.BlockSpec((None, TABLE_ROWS, tm), lambda i, e, f, c: (i, 0, 0)),
            pl.BlockSpec((1, d), lambda i, e, f, c: (0, 0)),
            pl.BlockSpec((None, d, tf), lambda i, e, f, c: (e, 0, f)),
            pl.BlockSpec((None, d, tf), lambda i, e, f, c: (e, 0, f)),
            pl.BlockSpec((None, tf, d), lambda i, e, f, c: (e, f, 0)),
        ],
        out_specs=pl.BlockSpec((tm, d), lambda i, e, f, c: (i, 0)),
        scratch_shapes=[
            pltpu.VMEM((tm, d), BF16), pltpu.VMEM((tm, d), F32),
            pltpu.VMEM((tm, d), F32), pltpu.VMEM((3, tm, LANES), F32),
        ],
    )
    return pl.pallas_call(
        functools.partial(_experts_kernel, sub_blocks=sub_blocks, final_norm=final_norm),
        grid_spec=grid_spec,
        out_shape=jax.ShapeDtypeStruct((m, d), F32),
        compiler_params=_cparams("parallel", "arbitrary", "arbitrary"),
        name="moe_experts",
    )(cnt[:, 0, :ne], x, hn, col, row, gf, wg, wu, wd)


def _tile(n, pref):
    return pref if n % pref == 0 else n


def _trunk(x3, attend, bufs_a, bufs_c, states, p):
    b, t, d = x3.shape
    m = b * t
    depth = p["w_proj"].shape[0]
    x = x3.reshape(m, d)
    tm = _tile(m, 1024)
    ks, vs, bas, bcs, ss = [], [], [], [], []
    for l in range(depth):
        u = _norm_matmul(x, p["norm_mix"][l], p["w_proj"][l], p["b_proj"][l], tm=tm, tn=1152)
        u3 = u.reshape(b, t, N_PROJ)
        act_a, nbuf_a = _mixer_a(u3, bufs_a[l], p["conv_a_w"][l], tt=_tile(t, 512))
        act_b = attend(u3, l)
        tin = min(t, 512)
        chunk = DELTA_CHUNK if tin >= DELTA_CHUNK else DELTA_CHUNK_SHORT
        act_c, s_new = _delta(u3, bufs_c[l], p["conv_c_w"][l], p["gate_par"][l], p["norm_c"][l], states[l],
                              tin=tin, tt=max(tin, chunk), chunk=chunk,
                              hps=2 if tin >= chunk else N_HEADS_C)
        act_c = act_c.reshape(m, -1)
        x = _merge(x, act_a.reshape(m, -1), act_b.reshape(m, -1), act_c, u,
                   p["w_br_a"][l], p["w_br_b"][l], p["w_br_c"][l], p["w_out"][l], tm=_tile(m, 256))
        final = l == depth - 1
        if l % 2 == 0:
            x = _ffn(x, p["norm_ffn"][l], p["norm_final"], p["w_ffn_gate"][l // 2], p["w_ffn_up"][l // 2],
                     p["w_ffn_down"][l // 2], tm=tm, tf=1408, final_norm=final)
        else:
            x = _moe(x, p["norm_ffn"][l], p["norm_final"], p["w_router"][l // 2], p["w_exp_gate"][l // 2],
                     p["w_exp_up"][l // 2], p["w_exp_down"][l // 2], tm=tm, tf=896, final_norm=final)
        ks.append(u3[:, :, 3 * W_A + W_B:3 * W_A + 2 * W_B].reshape(b, t, N_HEADS_B, HEAD_DIM_B))
        vs.append(u3[:, :, 3 * W_A + 2 * W_B:3 * W_A + 3 * W_B].reshape(b, t, N_HEADS_B, HEAD_DIM_B))
        bas.append(nbuf_a)
        wc = p["conv_c_w"].shape[1]
        bcs.append(u3[:, t - (wc - 1):, 3 * W_A + 3 * W_B:3 * W_A + 3 * W_B + W_QKV_C])
        ss.append(s_new)
    return (x.reshape(b, t, d), jnp.stack(ks), jnp.stack(vs), jnp.stack(bas), jnp.stack(bcs), jnp.stack(ss))


def kernel(x_prompt, x_sample, cache_k, cache_v, page_table, state_conv_a, state_conv_c, state_delta, norm_mix, w_in, b_in, conv_a_w, conv_c_w, a_log, dt_bias, norm_c, w_br_a, w_br_b, w_br_c, w_out, norm_ffn, w_ffn_gate, w_ffn_up, w_ffn_down, w_router, w_exp_gate, w_exp_up, w_exp_down, norm_final):
    depth, d, n_in = w_in.shape
    assert n_in == N_MAIN + 2 * N_HEADS_C + N_GATES
    col_gate_src = N_MAIN + 2 * N_HEADS_C
    pad = N_PROJ - n_in

    def reorder(a):
        return jnp.concatenate(
            [a[..., :N_MAIN], a[..., col_gate_src:], a[..., N_MAIN:col_gate_src],
             jnp.zeros(a.shape[:-1] + (pad,), a.dtype)], axis=-1)

    p = {
        "norm_mix": norm_mix[:, None, :],
        "w_proj": reorder(w_in).astype(BF16),
        "b_proj": reorder(b_in)[:, None, :],
        "conv_a_w": conv_a_w,
        "conv_c_w": conv_c_w,
        "gate_par": jnp.broadcast_to(jnp.stack([a_log, dt_bias], axis=-1)[..., None], (depth, N_HEADS_C, 2, LANES)),
        "norm_c": norm_c[:, None, :],
        "w_br_a": w_br_a.astype(BF16), "w_br_b": w_br_b.astype(BF16), "w_br_c": w_br_c.astype(BF16),
        "w_out": w_out.astype(BF16),
        "norm_ffn": norm_ffn[:, None, :],
        "norm_final": norm_final[None, :],
        "w_ffn_gate": w_ffn_gate.astype(BF16), "w_ffn_up": w_ffn_up.astype(BF16),
        "w_ffn_down": w_ffn_down.astype(BF16),
        "w_router": jnp.pad(w_router, ((0, 0), (0, 0), (0, LANES - w_router.shape[-1]))),
        "w_exp_gate": w_exp_gate.astype(BF16), "w_exp_up": w_exp_up.astype(BF16),
        "w_exp_down": w_exp_down.astype(BF16),
    }

    bp = x_prompt.shape[0]
    zero_a = jnp.zeros((depth, bp) + state_conv_a.shape[2:], F32)
    zero_c = jnp.zeros((depth, bp) + state_conv_c.shape[2:], F32)
    zero_s = jnp.zeros((depth, bp) + state_delta.shape[2:], F32)
    y_p, k_p, v_p, ca_p, cc_p, s_p = _trunk(
        x_prompt, lambda u3, l: _sb_prompt(u3, tq=512), zero_a, zero_c, zero_s, p)

    cache_kt = jnp.transpose(cache_k, (0, 1, 3, 4, 2))
    cache_vt = jnp.transpose(cache_v, (0, 1, 3, 4, 2))
    y_s, k_s, v_s, ca_s, cc_s, s_s = _trunk(
        x_sample, lambda u3, l: _sb_sample(u3, cache_kt, cache_vt, page_table, l, pps=16),
        state_conv_a, state_conv_c, state_delta, p)
    return (y_p, y_s, k_p, v_p, k_s, v_s, ca_p, ca_s, cc_p, cc_s, s_p, s_s)
```

```python
import functools

import jax
import jax.numpy as jnp
from jax import lax
from jax.experimental import pallas as pl
from jax.experimental.pallas import tpu as pltpu

F32 = jnp.float32
BF16 = jnp.bfloat16

RMS_EPS = 1e-6
L2_EPS = 1e-6

N_HEADS_B = 8
HEAD_DIM_B = 64
N_HEADS_C = 8
DK_C = 128
DV_C = 128
N_EXPERTS = 8
DELTA_CHUNK = 128
DELTA_CHUNK_SHORT = 64
MOE_ROWS = 128

LANES = 128
SUBLANES = 8
VMEM_LIMIT_BYTES = 56 * 1024 * 1024

W_A = 512
W_B = 512
W_QKV_C = 3072
W_Z_C = 1024
N_MAIN = 3 * W_A + 3 * W_B + W_QKV_C + W_Z_C
N_GATES = 3072
COL_GATES = N_MAIN
COL_BD = N_MAIN + N_GATES
N_PROJ = COL_BD + LANES


def _cparams(*sem):
    return pltpu.CompilerParams(dimension_semantics=sem, vmem_limit_bytes=VMEM_LIMIT_BYTES)


def _sigmoid(x):
    return 1.0 / (1.0 + jnp.exp(-x))


def _softplus(x):
    return jnp.maximum(x, 0.0) + jnp.log1p(jnp.exp(-jnp.abs(x)))


def _dot(a, b):
    return jnp.dot(a, b, preferred_element_type=F32)


def _dot_nt(a, b):
    return lax.dot_general(a, b, (((1,), (1,)), ((), ())), preferred_element_type=F32)


def _dot_tn(a, b):
    return lax.dot_general(a, b, (((0,), (0,)), ((), ())), preferred_element_type=F32)


def _dot_hi(a, b):
    return jnp.dot(a, b, preferred_element_type=F32, precision=lax.Precision.HIGHEST)


def _rms_scale(x):
    return x * lax.rsqrt(jnp.mean(x * x, axis=-1, keepdims=True) + RMS_EPS)


def _split_hi_lo(x):
    bits = lax.bitcast_convert_type(x, jnp.uint32) & jnp.uint32(0xFFFF0000)
    hi = lax.bitcast_convert_type(bits, F32)
    return hi.astype(BF16), (x - hi).astype(BF16)


def _norm_matmul_kernel(x_ref, g_ref, w_ref, b_ref, o_ref, xn_ref):
    @pl.when(pl.program_id(1) == 0)
    def _():
        xn_ref[...] = (_rms_scale(x_ref[...]) * g_ref[...]).astype(BF16)

    o_ref[...] = _dot(xn_ref[...], w_ref[...]) + b_ref[...]


def _norm_matmul(x, g, w, b, *, tm, tn):
    m, d = x.shape
    n = w.shape[1]
    return pl.pallas_call(
        _norm_matmul_kernel,
        grid=(m // tm, n // tn),
        in_specs=[
            pl.BlockSpec((tm, d), lambda i, j: (i, 0)),
            pl.BlockSpec((1, d), lambda i, j: (0, 0)),
            pl.BlockSpec((d, tn), lambda i, j: (0, j)),
            pl.BlockSpec((1, tn), lambda i, j: (0, j)),
        ],
        out_specs=pl.BlockSpec((tm, tn), lambda i, j: (i, j)),
        out_shape=jax.ShapeDtypeStruct((m, n), F32),
        scratch_shapes=[pltpu.VMEM((tm, d), BF16)],
        compiler_params=_cparams("parallel", "arbitrary"),
        name="norm_in_proj",
    )(x, g, w, b)


def _mixer_a_kernel(h_ref, gb_ref, gc_ref, buf_ref, w_ref, act_ref, nbuf_ref, p_ref, *, tt, width):
    t = pl.program_id(1)
    lo = SUBLANES - (width - 1)

    @pl.when(t == 0)
    def _():
        p_ref[lo:SUBLANES, :] = buf_ref[...]

    @pl.when(t > 0)
    def _():
        p_ref[0:SUBLANES, :] = p_ref[tt:tt + SUBLANES, :]

    p_ref[SUBLANES:SUBLANES + tt, :] = gc_ref[...] * h_ref[...]
    y = p_ref[lo:lo + tt, :] * w_ref[0:1, :]
    for i in range(1, width):
        y = y + p_ref[lo + i:lo + i + tt, :] * w_ref[i:i + 1, :]
    act_ref[...] = (gb_ref[...] * y).astype(BF16)
    nbuf_ref[...] = p_ref[SUBLANES + tt - (width - 1):SUBLANES + tt, :]


def _mixer_a(u3, buf, w, *, tt):
    b, t, _ = u3.shape
    width, c = w.shape
    assert t % tt == 0 and t >= width - 1 and c == W_A
    col = lambda k: pl.BlockSpec((None, tt, c), lambda bi, ti: (bi, ti, k))
    return pl.pallas_call(
        functools.partial(_mixer_a_kernel, tt=tt, width=width),
        grid=(b, t // tt),
        in_specs=[
            col(0), col(1), col(2),
            pl.BlockSpec((None, width - 1, c), lambda bi, ti: (bi, 0, 0)),
            pl.BlockSpec((width, c), lambda bi, ti: (0, 0)),
        ],
        out_specs=[
            pl.BlockSpec((None, tt, c), lambda bi, ti: (bi, ti, 0)),
            pl.BlockSpec((None, width - 1, c), lambda bi, ti: (bi, 0, 0)),
        ],
        out_shape=[
            jax.ShapeDtypeStruct((b, t, c), BF16),
            jax.ShapeDtypeStruct((b, width - 1, c), F32),
        ],
        scratch_shapes=[pltpu.VMEM((tt + SUBLANES, c), F32)],
        compiler_params=_cparams("parallel", "arbitrary"),
        name="mixer_a_conv",
    )(u3, u3, u3, buf, w)


def _suffix_sum_matrix(n, passes):
    s = lax.broadcasted_iota(jnp.int32, (passes * n, n), 0)
    s = jnp.where(s >= n, s - n, s)
    j = lax.broadcasted_iota(jnp.int32, (passes * n, n), 1)
    return jnp.where(s >= j, 1.0, 0.0).astype(BF16)


def _sb_softplus_sums(z, tri, mask):
    neg_abs = lax.bitcast_convert_type(lax.bitcast_convert_type(z, jnp.uint32) | jnp.uint32(0x80000000), F32)
    sp = jnp.maximum(z, 0.0) + jnp.log(1.0 + jnp.exp(neg_abs))
    if mask is not None:
        sp = jnp.where(mask, sp, 0.0)
    if tri.shape[0] == 2 * tri.shape[1]:
        addends = jnp.concatenate(_split_hi_lo(sp), axis=1)
    else:
        addends = sp.astype(BF16)
    return _dot(addends, tri), jnp.sum(sp, axis=-1, keepdims=True)


def _sb_weights(z, suffix, r_run, mask):
    a = jnp.exp(z - suffix - r_run)
    if mask is not None:
        a = jnp.where(mask, a, 0.0)
    return a.astype(BF16)


def _sb_prompt_kernel(q_ref, k_ref, v_ref, o_ref, kb_ref, vb_ref, acc_ref, r_ref, qh_ref, tri_ref,
                      za_ref, zb_ref, aa_ref, ab_ref, *, tq, scale):
    qi = pl.program_id(2)
    z_refs, a_refs = (za_ref, zb_ref), (aa_ref, ab_ref)
    kbn = LANES

    @pl.when(qi == 0)
    def _():
        kb_ref[...] = k_ref[...].astype(BF16)
        vb_ref[...] = v_ref[...].astype(BF16)

    lane = lax.broadcasted_iota(jnp.int32, (tq, LANES), 1)
    first = lane < HEAD_DIM_B
    q = q_ref[...] * scale
    qh_ref[0] = jnp.where(first, q, 0.0).astype(BF16)
    qh_ref[1] = jnp.where(first, 0.0, q).astype(BF16)
    tri_ref[...] = _suffix_sum_matrix(kbn, 1)
    row = lax.broadcasted_iota(jnp.int32, (tq, kbn), 0)
    col = lax.broadcasted_iota(jnp.int32, (tq, kbn), 1)
    acc_ref[...] = jnp.zeros(acc_ref.shape, F32)
    r_ref[...] = jnp.zeros(r_ref.shape, F32)

    def logits(kb2):
        k2 = kb_ref[pl.ds(pl.multiple_of(kb2 * 2 * kbn, 2 * kbn), 2 * kbn), :]
        pieces = []
        for h in range(2):
            z = _dot_nt(qh_ref[h], k2)
            pieces += [z[:, kbn:], z[:, :kbn]]
        return jnp.concatenate(pieces, axis=0)

    def weights(z, masks):
        mask = None if masks is None else jnp.concatenate(list(masks) * 2, axis=0)
        suffix, rs = _sb_softplus_sums(z, tri_ref[...], mask)
        r_parts = []
        for h in range(2):
            r_in = r_ref[h]
            r_mid = r_in + rs[2 * h * tq:(2 * h + 1) * tq]
            r_parts += [r_in, r_mid]
            r_ref[h] = r_mid + rs[(2 * h + 1) * tq:(2 * h + 2) * tq]
        return _sb_weights(z, suffix, jnp.concatenate(r_parts, axis=0), mask)

    def accumulate(a, kb2):
        v2 = vb_ref[pl.ds(pl.multiple_of(kb2 * 2 * kbn, 2 * kbn), 2 * kbn), :]
        for h in range(2):
            a_h = jnp.concatenate([a[(2 * h + 1) * tq:(2 * h + 2) * tq], a[2 * h * tq:(2 * h + 1) * tq]], axis=1)
            acc_ref[h] += _dot(a_h, v2)

    assert tq == 4 * kbn
    n_all = 2 * (qi + 1)
    blk = lambda s: jnp.maximum(n_all - 1 - s, 0)

    def step(s, half, masks, first_step=False):
        cur, nxt = half, 1 - half
        z_refs[nxt][...] = logits(blk(s + 1))
        if not first_step:
            accumulate(a_refs[nxt][...], blk(s - 1))
        a_refs[cur][...] = weights(z_refs[cur][...], masks)

    z_refs[0][...] = logits(blk(0))
    step(0, 0, (col + 3 * kbn < row, col + 2 * kbn < row), first_step=True)
    step(1, 1, (col + kbn < row, col < row))

    def trip(j, carry):
        for half in range(2):
            step(2 * j + 2 + half, half, None)
        return carry

    lax.fori_loop(0, qi, trip, 0)
    accumulate(a_refs[1][...], blk(n_all - 1))
    o_ref[...] = jnp.where(first, acc_ref[0], acc_ref[1]).astype(BF16)


def _sb_prompt(u3, *, tq):
    b, t, _ = u3.shape
    assert t % tq == 0 and tq % (2 * LANES) == 0
    pairs = W_B // LANES
    q_blk, k_blk, v_blk = (3 * W_A) // LANES, (3 * W_A + W_B) // LANES, (3 * W_A + 2 * W_B) // LANES
    return pl.pallas_call(
        functools.partial(_sb_prompt_kernel, tq=tq, scale=HEAD_DIM_B ** -0.5),
        grid=(b, pairs, t // tq),
        in_specs=[
            pl.BlockSpec((None, tq, LANES), lambda bi, hp, qi: (bi, qi, q_blk + hp)),
            pl.BlockSpec((None, t, LANES), lambda bi, hp, qi: (bi, 0, k_blk + hp)),
            pl.BlockSpec((None, t, LANES), lambda bi, hp, qi: (bi, 0, v_blk + hp)),
        ],
        out_specs=pl.BlockSpec((None, tq, LANES), lambda bi, hp, qi: (bi, qi, hp)),
        out_shape=jax.ShapeDtypeStruct((b, t, W_B), BF16),
        scratch_shapes=[
            pltpu.VMEM((t, LANES), BF16), pltpu.VMEM((t, LANES), BF16),
            pltpu.VMEM((2, tq, LANES), F32), pltpu.VMEM((2, tq, LANES), F32),
            pltpu.VMEM((2, tq, LANES), BF16), pltpu.VMEM((LANES, LANES), BF16),
            pltpu.VMEM((4 * tq, LANES), F32), pltpu.VMEM((4 * tq, LANES), F32),
            pltpu.VMEM((4 * tq, LANES), BF16), pltpu.VMEM((4 * tq, LANES), BF16),
        ],
        compiler_params=_cparams("parallel", "parallel", "arbitrary"),
        name="stickbreak_prompt",
    )(u3, u3, u3)


def _sb_sample_kernel(pt_ref, q_ref, ko_ref, vo_ref, *refs, n_new, page, pps, scale):
    del pt_ref
    k_refs, v_refs = refs[:pps], refs[pps:2 * pps]
    o_ref, qbd_ref, acc_ref, r_ref, own_ref = refs[2 * pps:]
    j = pl.program_id(1)
    nh, hd = N_HEADS_B, HEAD_DIM_B
    rows = nh * n_new
    tri2 = _suffix_sum_matrix(page, 2)

    @pl.when(j == 0)
    def _():
        qt = jnp.concatenate([q_ref[...] * scale] * nh, axis=0)
        rh = lax.broadcasted_iota(jnp.int32, (rows, W_B), 0) // n_new
        ch = lax.broadcasted_iota(jnp.int32, (rows, W_B), 1) // hd
        qbd_ref[...] = jnp.where(rh == ch, qt, 0.0).astype(BF16)
        r_ref[...] = jnp.zeros(r_ref.shape, F32)
        own_ref[...] = jnp.zeros(own_ref.shape, BF16)
        own_ref[0, 0:n_new, :] = ko_ref[...].astype(BF16)
        own_ref[1, 0:n_new, :] = vo_ref[...].astype(BF16)
        qpos = lax.broadcasted_iota(jnp.int32, (rows, page), 0) % n_new
        kpos = lax.broadcasted_iota(jnp.int32, (rows, page), 1)
        z = _dot_nt(qbd_ref[...], own_ref[0])
        suffix, rs = _sb_softplus_sums(z, tri2, kpos < qpos)
        r_ref[...] = jnp.broadcast_to(rs, r_ref.shape)
        acc_ref[...] = _dot(_sb_weights(z, suffix, 0.0, kpos < qpos), own_ref[1])

    def lanes(refs_):
        return jnp.concatenate([r[...].reshape(nh * hd, page).astype(BF16) for r in refs_], axis=1)

    z = _dot(qbd_ref[...], lanes(k_refs))
    z = jnp.concatenate([z[:, i * page:(i + 1) * page] for i in range(pps)], axis=0)
    suffix, rs = _sb_softplus_sums(z, tri2, None)
    r = r_ref[...]
    r_parts = []
    for i in range(pps):
        r_parts.append(r)
        r = r + rs[i * rows:(i + 1) * rows]
    r_ref[...] = r
    a = _sb_weights(z, suffix, jnp.concatenate(r_parts, axis=0), None)
    a = jnp.concatenate([a[i * rows:(i + 1) * rows] for i in range(pps)], axis=1)
    acc_ref[...] += _dot_nt(a, lanes(v_refs))

    @pl.when(j == pl.num_programs(1) - 1)
    def _():
        acc = acc_ref[...]
        ch = lax.broadcasted_iota(jnp.int32, (n_new, W_B), 1) // hd
        out = jnp.zeros((n_new, W_B), F32)
        for h in range(nh):
            out = out + jnp.where(ch == h, acc[h * n_new:(h + 1) * n_new, :], 0.0)
        o_ref[...] = out.astype(BF16)


def _sb_sample(u3, cache_kt, cache_vt, page_table, layer, *, pps):
    db, n_new, _ = u3.shape
    _, _, nh, hd, page = cache_kt.shape
    n_pages = page_table.shape[1]
    assert nh == N_HEADS_B and hd == HEAD_DIM_B and n_new % SUBLANES == 0 and n_pages % pps == 0
    blk = (3 * W_A) // W_B

    def page_spec(i):
        return pl.BlockSpec((None, None, nh, hd, page),
                            lambda bi, j, pt: (layer, pt[bi, n_pages - 1 - (j * pps + i)], 0, 0, 0))

    new_spec = lambda k: pl.BlockSpec((None, n_new, W_B), lambda bi, j, pt: (bi, 0, blk + k))
    grid_spec = pltpu.PrefetchScalarGridSpec(
        num_scalar_prefetch=1,
        grid=(db, n_pages // pps),
        in_specs=[new_spec(0), new_spec(1), new_spec(2)] + [page_spec(i) for i in range(pps)] * 2,
        out_specs=pl.BlockSpec((None, n_new, W_B), lambda bi, j, pt: (bi, 0, 0)),
        scratch_shapes=[
            pltpu.VMEM((nh * n_new, W_B), BF16),
            pltpu.VMEM((nh * n_new, W_B), F32),
            pltpu.VMEM((nh * n_new, LANES), F32),
            pltpu.VMEM((2, page, W_B), BF16),
        ],
    )
    return pl.pallas_call(
        functools.partial(_sb_sample_kernel, n_new=n_new, page=page, pps=pps, scale=hd ** -0.5),
        grid_spec=grid_spec,
        out_shape=jax.ShapeDtypeStruct((db, n_new, W_B), BF16),
        compiler_params=_cparams("parallel", "arbitrary"),
        name="stickbreak_sample",
    )(page_table, u3, u3, u3, *([cache_kt] * pps), *([cache_vt] * pps))


def _dot3(a_split, b_split):
    a_hi, a_lo = a_split
    b_hi, b_lo = b_split
    return _dot(jnp.concatenate([a_hi, a_hi, a_lo], axis=1), jnp.concatenate([b_hi, b_lo, b_hi], axis=0))


def _delta_kernel(q_ref, k_ref, v_ref, z_ref, bd_ref, buf_ref, cw_ref, gp_ref, nc_ref, s0_ref,
                  y_ref, s_ref, p_ref, *, tin, tt, chunk, width, hps):
    hg = pl.program_id(1)
    t = pl.program_id(2)
    lo = SUBLANES - (width - 1)
    dk = DK_C

    @pl.when(t == 0)
    def _():
        if tin < tt:
            p_ref[...] = jnp.zeros(p_ref.shape, F32)
        for hh in range(hps):
            for i in range(3):
                p_ref[hh, i, lo:SUBLANES, :] = buf_ref[hh, i]
        s_ref[...] = s0_ref[...]

    @pl.when(t > 0)
    def _():
        for hh in range(hps):
            for i in range(3):
                p_ref[hh, i, 0:SUBLANES, :] = p_ref[hh, i, tt:tt + SUBLANES, :]

    lane = lax.broadcasted_iota(jnp.int32, (tin, LANES), 1)
    bd = bd_ref[...]

    def head_inputs(hh):
        cols = slice(hh * LANES, (hh + 1) * LANES)
        conv = []
        for i, ref in enumerate((q_ref, k_ref, v_ref)):
            p_ref[hh, i, SUBLANES:SUBLANES + tin, :] = ref[:, cols]
            y = p_ref[hh, i, lo:lo + tt, :] * cw_ref[i, hh, 0:1, :]
            for w in range(1, width):
                y = y + p_ref[hh, i, lo + w:lo + w + tt, :] * cw_ref[i, hh, w:w + 1, :]
            conv.append(y * _sigmoid(y))
        qc, kc, vc = conv
        qn = qc * lax.rsqrt(jnp.sum(qc * qc, axis=-1, keepdims=True) + L2_EPS) * (dk ** -0.5)
        kn = kc * lax.rsqrt(jnp.sum(kc * kc, axis=-1, keepdims=True) + L2_EPS)
        h = hg * hps + hh
        neg_a = -jnp.exp(gp_ref[hh, 0:1, 0:1])
        dt_b = gp_ref[hh, 1:2, 0:1]
        b_col = jnp.sum(jnp.where(lane == h, bd, 0.0), axis=-1, keepdims=True)
        a_col = jnp.sum(jnp.where(lane == h + N_HEADS_C, bd, 0.0), axis=-1, keepdims=True)
        beta_col = _sigmoid(b_col)
        g_col = neg_a * _softplus(a_col + dt_b)
        if tin < tt:
            pad = jnp.zeros((tt - tin, 1), F32)
            beta_col = jnp.concatenate([beta_col, pad], axis=0)
            g_col = jnp.concatenate([g_col, pad], axis=0)
        return qn, kn, vc, g_col, beta_col

    ri = lax.broadcasted_iota(jnp.int32, (chunk, chunk), 0)
    ci = lax.broadcasted_iota(jnp.int32, (chunk, chunk), 1)
    incl = ri >= ci
    strict = ri > ci
    eye = jnp.where(ri == ci, 1.0, 0.0)
    n_doubling = (min(tin, chunk) - 1).bit_length() - 1

    def prepare(inputs, c):
        sl = slice(c * chunk, (c + 1) * chunk)
        qk, kk_, vk, gc_col, bc = (a[sl] for a in inputs)
        g_lanes = jnp.transpose(jnp.broadcast_to(gc_col, (chunk, chunk)))
        gcum_col = jnp.sum(jnp.where(incl, g_lanes, 0.0), axis=1, keepdims=True)
        gcum_row = jnp.sum(jnp.where(ri <= ci, gc_col, 0.0), axis=0, keepdims=True)
        dec_incl = jnp.where(incl, jnp.exp(jnp.where(incl, gcum_col - gcum_row, 0.0)), 0.0)
        k_bf = kk_.astype(BF16)
        e_col = jnp.exp(gcum_col)
        g_last = gcum_col[chunk - 1:chunk, :]
        return dict(
            m=bc * _dot_nt(k_bf, k_bf) * jnp.where(strict, dec_incl, 0.0),
            rhs=_split_hi_lo(jnp.concatenate([kk_ * (bc * e_col), vk * bc], axis=-1)),
            aqk=(_dot_nt(qk.astype(BF16), k_bf) * dec_incl).astype(BF16),
            q_dec=qk * e_col,
            k_dec=(kk_ * jnp.exp(g_last - gcum_col)).astype(BF16),
            g_end=jnp.exp(g_last))

    n_chunks = tt // chunk
    pre = []
    for hh in range(hps):
        inputs = head_inputs(hh)
        pre += [prepare(inputs, c) for c in range(n_chunks)]
    pw_s = [_split_hi_lo(-p["m"]) for p in pre]
    inv = [eye - p["m"] for p in pre]
    for _ in range(n_doubling):
        pw_s = [_split_hi_lo(_dot3(s_, s_)) for s_ in pw_s]
        inv = [iv + _dot3(_split_hi_lo(iv), s_) for iv, s_ in zip(inv, pw_s)]
    sols = [_dot3(_split_hi_lo(iv), p["rhs"]) for iv, p in zip(inv, pre)]

    steps = []
    for p, sol in zip(pre, sols):
        w_bf, uv_bf = sol[:, :dk].astype(BF16), sol[:, dk:].astype(BF16)
        steps.append(dict(
            s_mix=_dot_tn(p["k_dec"], w_bf).astype(BF16), s_add=_dot_tn(p["k_dec"], uv_bf),
            o_mix=(p["q_dec"] - _dot(p["aqk"], w_bf)).astype(BF16), o_add=_dot(p["aqk"], uv_bf),
            g_end=p["g_end"]))

    outs = [[] for _ in range(hps)]
    for c in range(n_chunks):
        for hh in range(hps):
            st = steps[hh * n_chunks + c]
            s = s_ref[hh]
            s_bf = s.astype(BF16)
            outs[hh].append(_dot(st["o_mix"], s_bf) + st["o_add"])
            s_ref[hh] = st["g_end"] * s + st["s_add"] - _dot(st["s_mix"], s_bf)
    for hh in range(hps):
        cols = slice(hh * LANES, (hh + 1) * LANES)
        o = (jnp.concatenate(outs[hh], axis=0) if n_chunks > 1 else outs[hh][0])[:tin]
        zg = z_ref[:, cols]
        y_ref[:, cols] = (_rms_scale(o) * nc_ref[...] * (zg * _sigmoid(zg))).astype(BF16)


def _delta(u3, buf, conv_w, gate_par, norm_c, s0, *, tin, tt, chunk, hps):
    b, t, _ = u3.shape
    width = conv_w.shape[0]
    nh = N_HEADS_C
    assert t % tin == 0 and tt % chunk == 0 and (tin == tt or t == tin) and tin >= width - 1 and nh % hps == 0
    qb = (3 * W_A + 3 * W_B) // LANES
    zb = (3 * W_A + 3 * W_B + W_QKV_C) // LANES
    bdb = COL_BD // LANES
    assert qb % hps == 0 and zb % hps == 0
    cw3 = conv_w.reshape(width, 3, nh, LANES).transpose(1, 2, 0, 3)
    buf3 = buf.reshape(b, width - 1, 3, nh, LANES).transpose(0, 3, 2, 1, 4)
    col = lambda k: pl.BlockSpec((None, tin, hps * LANES), lambda bi, hi, ti: (bi, ti, k // hps + hi))
    return pl.pallas_call(
        functools.partial(_delta_kernel, tin=tin, tt=tt, chunk=chunk, width=width, hps=hps),
        grid=(b, nh // hps, t // tin),
        in_specs=[
            col(qb), col(qb + nh), col(qb + 2 * nh), col(zb),
            pl.BlockSpec((None, tin, LANES), lambda bi, hi, ti: (bi, ti, bdb)),
            pl.BlockSpec((None, hps, 3, width - 1, LANES), lambda bi, hi, ti: (bi, hi, 0, 0, 0)),
            pl.BlockSpec((3, hps, width, LANES), lambda bi, hi, ti: (0, hi, 0, 0)),
            pl.BlockSpec((hps, 2, LANES), lambda bi, hi, ti: (hi, 0, 0)),
            pl.BlockSpec((1, DV_C), lambda bi, hi, ti: (0, 0)),
            pl.BlockSpec((None, hps, DK_C, DV_C), lambda bi, hi, ti: (bi, hi, 0, 0)),
        ],
        out_specs=[
            pl.BlockSpec((None, tin, hps * LANES), lambda bi, hi, ti: (bi, ti, hi)),
            pl.BlockSpec((None, hps, DK_C, DV_C), lambda bi, hi, ti: (bi, hi, 0, 0)),
        ],
        out_shape=[
            jax.ShapeDtypeStruct((b, t, nh * DV_C), BF16),
            jax.ShapeDtypeStruct((b, nh, DK_C, DV_C), F32),
        ],
        scratch_shapes=[pltpu.VMEM((hps, 3, tt + SUBLANES, LANES), F32)],
        compiler_params=_cparams("parallel", "parallel", "arbitrary"),
        name="gated_delta",
    )(u3, u3, u3, u3, u3, buf3, cw3, gate_par, norm_c, s0)


def _merge_kernel(x_ref, a_ref, b_ref, c_ref, g0_ref, g1_ref, g2_ref, wa_ref, wb_ref, wc_ref, wo_ref, o_ref):
    merged = (_sigmoid(g0_ref[...]) * _dot(a_ref[...], wa_ref[...])
              + _sigmoid(g1_ref[...]) * _dot(b_ref[...], wb_ref[...])
              + _sigmoid(g2_ref[...]) * _dot(c_ref[...], wc_ref[...]))
    o_ref[...] = x_ref[...] + _dot(merged.astype(BF16), wo_ref[...])


def _merge(x, act_a, act_b, act_c, u, wa, wb, wc, wo, *, tm):
    m, d = x.shape
    gb = COL_GATES // d
    row = lambda width, k=0: pl.BlockSpec((tm, width), lambda i: (i, k))
    full = lambda w: pl.BlockSpec(w.shape, lambda i: (0, 0))
    return pl.pallas_call(
        _merge_kernel,
        grid=(m // tm,),
        in_specs=[
            row(d), row(act_a.shape[1]), row(act_b.shape[1]), row(act_c.shape[1]),
            row(d, gb), row(d, gb + 1), row(d, gb + 2),
            full(wa), full(wb), full(wc), full(wo),
        ],
        out_specs=row(d),
        out_shape=jax.ShapeDtypeStruct((m, d), F32),
        compiler_params=_cparams("parallel"),
        name="merge_out_proj",
    )(x, act_a, act_b, act_c, u, u, u, wa, wb, wc, wo)


def _finish(x_ref, total, gf_ref, o_ref, final_norm):
    y = x_ref[...] + total
    if final_norm:
        y = _rms_scale(y) * gf_ref[...]
    o_ref[...] = y


def _ffn_kernel(x_ref, g_ref, gf_ref, wg_ref, wu_ref, wd_ref, o_ref, hn_ref, acc_ref, *, final_norm):
    f = pl.program_id(1)

    @pl.when(f == 0)
    def _():
        hn_ref[...] = (_rms_scale(x_ref[...]) * g_ref[...]).astype(BF16)
        acc_ref[...] = jnp.zeros(acc_ref.shape, F32)

    hn = hn_ref[...]
    a = _dot(hn, wg_ref[...])
    hidden = (a * _sigmoid(a) * _dot(hn, wu_ref[...])).astype(BF16)
    acc_ref[...] += _dot(hidden, wd_ref[...])

    @pl.when(f == pl.num_programs(1) - 1)
    def _():
        _finish(x_ref, acc_ref[...], gf_ref, o_ref, final_norm)


def _ffn(x, g, gf, wg, wu, wd, *, tm, tf, final_norm):
    m, d = x.shape
    ff = wg.shape[1]
    return pl.pallas_call(
        functools.partial(_ffn_kernel, final_norm=final_norm),
        grid=(m // tm, ff // tf),
        in_specs=[
            pl.BlockSpec((tm, d), lambda i, f: (i, 0)),
            pl.BlockSpec((1, d), lambda i, f: (0, 0)),
            pl.BlockSpec((1, d), lambda i, f: (0, 0)),
            pl.BlockSpec((d, tf), lambda i, f: (0, f)),
            pl.BlockSpec((d, tf), lambda i, f: (0, f)),
            pl.BlockSpec((tf, d), lambda i, f: (f, 0)),
        ],
        out_specs=pl.BlockSpec((tm, d), lambda i, f: (i, 0)),
        out_shape=jax.ShapeDtypeStruct((m, d), F32),
        scratch_shapes=[pltpu.VMEM((tm, d), BF16), pltpu.VMEM((tm, d), F32)],
        compiler_params=_cparams("parallel", "arbitrary"),
        name="dense_swiglu",
    )(x, g, gf, wg, wu, wd)


def _top2_gates(logits):
    lane = lax.broadcasted_iota(jnp.int32, logits.shape, 1).astype(F32)
    neg = -jnp.inf
    lg = jnp.where(lane < N_EXPERTS, logits, neg)
    m1 = jnp.max(lg, axis=-1, keepdims=True)
    i1 = jnp.min(jnp.where(lg == m1, lane, float(LANES)), axis=-1, keepdims=True)
    lg2 = jnp.where(lane == i1, neg, lg)
    m2 = jnp.max(lg2, axis=-1, keepdims=True)
    i2 = jnp.min(jnp.where(lg2 == m2, lane, float(LANES)), axis=-1, keepdims=True)
    e2 = jnp.exp(m2 - m1)
    w1 = 1.0 / (1.0 + e2)
    gate = jnp.where(lane == i1, w1, jnp.where(lane == i2, e2 * w1, 0.0))
    return gate, jnp.where(lane == i1, 1.0, jnp.where(lane == i2, 1.0, 0.0))


GATE, RANK, SEL, TABLE_ROWS = 0, N_EXPERTS, 2 * N_EXPERTS, 4 * N_EXPERTS


def _router_kernel(x_ref, g_ref, wr_ref, hn_ref, col_ref, row_ref, cnt_ref):
    tm = x_ref.shape[0]
    hn = _rms_scale(x_ref[...]) * g_ref[...]
    hn_ref[...] = hn.astype(BF16)
    gate, sel = _top2_gates(_dot_hi(hn, wr_ref[...]))
    gate_t = jnp.transpose(gate)[0:N_EXPERTS]
    sel_t = jnp.transpose(sel)[0:N_EXPERTS]
    earlier = jnp.where(lax.broadcasted_iota(jnp.int32, (tm, tm), 0) < lax.broadcasted_iota(jnp.int32, (tm, tm), 1),
                        1.0, 0.0).astype(BF16)
    rank_t = _dot(sel_t.astype(BF16), earlier)
    table = jnp.concatenate([gate_t, rank_t, sel_t, jnp.zeros((LANES - 3 * N_EXPERTS, tm), F32)], axis=0)
    row_ref[...] = table[0:TABLE_ROWS]
    col_ref[...] = jnp.transpose(table)
    cnt_ref[...] = jnp.broadcast_to(jnp.sum(sel, axis=0, keepdims=True), cnt_ref.shape).astype(jnp.int32)


def _router(x, g, w_router, *, tm):
    m, d = x.shape
    nt = m // tm
    return pl.pallas_call(
        _router_kernel,
        grid=(nt,),
        in_specs=[
            pl.BlockSpec((tm, d), lambda i: (i, 0)),
            pl.BlockSpec((1, d), lambda i: (0, 0)),
            pl.BlockSpec((d, LANES), lambda i: (0, 0)),
        ],
        out_specs=[
            pl.BlockSpec((tm, d), lambda i: (i, 0)),
            pl.BlockSpec((tm, LANES), lambda i: (i, 0)),
            pl.BlockSpec((None, TABLE_ROWS, tm), lambda i: (i, 0, 0)),
            pl.BlockSpec((None, SUBLANES, LANES), lambda i: (i, 0, 0)),
        ],
        out_shape=[
            jax.ShapeDtypeStruct((m, d), BF16),
            jax.ShapeDtypeStruct((m, LANES), F32),
            jax.ShapeDtypeStruct((nt, TABLE_ROWS, tm), F32),
            jax.ShapeDtypeStruct((nt, SUBLANES, LANES), jnp.int32),
        ],
        compiler_params=_cparams("parallel"),
        name="moe_router",
    )(x, g, w_router)


def _experts_kernel(cnt_ref, x_ref, hn_ref, col_ref, row_ref, gf_ref, wg_ref, wu_ref, wd_ref, o_ref,
                    hc_ref, yacc_ref, tot_ref, ecol_ref, *, sub_blocks, final_norm):
    i = pl.program_id(0)
    e = pl.program_id(1)
    f = pl.program_id(2)
    last_f = pl.num_programs(2) - 1
    tm = x_ref.shape[0]
    count = cnt_ref[i, e]

    @pl.when((e == 0) & (f == 0))
    def _():
        tot_ref[...] = jnp.zeros(tot_ref.shape, F32)

    @pl.when(f == last_f)
    def _():
        lane = lax.broadcasted_iota(jnp.int32, (tm, LANES), 1)
        col = col_ref[...]
        for n, k in enumerate((GATE, RANK, SEL)):
            ecol_ref[n] = jnp.broadcast_to(
                jnp.sum(jnp.where(lane == k + e, col, 0.0), axis=-1, keepdims=True), (tm, LANES))

    for start, rows in sub_blocks:
        @pl.when(start < count)
        def _(start=start, rows=rows):
            span = slice(start, start + rows)

            @pl.when(f == 0)
            def _():
                rank_row = row_ref[pl.ds(RANK + e, 1), :]
                sel_row = row_ref[pl.ds(SEL + e, 1), :]
                slot = (lax.broadcasted_iota(jnp.int32, (rows, tm), 0) + start).astype(F32)
                pick = jnp.where(rank_row == slot, sel_row, 0.0).astype(BF16)
                hc_ref[span, :] = _dot(pick, hn_ref[...]).astype(BF16)
                yacc_ref[span, :] = jnp.zeros((rows, yacc_ref.shape[1]), F32)

            hc = hc_ref[span, :]
            a = _dot(hc, wg_ref[...])
            hidden = (a * _sigmoid(a) * _dot(hc, wu_ref[...])).astype(BF16)
            yacc_ref[span, :] += _dot(hidden, wd_ref[...])

            @pl.when(f == last_f)
            def _():
                slot = (lax.broadcasted_iota(jnp.int32, (tm, rows), 1) + start).astype(F32)
                place = jnp.where(ecol_ref[1][:, 0:1] == slot, ecol_ref[2][:, 0:1], 0.0).astype(BF16)
                tot_ref[...] += ecol_ref[0][:, 0:1] * _dot(place, yacc_ref[span, :].astype(BF16))

    @pl.when((e == pl.num_programs(1) - 1) & (f == last_f))
    def _():
        _finish(x_ref, tot_ref[...], gf_ref, o_ref, final_norm)


def _moe(x, g, gf, w_router, wg, wu, wd, *, tm, tf, final_norm):
    m, d = x.shape
    ne, _, ff = wg.shape
    expected = (2 * tm) // ne
    sizes = [expected] if expected > MOE_ROWS else []
    sizes += [MOE_ROWS] * ((tm - sum(sizes)) // MOE_ROWS)
    assert sum(sizes) == tm and all(r % 16 == 0 for r in sizes)
    sub_blocks = tuple((sum(sizes[:n]), r) for n, r in enumerate(sizes))
    hn, col, row, cnt = _router(x, g, w_router, tm=tm)
    grid_spec = pltpu.PrefetchScalarGridSpec(
        num_scalar_prefetch=1,
        grid=(m // tm, ne, ff // tf),
        in_specs=[
            pl.BlockSpec((tm, d), lambda i, e, f, c: (i, 0)),
            pl.BlockSpec((tm, d), lambda i, e, f, c: (i, 0)),
            pl.BlockSpec((tm, LANES), lambda i, e, f, c: (i, 0)),
            pl.BlockSpec((None, TABLE_ROWS, tm), lambda i, e, f, c: (i, 0, 0)),
            pl.BlockSpec((1, d), lambda i, e, f, c: (0, 0)),
            pl.BlockSpec((None, d, tf), lambda i, e, f, c: (e, 0, f)),
            pl.BlockSpec((None, d, tf), lambda i, e, f, c: (e, 0, f)),
            pl.BlockSpec((None, tf, d), lambda i, e, f, c: (e, f, 0)),
        ],
        out_specs=pl.BlockSpec((tm, d), lambda i, e, f, c: (i, 0)),
        scratch_shapes=[
            pltpu.VMEM((tm, d), BF16), pltpu.VMEM((tm, d), F32),
            pltpu.VMEM((tm, d), F32), pltpu.VMEM((3, tm, LANES), F32),
        ],
    )
    return pl.pallas_call(
        functools.partial(_experts_kernel, sub_blocks=sub_blocks, final_norm=final_norm),
        grid_spec=grid_spec,
        out_shape=jax.ShapeDtypeStruct((m, d), F32),
        compiler_params=_cparams("parallel", "arbitrary", "arbitrary"),
        name="moe_experts",
    )(cnt[:, 0, :ne], x, hn, col, row, gf, wg, wu, wd)


def _tile(n, pref):
    return pref if n % pref == 0 else n


def _trunk(x3, attend, bufs_a, bufs_c, states, p):
    b, t, d = x3.shape
    m = b * t
    depth = p["w_proj"].shape[0]
    x = x3.reshape(m, d)
    tm = _tile(m, 1024)
    ks, vs, bas, bcs, ss = [], [], [], [], []
    for l in range(depth):
        u = _norm_matmul(x, p["norm_mix"][l], p["w_proj"][l], p["b_proj"][l], tm=_tile(m, 2048), tn=1152)
        u3 = u.reshape(b, t, N_PROJ)
        act_a, nbuf_a = _mixer_a(u3, bufs_a[l], p["conv_a_w"][l], tt=_tile(t, 512))
        act_b = attend(u3, l)
        tin = min(t, 512)
        chunk = DELTA_CHUNK if tin >= DELTA_CHUNK else DELTA_CHUNK_SHORT
        act_c, s_new = _delta(u3, bufs_c[l], p["conv_c_w"][l], p["gate_par"][l], p["norm_c"][l], states[l],
                              tin=tin, tt=max(tin, chunk), chunk=chunk,
                              hps=2 if tin >= chunk else N_HEADS_C)
        act_c = act_c.reshape(m, -1)
        x = _merge(x, act_a.reshape(m, -1), act_b.reshape(m, -1), act_c, u,
                   p["w_br_a"][l], p["w_br_b"][l], p["w_br_c"][l], p["w_out"][l], tm=_tile(m, 256))
        final = l == depth - 1
        if l % 2 == 0:
            x = _ffn(x, p["norm_ffn"][l], p["norm_final"], p["w_ffn_gate"][l // 2], p["w_ffn_up"][l // 2],
                     p["w_ffn_down"][l // 2], tm=tm, tf=1408, final_norm=final)
        else:
            x = _moe(x, p["norm_ffn"][l], p["norm_final"], p["w_router"][l // 2], p["w_exp_gate"][l // 2],
                     p["w_exp_up"][l // 2], p["w_exp_down"][l // 2], tm=tm, tf=896, final_norm=final)
        ks.append(u3[:, :, 3 * W_A + W_B:3 * W_A + 2 * W_B].reshape(b, t, N_HEADS_B, HEAD_DIM_B))
        vs.append(u3[:, :, 3 * W_A + 2 * W_B:3 * W_A + 3 * W_B].reshape(b, t, N_HEADS_B, HEAD_DIM_B))
        bas.append(nbuf_a)
        wc = p["conv_c_w"].shape[1]
        bcs.append(u3[:, t - (wc - 1):, 3 * W_A + 3 * W_B:3 * W_A + 3 * W_B + W_QKV_C])
        ss.append(s_new)
    return (x.reshape(b, t, d), jnp.stack(ks), jnp.stack(vs), jnp.stack(bas), jnp.stack(bcs), jnp.stack(ss))


def kernel(x_prompt, x_sample, cache_k, cache_v, page_table, state_conv_a, state_conv_c, state_delta, norm_mix, w_in, b_in, conv_a_w, conv_c_w, a_log, dt_bias, norm_c, w_br_a, w_br_b, w_br_c, w_out, norm_ffn, w_ffn_gate, w_ffn_up, w_ffn_down, w_router, w_exp_gate, w_exp_up, w_exp_down, norm_final):
    depth, d, n_in = w_in.shape
    assert n_in == N_MAIN + 2 * N_HEADS_C + N_GATES
    col_gate_src = N_MAIN + 2 * N_HEADS_C
    pad = N_PROJ - n_in

    def reorder(a):
        return jnp.concatenate(
            [a[..., :N_MAIN], a[..., col_gate_src:], a[..., N_MAIN:col_gate_src],
             jnp.zeros(a.shape[:-1] + (pad,), a.dtype)], axis=-1)

    p = {
        "norm_mix": norm_mix[:, None, :],
        "w_proj": reorder(w_in).astype(BF16),
        "b_proj": reorder(b_in)[:, None, :],
        "conv_a_w": conv_a_w,
        "conv_c_w": conv_c_w,
        "gate_par": jnp.broadcast_to(jnp.stack([a_log, dt_bias], axis=-1)[..., None], (depth, N_HEADS_C, 2, LANES)),
        "norm_c": norm_c[:, None, :],
        "w_br_a": w_br_a.astype(BF16), "w_br_b": w_br_b.astype(BF16), "w_br_c": w_br_c.astype(BF16),
        "w_out": w_out.astype(BF16),
        "norm_ffn": norm_ffn[:, None, :],
        "norm_final": norm_final[None, :],
        "w_ffn_gate": w_ffn_gate.astype(BF16), "w_ffn_up": w_ffn_up.astype(BF16),
        "w_ffn_down": w_ffn_down.astype(BF16),
        "w_router": jnp.pad(w_router, ((0, 0), (0, 0), (0, LANES - w_router.shape[-1]))),
        "w_exp_gate": w_exp_gate.astype(BF16), "w_exp_up": w_exp_up.astype(BF16),
        "w_exp_down": w_exp_down.astype(BF16),
    }

    bp = x_prompt.shape[0]
    zero_a = jnp.zeros((depth, bp) + state_conv_a.shape[2:], F32)
    zero_c = jnp.zeros((depth, bp) + state_conv_c.shape[2:], F32)
    zero_s = jnp.zeros((depth, bp) + state_delta.shape[2:], F32)
    y_p, k_p, v_p, ca_p, cc_p, s_p = _trunk(
        x_prompt, lambda u3, l: _sb_prompt(u3, tq=512), zero_a, zero_c, zero_s, p)

    cache_kt = jnp.transpose(cache_k, (0, 1, 3, 4, 2))
    cache_vt = jnp.transpose(cache_v, (0, 1, 3, 4, 2))
    y_s, k_s, v_s, ca_s, cc_s, s_s = _trunk(
        x_sample, lambda u3, l: _sb_sample(u3, cache_kt, cache_vt, page_table, l, pps=32),
        state_conv_a, state_conv_c, state_delta, p)
    return (y_p, y_s, k_p, v_p, k_s, v_s, ca_p, ca_s, cc_p, cc_s, s_p, s_s)
```

```python
import functools

import jax
import jax.numpy as jnp
from jax import lax
from jax.experimental import pallas as pl
from jax.experimental.pallas import tpu as pltpu

F32 = jnp.float32
BF16 = jnp.bfloat16

RMS_EPS = 1e-6
L2_EPS = 1e-6

N_HEADS_B = 8
HEAD_DIM_B = 64
N_HEADS_C = 8
DK_C = 128
DV_C = 128
N_EXPERTS = 8
DELTA_CHUNK = 128
DELTA_CHUNK_SHORT = 64
MOE_ROWS = 128

LANES = 128
SUBLANES = 8
VMEM_LIMIT_BYTES = 56 * 1024 * 1024

W_A = 512
W_B = 512
W_QKV_C = 3072
W_Z_C = 1024
N_MAIN = 3 * W_A + 3 * W_B + W_QKV_C + W_Z_C
N_GATES = 3072
COL_GATES = N_MAIN
COL_BD = N_MAIN + N_GATES
N_PROJ = COL_BD + LANES


def _cparams(*sem):
    return pltpu.CompilerParams(dimension_semantics=sem, vmem_limit_bytes=VMEM_LIMIT_BYTES)


def _sigmoid(x):
    return 1.0 / (1.0 + jnp.exp(-x))


def _softplus(x):
    return jnp.maximum(x, 0.0) + jnp.log1p(jnp.exp(-jnp.abs(x)))


def _dot(a, b):
    return jnp.dot(a, b, preferred_element_type=F32)


def _dot_nt(a, b):
    return lax.dot_general(a, b, (((1,), (1,)), ((), ())), preferred_element_type=F32)


def _dot_tn(a, b):
    return lax.dot_general(a, b, (((0,), (0,)), ((), ())), preferred_element_type=F32)


def _dot_hi(a, b):
    return jnp.dot(a, b, preferred_element_type=F32, precision=lax.Precision.HIGHEST)


def _rms_scale(x):
    return x * lax.rsqrt(jnp.mean(x * x, axis=-1, keepdims=True) + RMS_EPS)


def _split_hi_lo(x):
    bits = lax.bitcast_convert_type(x, jnp.uint32) & jnp.uint32(0xFFFF0000)
    hi = lax.bitcast_convert_type(bits, F32)
    return hi.astype(BF16), (x - hi).astype(BF16)


def _norm_matmul_kernel(x_ref, g_ref, w_ref, b_ref, o_ref, xn_ref):
    @pl.when(pl.program_id(1) == 0)
    def _():
        xn_ref[...] = (_rms_scale(x_ref[...]) * g_ref[...]).astype(BF16)

    o_ref[...] = _dot(xn_ref[...], w_ref[...]) + b_ref[...]


def _norm_matmul(x, g, w, b, layer, *, tm, tn):
    m, d = x.shape
    n = w.shape[2]
    return pl.pallas_call(
        _norm_matmul_kernel,
        grid=(m // tm, n // tn),
        in_specs=[
            pl.BlockSpec((tm, d), lambda i, j: (i, 0)),
            pl.BlockSpec((None, 1, d), lambda i, j: (layer, 0, 0)),
            pl.BlockSpec((None, d, tn), lambda i, j: (layer, 0, j)),
            pl.BlockSpec((None, 1, tn), lambda i, j: (layer, 0, j)),
        ],
        out_specs=pl.BlockSpec((tm, tn), lambda i, j: (i, j)),
        out_shape=jax.ShapeDtypeStruct((m, n), F32),
        scratch_shapes=[pltpu.VMEM((tm, d), BF16)],
        compiler_params=_cparams("parallel", "arbitrary"),
        name="norm_in_proj",
    )(x, g, w, b)


def _mixer_a_kernel(h_ref, gb_ref, gc_ref, buf_ref, w_ref, act_ref, nbuf_ref, p_ref, *, tt, width):
    t = pl.program_id(1)
    lo = SUBLANES - (width - 1)

    @pl.when(t == 0)
    def _():
        p_ref[lo:SUBLANES, :] = buf_ref[...]

    @pl.when(t > 0)
    def _():
        p_ref[0:SUBLANES, :] = p_ref[tt:tt + SUBLANES, :]

    p_ref[SUBLANES:SUBLANES + tt, :] = gc_ref[...] * h_ref[...]
    y = p_ref[lo:lo + tt, :] * w_ref[0:1, :]
    for i in range(1, width):
        y = y + p_ref[lo + i:lo + i + tt, :] * w_ref[i:i + 1, :]
    act_ref[...] = (gb_ref[...] * y).astype(BF16)
    nbuf_ref[...] = p_ref[SUBLANES + tt - (width - 1):SUBLANES + tt, :]


def _mixer_a(u3, buf, w, *, tt):
    b, t, _ = u3.shape
    width, c = w.shape
    assert t % tt == 0 and t >= width - 1 and c == W_A
    col = lambda k: pl.BlockSpec((None, tt, c), lambda bi, ti: (bi, ti, k))
    return pl.pallas_call(
        functools.partial(_mixer_a_kernel, tt=tt, width=width),
        grid=(b, t // tt),
        in_specs=[
            col(0), col(1), col(2),
            pl.BlockSpec((None, width - 1, c), lambda bi, ti: (bi, 0, 0)),
            pl.BlockSpec((width, c), lambda bi, ti: (0, 0)),
        ],
        out_specs=[
            pl.BlockSpec((None, tt, c), lambda bi, ti: (bi, ti, 0)),
            pl.BlockSpec((None, width - 1, c), lambda bi, ti: (bi, 0, 0)),
        ],
        out_shape=[
            jax.ShapeDtypeStruct((b, t, c), BF16),
            jax.ShapeDtypeStruct((b, width - 1, c), F32),
        ],
        scratch_shapes=[pltpu.VMEM((tt + SUBLANES, c), F32)],
        compiler_params=_cparams("parallel", "arbitrary"),
        name="mixer_a_conv",
    )(u3, u3, u3, buf, w)


def _suffix_sum_matrix(n, passes):
    s = lax.broadcasted_iota(jnp.int32, (passes * n, n), 0)
    s = jnp.where(s >= n, s - n, s)
    j = lax.broadcasted_iota(jnp.int32, (passes * n, n), 1)
    return jnp.where(s >= j, 1.0, 0.0).astype(BF16)


def _sb_softplus_sums(z, tri, mask):
    neg_abs = lax.bitcast_convert_type(lax.bitcast_convert_type(z, jnp.uint32) | jnp.uint32(0x80000000), F32)
    sp = jnp.maximum(z, 0.0) + jnp.log(1.0 + jnp.exp(neg_abs))
    if mask is not None:
        sp = jnp.where(mask, sp, 0.0)
    if tri.shape[0] == 2 * tri.shape[1]:
        addends = jnp.concatenate(_split_hi_lo(sp), axis=1)
    else:
        addends = sp.astype(BF16)
    return _dot(addends, tri), jnp.sum(sp, axis=-1, keepdims=True)


def _sb_weights(z, suffix, r_run, mask):
    a = jnp.exp(z - suffix - r_run)
    if mask is not None:
        a = jnp.where(mask, a, 0.0)
    return a.astype(BF16)


def _sb_prompt_kernel(q_ref, k_ref, v_ref, o_ref, kb_ref, vb_ref, acc_ref, r_ref, qh_ref, tri_ref,
                      za_ref, zb_ref, aa_ref, ab_ref, *, tq, scale):
    qi = pl.program_id(2)
    z_refs, a_refs = (za_ref, zb_ref), (aa_ref, ab_ref)
    kbn = LANES

    @pl.when(qi == 0)
    def _():
        kb_ref[...] = k_ref[...].astype(BF16)
        vb_ref[...] = v_ref[...].astype(BF16)

    lane = lax.broadcasted_iota(jnp.int32, (tq, LANES), 1)
    first = lane < HEAD_DIM_B
    q = q_ref[...] * scale
    qh_ref[0] = jnp.where(first, q, 0.0).astype(BF16)
    qh_ref[1] = jnp.where(first, 0.0, q).astype(BF16)
    tri_ref[...] = _suffix_sum_matrix(kbn, 1)
    row = lax.broadcasted_iota(jnp.int32, (tq, kbn), 0)
    col = lax.broadcasted_iota(jnp.int32, (tq, kbn), 1)
    acc_ref[...] = jnp.zeros(acc_ref.shape, F32)
    r_ref[...] = jnp.zeros(r_ref.shape, F32)

    def logits(kb2):
        k2 = kb_ref[pl.ds(pl.multiple_of(kb2 * 2 * kbn, 2 * kbn), 2 * kbn), :]
        pieces = []
        for h in range(2):
            z = _dot_nt(qh_ref[h], k2)
            pieces += [z[:, kbn:], z[:, :kbn]]
        return jnp.concatenate(pieces, axis=0)

    def weights(z, masks):
        mask = None if masks is None else jnp.concatenate(list(masks) * 2, axis=0)
        suffix, rs = _sb_softplus_sums(z, tri_ref[...], mask)
        r_parts = []
        for h in range(2):
            r_in = r_ref[h]
            r_mid = r_in + rs[2 * h * tq:(2 * h + 1) * tq]
            r_parts += [r_in, r_mid]
            r_ref[h] = r_mid + rs[(2 * h + 1) * tq:(2 * h + 2) * tq]
        return _sb_weights(z, suffix, jnp.concatenate(r_parts, axis=0), mask)

    def accumulate(a, kb2):
        v2 = vb_ref[pl.ds(pl.multiple_of(kb2 * 2 * kbn, 2 * kbn), 2 * kbn), :]
        for h in range(2):
            a_h = jnp.concatenate([a[(2 * h + 1) * tq:(2 * h + 2) * tq], a[2 * h * tq:(2 * h + 1) * tq]], axis=1)
            acc_ref[h] += _dot(a_h, v2)

    assert tq == 4 * kbn
    n_all = 2 * (qi + 1)
    blk = lambda s: jnp.maximum(n_all - 1 - s, 0)

    def step(s, half, masks, first_step=False):
        cur, nxt = half, 1 - half
        z_refs[nxt][...] = logits(blk(s + 1))
        if not first_step:
            accumulate(a_refs[nxt][...], blk(s - 1))
        a_refs[cur][...] = weights(z_refs[cur][...], masks)

    z_refs[0][...] = logits(blk(0))
    step(0, 0, (col + 3 * kbn < row, col + 2 * kbn < row), first_step=True)
    step(1, 1, (col + kbn < row, col < row))

    def trip(j, carry):
        for half in range(2):
            step(2 * j + 2 + half, half, None)
        return carry

    lax.fori_loop(0, qi, trip, 0)
    accumulate(a_refs[1][...], blk(n_all - 1))
    o_ref[...] = jnp.where(first, acc_ref[0], acc_ref[1]).astype(BF16)


def _sb_prompt(u3, *, tq):
    b, t, _ = u3.shape
    assert t % tq == 0 and tq % (2 * LANES) == 0
    pairs = W_B // LANES
    q_blk, k_blk, v_blk = (3 * W_A) // LANES, (3 * W_A + W_B) // LANES, (3 * W_A + 2 * W_B) // LANES
    return pl.pallas_call(
        functools.partial(_sb_prompt_kernel, tq=tq, scale=HEAD_DIM_B ** -0.5),
        grid=(b, pairs, t // tq),
        in_specs=[
            pl.BlockSpec((None, tq, LANES), lambda bi, hp, qi: (bi, qi, q_blk + hp)),
            pl.BlockSpec((None, t, LANES), lambda bi, hp, qi: (bi, 0, k_blk + hp)),
            pl.BlockSpec((None, t, LANES), lambda bi, hp, qi: (bi, 0, v_blk + hp)),
        ],
        out_specs=pl.BlockSpec((None, tq, LANES), lambda bi, hp, qi: (bi, qi, hp)),
        out_shape=jax.ShapeDtypeStruct((b, t, W_B), BF16),
        scratch_shapes=[
            pltpu.VMEM((t, LANES), BF16), pltpu.VMEM((t, LANES), BF16),
            pltpu.VMEM((2, tq, LANES), F32), pltpu.VMEM((2, tq, LANES), F32),
            pltpu.VMEM((2, tq, LANES), BF16), pltpu.VMEM((LANES, LANES), BF16),
            pltpu.VMEM((4 * tq, LANES), F32), pltpu.VMEM((4 * tq, LANES), F32),
            pltpu.VMEM((4 * tq, LANES), BF16), pltpu.VMEM((4 * tq, LANES), BF16),
        ],
        compiler_params=_cparams("parallel", "parallel", "arbitrary"),
        name="stickbreak_prompt",
    )(u3, u3, u3)


def _sb_sample_kernel(pt_ref, q_ref, ko_ref, vo_ref, *refs, n_new, page, pps, scale):
    del pt_ref
    k_refs, v_refs = refs[:pps], refs[pps:2 * pps]
    o_ref, qbd_ref, acc_ref, r_ref, own_ref = refs[2 * pps:]
    j = pl.program_id(1)
    nh, hd = N_HEADS_B, HEAD_DIM_B
    rows = nh * n_new
    tri2 = _suffix_sum_matrix(page, 2)

    @pl.when(j == 0)
    def _():
        qt = jnp.concatenate([q_ref[...] * scale] * nh, axis=0)
        rh = lax.broadcasted_iota(jnp.int32, (rows, W_B), 0) // n_new
        ch = lax.broadcasted_iota(jnp.int32, (rows, W_B), 1) // hd
        qbd_ref[...] = jnp.where(rh == ch, qt, 0.0).astype(BF16)
        r_ref[...] = jnp.zeros(r_ref.shape, F32)
        own_ref[...] = jnp.zeros(own_ref.shape, BF16)
        own_ref[0, 0:n_new, :] = ko_ref[...].astype(BF16)
        own_ref[1, 0:n_new, :] = vo_ref[...].astype(BF16)
        qpos = lax.broadcasted_iota(jnp.int32, (rows, page), 0) % n_new
        kpos = lax.broadcasted_iota(jnp.int32, (rows, page), 1)
        z = _dot_nt(qbd_ref[...], own_ref[0])
        suffix, rs = _sb_softplus_sums(z, tri2, kpos < qpos)
        r_ref[...] = jnp.broadcast_to(rs, r_ref.shape)
        acc_ref[...] = _dot(_sb_weights(z, suffix, 0.0, kpos < qpos), own_ref[1])

    def lanes(refs_):
        return jnp.concatenate([r[...].reshape(nh * hd, page).astype(BF16) for r in refs_], axis=1)

    z = _dot(qbd_ref[...], lanes(k_refs))
    z = jnp.concatenate([z[:, i * page:(i + 1) * page] for i in range(pps)], axis=0)
    suffix, rs = _sb_softplus_sums(z, tri2, None)
    r = r_ref[...]
    r_parts = []
    for i in range(pps):
        r_parts.append(r)
        r = r + rs[i * rows:(i + 1) * rows]
    r_ref[...] = r
    a = _sb_weights(z, suffix, jnp.concatenate(r_parts, axis=0), None)
    a = jnp.concatenate([a[i * rows:(i + 1) * rows] for i in range(pps)], axis=1)
    acc_ref[...] += _dot_nt(a, lanes(v_refs))

    @pl.when(j == pl.num_programs(1) - 1)
    def _():
        acc = acc_ref[...]
        ch = lax.broadcasted_iota(jnp.int32, (n_new, W_B), 1) // hd
        out = jnp.zeros((n_new, W_B), F32)
        for h in range(nh):
            out = out + jnp.where(ch == h, acc[h * n_new:(h + 1) * n_new, :], 0.0)
        o_ref[...] = out.astype(BF16)


def _sb_sample(u3, cache_kt, cache_vt, page_table, layer, *, pps):
    db, n_new, _ = u3.shape
    _, _, nh, hd, page = cache_kt.shape
    n_pages = page_table.shape[1]
    assert nh == N_HEADS_B and hd == HEAD_DIM_B and n_new % SUBLANES == 0 and n_pages % pps == 0
    blk = (3 * W_A) // W_B

    def page_spec(i):
        return pl.BlockSpec((None, None, nh, hd, page),
                            lambda bi, j, pt: (layer, pt[bi, n_pages - 1 - (j * pps + i)], 0, 0, 0))

    new_spec = lambda k: pl.BlockSpec((None, n_new, W_B), lambda bi, j, pt: (bi, 0, blk + k))
    grid_spec = pltpu.PrefetchScalarGridSpec(
        num_scalar_prefetch=1,
        grid=(db, n_pages // pps),
        in_specs=[new_spec(0), new_spec(1), new_spec(2)] + [page_spec(i) for i in range(pps)] * 2,
        out_specs=pl.BlockSpec((None, n_new, W_B), lambda bi, j, pt: (bi, 0, 0)),
        scratch_shapes=[
            pltpu.VMEM((nh * n_new, W_B), BF16),
            pltpu.VMEM((nh * n_new, W_B), F32),
            pltpu.VMEM((nh * n_new, LANES), F32),
            pltpu.VMEM((2, page, W_B), BF16),
        ],
    )
    return pl.pallas_call(
        functools.partial(_sb_sample_kernel, n_new=n_new, page=page, pps=pps, scale=hd ** -0.5),
        grid_spec=grid_spec,
        out_shape=jax.ShapeDtypeStruct((db, n_new, W_B), BF16),
        compiler_params=_cparams("parallel", "arbitrary"),
        name="stickbreak_sample",
    )(page_table, u3, u3, u3, *([cache_kt] * pps), *([cache_vt] * pps))


def _dot3(a_split, b_split):
    a_hi, a_lo = a_split
    b_hi, b_lo = b_split
    return _dot(jnp.concatenate([a_hi, a_hi, a_lo], axis=1), jnp.concatenate([b_hi, b_lo, b_hi], axis=0))


def _delta_kernel(q_ref, k_ref, v_ref, z_ref, bd_ref, buf_ref, cw_ref, gp_ref, nc_ref, s0_ref,
                  y_ref, s_ref, p_ref, *, tin, tt, chunk, width, hps):
    hg = pl.program_id(1)
    t = pl.program_id(2)
    lo = SUBLANES - (width - 1)
    dk = DK_C

    @pl.when(t == 0)
    def _():
        if tin < tt:
            p_ref[...] = jnp.zeros(p_ref.shape, F32)
        for hh in range(hps):
            for i in range(3):
                p_ref[hh, i, lo:SUBLANES, :] = buf_ref[hh, i]
        s_ref[...] = s0_ref[...]

    @pl.when(t > 0)
    def _():
        for hh in range(hps):
            for i in range(3):
                p_ref[hh, i, 0:SUBLANES, :] = p_ref[hh, i, tt:tt + SUBLANES, :]

    lane = lax.broadcasted_iota(jnp.int32, (tin, LANES), 1)
    bd = bd_ref[...]

    def head_inputs(hh):
        cols = slice(hh * LANES, (hh + 1) * LANES)
        conv = []
        for i, ref in enumerate((q_ref, k_ref, v_ref)):
            p_ref[hh, i, SUBLANES:SUBLANES + tin, :] = ref[:, cols]
            y = p_ref[hh, i, lo:lo + tt, :] * cw_ref[i, hh, 0:1, :]
            for w in range(1, width):
                y = y + p_ref[hh, i, lo + w:lo + w + tt, :] * cw_ref[i, hh, w:w + 1, :]
            conv.append(y * _sigmoid(y))
        qc, kc, vc = conv
        qn = qc * lax.rsqrt(jnp.sum(qc * qc, axis=-1, keepdims=True) + L2_EPS) * (dk ** -0.5)
        kn = kc * lax.rsqrt(jnp.sum(kc * kc, axis=-1, keepdims=True) + L2_EPS)
        h = hg * hps + hh
        neg_a = -jnp.exp(gp_ref[hh, 0:1, 0:1])
        dt_b = gp_ref[hh, 1:2, 0:1]
        b_col = jnp.sum(jnp.where(lane == h, bd, 0.0), axis=-1, keepdims=True)
        a_col = jnp.sum(jnp.where(lane == h + N_HEADS_C, bd, 0.0), axis=-1, keepdims=True)
        beta_col = _sigmoid(b_col)
        g_col = neg_a * _softplus(a_col + dt_b)
        if tin < tt:
            pad = jnp.zeros((tt - tin, 1), F32)
            beta_col = jnp.concatenate([beta_col, pad], axis=0)
            g_col = jnp.concatenate([g_col, pad], axis=0)
        return qn, kn, vc, g_col, beta_col

    ri = lax.broadcasted_iota(jnp.int32, (chunk, chunk), 0)
    ci = lax.broadcasted_iota(jnp.int32, (chunk, chunk), 1)
    incl = ri >= ci
    strict = ri > ci
    eye = jnp.where(ri == ci, 1.0, 0.0)
    n_doubling = (min(tin, chunk) - 1).bit_length() - 1

    def prepare(inputs, c):
        sl = slice(c * chunk, (c + 1) * chunk)
        qk, kk_, vk, gc_col, bc = (a[sl] for a in inputs)
        g_lanes = jnp.transpose(jnp.broadcast_to(gc_col, (chunk, chunk)))
        gcum_col = jnp.sum(jnp.where(incl, g_lanes, 0.0), axis=1, keepdims=True)
        gcum_row = jnp.sum(jnp.where(ri <= ci, gc_col, 0.0), axis=0, keepdims=True)
        dec_incl = jnp.where(incl, jnp.exp(jnp.where(incl, gcum_col - gcum_row, 0.0)), 0.0)
        k_bf = kk_.astype(BF16)
        e_col = jnp.exp(gcum_col)
        g_last = gcum_col[chunk - 1:chunk, :]
        return dict(
            m=bc * _dot_nt(k_bf, k_bf) * jnp.where(strict, dec_incl, 0.0),
            rhs=_split_hi_lo(jnp.concatenate([kk_ * (bc * e_col), vk * bc], axis=-1)),
            aqk=(_dot_nt(qk.astype(BF16), k_bf) * dec_incl).astype(BF16),
            q_dec=qk * e_col,
            k_dec=(kk_ * jnp.exp(g_last - gcum_col)).astype(BF16),
            g_end=jnp.exp(g_last))

    n_chunks = tt // chunk
    pre = []
    for hh in range(hps):
        inputs = head_inputs(hh)
        pre += [prepare(inputs, c) for c in range(n_chunks)]
    pw_s = [_split_hi_lo(-p["m"]) for p in pre]
    inv = [eye - p["m"] for p in pre]
    for _ in range(n_doubling):
        pw_s = [_split_hi_lo(_dot3(s_, s_)) for s_ in pw_s]
        inv = [iv + _dot3(_split_hi_lo(iv), s_) for iv, s_ in zip(inv, pw_s)]
    sols = [_dot3(_split_hi_lo(iv), p["rhs"]) for iv, p in zip(inv, pre)]

    steps = []
    for p, sol in zip(pre, sols):
        w_bf, uv_bf = sol[:, :dk].astype(BF16), sol[:, dk:].astype(BF16)
        steps.append(dict(
            s_mix=_dot_tn(p["k_dec"], w_bf).astype(BF16), s_add=_dot_tn(p["k_dec"], uv_bf),
            o_mix=(p["q_dec"] - _dot(p["aqk"], w_bf)).astype(BF16), o_add=_dot(p["aqk"], uv_bf),
            g_end=p["g_end"]))

    outs = [[] for _ in range(hps)]
    for c in range(n_chunks):
        for hh in range(hps):
            st = steps[hh * n_chunks + c]
            s = s_ref[hh]
            s_bf = s.astype(BF16)
            outs[hh].append(_dot(st["o_mix"], s_bf) + st["o_add"])
            s_ref[hh] = st["g_end"] * s + st["s_add"] - _dot(st["s_mix"], s_bf)
    for hh in range(hps):
        cols = slice(hh * LANES, (hh + 1) * LANES)
        o = (jnp.concatenate(outs[hh], axis=0) if n_chunks > 1 else outs[hh][0])[:tin]
        zg = z_ref[:, cols]
        y_ref[:, cols] = (_rms_scale(o) * nc_ref[...] * (zg * _sigmoid(zg))).astype(BF16)


def _delta(u3, buf, conv_w, gate_par, norm_c, s0, layer, *, tin, tt, chunk, hps):
    b, t, _ = u3.shape
    width = conv_w.shape[0]
    nh = N_HEADS_C
    assert t % tin == 0 and tt % chunk == 0 and (tin == tt or t == tin) and tin >= width - 1 and nh % hps == 0
    qb = (3 * W_A + 3 * W_B) // LANES
    zb = (3 * W_A + 3 * W_B + W_QKV_C) // LANES
    bdb = COL_BD // LANES
    assert qb % hps == 0 and zb % hps == 0
    cw3 = conv_w.reshape(width, 3, nh, LANES).transpose(1, 2, 0, 3)
    buf3 = buf.reshape(b, width - 1, 3, nh, LANES).transpose(0, 3, 2, 1, 4)
    col = lambda k: pl.BlockSpec((None, tin, hps * LANES), lambda bi, hi, ti: (bi, ti, k // hps + hi))
    return pl.pallas_call(
        functools.partial(_delta_kernel, tin=tin, tt=tt, chunk=chunk, width=width, hps=hps),
        grid=(b, nh // hps, t // tin),
        in_specs=[
            col(qb), col(qb + nh), col(qb + 2 * nh), col(zb),
            pl.BlockSpec((None, tin, LANES), lambda bi, hi, ti: (bi, ti, bdb)),
            pl.BlockSpec((None, hps, 3, width - 1, LANES), lambda bi, hi, ti: (bi, hi, 0, 0, 0)),
            pl.BlockSpec((3, hps, width, LANES), lambda bi, hi, ti: (0, hi, 0, 0)),
            pl.BlockSpec((hps, 2, LANES), lambda bi, hi, ti: (hi, 0, 0)),
            pl.BlockSpec((1, DV_C), lambda bi, hi, ti: (0, 0)),
            pl.BlockSpec((None, None, hps, DK_C, DV_C), lambda bi, hi, ti: (layer, bi, hi, 0, 0)),
        ],
        out_specs=[
            pl.BlockSpec((None, tin, hps * LANES), lambda bi, hi, ti: (bi, ti, hi)),
            pl.BlockSpec((None, hps, DK_C, DV_C), lambda bi, hi, ti: (bi, hi, 0, 0)),
        ],
        out_shape=[
            jax.ShapeDtypeStruct((b, t, nh * DV_C), BF16),
            jax.ShapeDtypeStruct((b, nh, DK_C, DV_C), F32),
        ],
        scratch_shapes=[pltpu.VMEM((hps, 3, tt + SUBLANES, LANES), F32)],
        compiler_params=_cparams("parallel", "parallel", "arbitrary"),
        name="gated_delta",
    )(u3, u3, u3, u3, u3, buf3, cw3, gate_par, norm_c, s0)


def _merge_kernel(x_ref, a_ref, b_ref, c_ref, g0_ref, g1_ref, g2_ref, wa_ref, wb_ref, wc_ref, wo_ref, o_ref):
    merged = (_sigmoid(g0_ref[...]) * _dot(a_ref[...], wa_ref[...])
              + _sigmoid(g1_ref[...]) * _dot(b_ref[...], wb_ref[...])
              + _sigmoid(g2_ref[...]) * _dot(c_ref[...], wc_ref[...]))
    o_ref[...] = x_ref[...] + _dot(merged.astype(BF16), wo_ref[...])


def _merge(x, act_a, act_b, act_c, u, wa, wb, wc, wo, *, tm):
    m, d = x.shape
    gb = COL_GATES // d
    row = lambda width, k=0: pl.BlockSpec((tm, width), lambda i: (i, k))
    full = lambda w: pl.BlockSpec(w.shape, lambda i: (0, 0))
    return pl.pallas_call(
        _merge_kernel,
        grid=(m // tm,),
        in_specs=[
            row(d), row(act_a.shape[1]), row(act_b.shape[1]), row(act_c.shape[1]),
            row(d, gb), row(d, gb + 1), row(d, gb + 2),
            full(wa), full(wb), full(wc), full(wo),
        ],
        out_specs=row(d),
        out_shape=jax.ShapeDtypeStruct((m, d), F32),
        compiler_params=_cparams("parallel"),
        name="merge_out_proj",
    )(x, act_a, act_b, act_c, u, u, u, wa, wb, wc, wo)


def _finish(x_ref, total, gf_ref, o_ref, final_norm):
    y = x_ref[...] + total
    if final_norm:
        y = _rms_scale(y) * gf_ref[...]
    o_ref[...] = y


def _ffn_kernel(x_ref, g_ref, gf_ref, wg_ref, wu_ref, wd_ref, o_ref, hn_ref, acc_ref, *, final_norm):
    f = pl.program_id(1)

    @pl.when(f == 0)
    def _():
        hn_ref[...] = (_rms_scale(x_ref[...]) * g_ref[...]).astype(BF16)
        acc_ref[...] = jnp.zeros(acc_ref.shape, F32)

    hn = hn_ref[...]
    a = _dot(hn, wg_ref[...])
    hidden = (a * _sigmoid(a) * _dot(hn, wu_ref[...])).astype(BF16)
    acc_ref[...] += _dot(hidden, wd_ref[...])

    @pl.when(f == pl.num_programs(1) - 1)
    def _():
        _finish(x_ref, acc_ref[...], gf_ref, o_ref, final_norm)


def _ffn(x, g, gf, wg, wu, wd, *, tm, tf, final_norm):
    m, d = x.shape
    ff = wg.shape[1]
    return pl.pallas_call(
        functools.partial(_ffn_kernel, final_norm=final_norm),
        grid=(m // tm, ff // tf),
        in_specs=[
            pl.BlockSpec((tm, d), lambda i, f: (i, 0)),
            pl.BlockSpec((1, d), lambda i, f: (0, 0)),
            pl.BlockSpec((1, d), lambda i, f: (0, 0)),
            pl.BlockSpec((d, tf), lambda i, f: (0, f)),
            pl.BlockSpec((d, tf), lambda i, f: (0, f)),
            pl.BlockSpec((tf, d), lambda i, f: (f, 0)),
        ],
        out_specs=pl.BlockSpec((tm, d), lambda i, f: (i, 0)),
        out_shape=jax.ShapeDtypeStruct((m, d), F32),
        scratch_shapes=[pltpu.VMEM((tm, d), BF16), pltpu.VMEM((tm, d), F32)],
        compiler_params=_cparams("parallel", "arbitrary"),
        name="dense_swiglu",
    )(x, g, gf, wg, wu, wd)


def _top2_gates(logits):
    lane = lax.broadcasted_iota(jnp.int32, logits.shape, 1).astype(F32)
    neg = -jnp.inf
    lg = jnp.where(lane < N_EXPERTS, logits, neg)
    m1 = jnp.max(lg, axis=-1, keepdims=True)
    i1 = jnp.min(jnp.where(lg == m1, lane, float(LANES)), axis=-1, keepdims=True)
    lg2 = jnp.where(lane == i1, neg, lg)
    m2 = jnp.max(lg2, axis=-1, keepdims=True)
    i2 = jnp.min(jnp.where(lg2 == m2, lane, float(LANES)), axis=-1, keepdims=True)
    e2 = jnp.exp(m2 - m1)
    w1 = 1.0 / (1.0 + e2)
    gate = jnp.where(lane == i1, w1, jnp.where(lane == i2, e2 * w1, 0.0))
    return gate, jnp.where(lane == i1, 1.0, jnp.where(lane == i2, 1.0, 0.0))


GATE, RANK, SEL, TABLE_ROWS = 0, N_EXPERTS, 2 * N_EXPERTS, 4 * N_EXPERTS


def _router_kernel(x_ref, g_ref, wr_ref, hn_ref, col_ref, row_ref, cnt_ref):
    tm = x_ref.shape[0]
    hn = _rms_scale(x_ref[...]) * g_ref[...]
    hn_ref[...] = hn.astype(BF16)
    gate, sel = _top2_gates(_dot_hi(hn, wr_ref[...]))
    gate_t = jnp.transpose(gate)[0:N_EXPERTS]
    sel_t = jnp.transpose(sel)[0:N_EXPERTS]
    earlier = jnp.where(lax.broadcasted_iota(jnp.int32, (tm, tm), 0) < lax.broadcasted_iota(jnp.int32, (tm, tm), 1),
                        1.0, 0.0).astype(BF16)
    rank_t = _dot(sel_t.astype(BF16), earlier)
    table = jnp.concatenate([gate_t, rank_t, sel_t, jnp.zeros((LANES - 3 * N_EXPERTS, tm), F32)], axis=0)
    row_ref[...] = table[0:TABLE_ROWS]
    col_ref[...] = jnp.transpose(table)
    cnt_ref[...] = jnp.broadcast_to(jnp.sum(sel, axis=0, keepdims=True), cnt_ref.shape).astype(jnp.int32)


def _router(x, g, w_router, *, tm):
    m, d = x.shape
    nt = m // tm
    return pl.pallas_call(
        _router_kernel,
        grid=(nt,),
        in_specs=[
            pl.BlockSpec((tm, d), lambda i: (i, 0)),
            pl.BlockSpec((1, d), lambda i: (0, 0)),
            pl.BlockSpec((d, LANES), lambda i: (0, 0)),
        ],
        out_specs=[
            pl.BlockSpec((tm, d), lambda i: (i, 0)),
            pl.BlockSpec((tm, LANES), lambda i: (i, 0)),
            pl.BlockSpec((None, TABLE_ROWS, tm), lambda i: (i, 0, 0)),
            pl.BlockSpec((None, SUBLANES, LANES), lambda i: (i, 0, 0)),
        ],
        out_shape=[
            jax.ShapeDtypeStruct((m, d), BF16),
            jax.ShapeDtypeStruct((m, LANES), F32),
            jax.ShapeDtypeStruct((nt, TABLE_ROWS, tm), F32),
            jax.ShapeDtypeStruct((nt, SUBLANES, LANES), jnp.int32),
        ],
        compiler_params=_cparams("parallel"),
        name="moe_router",
    )(x, g, w_router)


def _experts_kernel(cnt_ref, x_ref, hn_ref, col_ref, row_ref, gf_ref, wg_ref, wu_ref, wd_ref, o_ref,
                    hc_ref, yacc_ref, tot_ref, ecol_ref, *, sub_blocks, final_norm):
    i = pl.program_id(0)
    e = pl.program_id(1)
    f = pl.program_id(2)
    last_f = pl.num_programs(2) - 1
    tm = x_ref.shape[0]
    count = cnt_ref[i, e]

    @pl.when((e == 0) & (f == 0))
    def _():
        tot_ref[...] = jnp.zeros(tot_ref.shape, F32)

    @pl.when(f == last_f)
    def _():
        lane = lax.broadcasted_iota(jnp.int32, (tm, LANES), 1)
        col = col_ref[...]
        for n, k in enumerate((GATE, RANK, SEL)):
            ecol_ref[n] = jnp.broadcast_to(
                jnp.sum(jnp.where(lane == k + e, col, 0.0), axis=-1, keepdims=True), (tm, LANES))

    for start, rows in sub_blocks:
        @pl.when(start < count)
        def _(start=start, rows=rows):
            span = slice(start, start + rows)

            @pl.when(f == 0)
            def _():
                rank_row = row_ref[pl.ds(RANK + e, 1), :]
                sel_row = row_ref[pl.ds(SEL + e, 1), :]
                slot = (lax.broadcasted_iota(jnp.int32, (rows, tm), 0) + start).astype(F32)
                pick = jnp.where(rank_row == slot, sel_row, 0.0).astype(BF16)
                hc_ref[span, :] = _dot(pick, hn_ref[...]).astype(BF16)
                yacc_ref[span, :] = jnp.zeros((rows, yacc_ref.shape[1]), F32)

            hc = hc_ref[span, :]
            a = _dot(hc, wg_ref[...])
            hidden = (a * _sigmoid(a) * _dot(hc, wu_ref[...])).astype(BF16)
            yacc_ref[span, :] += _dot(hidden, wd_ref[...])

            @pl.when(f == last_f)
            def _():
                slot = (lax.broadcasted_iota(jnp.int32, (tm, rows), 1) + start).astype(F32)
                place = jnp.where(ecol_ref[1][:, 0:1] == slot, ecol_ref[2][:, 0:1], 0.0).astype(BF16)
                tot_ref[...] += ecol_ref[0][:, 0:1] * _dot(place, yacc_ref[span, :].astype(BF16))

    @pl.when((e == pl.num_programs(1) - 1) & (f == last_f))
    def _():
        _finish(x_ref, tot_ref[...], gf_ref, o_ref, final_norm)


def _moe(x, g, gf, w_router, wg, wu, wd, *, tm, tf, final_norm):
    m, d = x.shape
    ne, _, ff = wg.shape
    expected = (2 * tm) // ne
    sizes = [expected] if expected > MOE_ROWS else []
    sizes += [MOE_ROWS] * ((tm - sum(sizes)) // MOE_ROWS)
    assert sum(sizes) == tm and all(r % 16 == 0 for r in sizes)
    sub_blocks = tuple((sum(sizes[:n]), r) for n, r in enumerate(sizes))
    hn, col, row, cnt = _router(x, g, w_router, tm=tm)
    grid_spec = pltpu.PrefetchScalarGridSpec(
        num_scalar_prefetch=1,
        grid=(m // tm, ne, ff // tf),
        in_specs=[
            pl.BlockSpec((tm, d), lambda i, e, f, c: (i, 0)),
            pl.BlockSpec((tm, d), lambda i, e, f, c: (i, 0)),
            pl.BlockSpec((tm, LANES), lambda i, e, f, c: (i, 0)),
            pl.BlockSpec((None, TABLE_ROWS, tm), lambda i, e, f, c: (i, 0, 0)),
            pl.BlockSpec((1, d), lambda i, e, f, c: (0, 0)),
            pl.BlockSpec((None, d, tf), lambda i, e, f, c: (e, 0, f)),
            pl.BlockSpec((None, d, tf), lambda i, e, f, c: (e, 0, f)),
            pl.BlockSpec((None, tf, d), lambda i, e, f, c: (e, f, 0)),
        ],
        out_specs=pl.BlockSpec((tm, d), lambda i, e, f, c: (i, 0)),
        scratch_shapes=[
            pltpu.VMEM((tm, d), BF16), pltpu.VMEM((tm, d), F32),
            pltpu.VMEM((tm, d), F32), pltpu.VMEM((3, tm, LANES), F32),
        ],
    )
    return pl.pallas_call(
        functools.partial(_experts_kernel, sub_blocks=sub_blocks, final_norm=final_norm),
        grid_spec=grid_spec,
        out_shape=jax.ShapeDtypeStruct((m, d), F32),
        compiler_params=_cparams("parallel", "arbitrary", "arbitrary"),
        name="moe_experts",
    )(cnt[:, 0, :ne], x, hn, col, row, gf, wg, wu, wd)


def _tile(n, pref):
    return pref if n % pref == 0 else n


def _trunk(x3, attend, bufs_a, bufs_c, states, p):
    b, t, d = x3.shape
    m = b * t
    depth = p["w_proj"].shape[0]
    x = x3.reshape(m, d)
    tm = _tile(m, 1024)
    ks, vs, bas, bcs, ss = [], [], [], [], []
    for l in range(depth):
        u = _norm_matmul(x, p["norm_mix"], p["w_proj"], p["b_proj"], l, tm=_tile(m, 2048), tn=1152)
        u3 = u.reshape(b, t, N_PROJ)
        act_a, nbuf_a = _mixer_a(u3, bufs_a[l], p["conv_a_w"][l], tt=_tile(t, 512))
        act_b = attend(u3, l)
        tin = min(t, 512)
        chunk = DELTA_CHUNK if tin >= DELTA_CHUNK else DELTA_CHUNK_SHORT
        act_c, s_new = _delta(u3, bufs_c[l], p["conv_c_w"][l], p["gate_par"][l], p["norm_c"][l], states, l,
                              tin=tin, tt=max(tin, chunk), chunk=chunk,
                              hps=2 if tin >= chunk else N_HEADS_C)
        act_c = act_c.reshape(m, -1)
        x = _merge(x, act_a.reshape(m, -1), act_b.reshape(m, -1), act_c, u,
                   p["w_br_a"][l], p["w_br_b"][l], p["w_br_c"][l], p["w_out"][l], tm=_tile(m, 256))
        final = l == depth - 1
        if l % 2 == 0:
            x = _ffn(x, p["norm_ffn"][l], p["norm_final"], p["w_ffn_gate"][l // 2], p["w_ffn_up"][l // 2],
                     p["w_ffn_down"][l // 2], tm=tm, tf=1408, final_norm=final)
        else:
            x = _moe(x, p["norm_ffn"][l], p["norm_final"], p["w_router"][l // 2], p["w_exp_gate"][l // 2],
                     p["w_exp_up"][l // 2], p["w_exp_down"][l // 2], tm=tm, tf=896, final_norm=final)
        ks.append(u3[:, :, 3 * W_A + W_B:3 * W_A + 2 * W_B].reshape(b, t, N_HEADS_B, HEAD_DIM_B))
        vs.append(u3[:, :, 3 * W_A + 2 * W_B:3 * W_A + 3 * W_B].reshape(b, t, N_HEADS_B, HEAD_DIM_B))
        bas.append(nbuf_a)
        wc = p["conv_c_w"].shape[1]
        bcs.append(u3[:, t - (wc - 1):, 3 * W_A + 3 * W_B:3 * W_A + 3 * W_B + W_QKV_C])
        ss.append(s_new)
    return (x.reshape(b, t, d), jnp.stack(ks), jnp.stack(vs), jnp.stack(bas), jnp.stack(bcs), jnp.stack(ss))


def kernel(x_prompt, x_sample, cache_k, cache_v, page_table, state_conv_a, state_conv_c, state_delta, norm_mix, w_in, b_in, conv_a_w, conv_c_w, a_log, dt_bias, norm_c, w_br_a, w_br_b, w_br_c, w_out, norm_ffn, w_ffn_gate, w_ffn_up, w_ffn_down, w_router, w_exp_gate, w_exp_up, w_exp_down, norm_final):
    depth, d, n_in = w_in.shape
    assert n_in == N_MAIN + 2 * N_HEADS_C + N_GATES
    col_gate_src = N_MAIN + 2 * N_HEADS_C
    pad = N_PROJ - n_in

    def reorder(a):
        return jnp.concatenate(
            [a[..., :N_MAIN], a[..., col_gate_src:], a[..., N_MAIN:col_gate_src],
             jnp.zeros(a.shape[:-1] + (pad,), a.dtype)], axis=-1)

    p = {
        "norm_mix": norm_mix[:, None, :],
        "w_proj": reorder(w_in.astype(BF16)),
        "b_proj": reorder(b_in)[:, None, :],
        "conv_a_w": conv_a_w,
        "conv_c_w": conv_c_w,
        "gate_par": jnp.broadcast_to(jnp.stack([a_log, dt_bias], axis=-1)[..., None], (depth, N_HEADS_C, 2, LANES)),
        "norm_c": norm_c[:, None, :],
        "w_br_a": w_br_a.astype(BF16), "w_br_b": w_br_b.astype(BF16), "w_br_c": w_br_c.astype(BF16),
        "w_out": w_out.astype(BF16),
        "norm_ffn": norm_ffn[:, None, :],
        "norm_final": norm_final[None, :],
        "w_ffn_gate": w_ffn_gate.astype(BF16), "w_ffn_up": w_ffn_up.astype(BF16),
        "w_ffn_down": w_ffn_down.astype(BF16),
        "w_router": jnp.pad(w_router, ((0, 0), (0, 0), (0, LANES - w_router.shape[-1]))),
        "w_exp_gate": w_exp_gate.astype(BF16), "w_exp_up": w_exp_up.astype(BF16),
        "w_exp_down": w_exp_down.astype(BF16),
    }

    bp = x_prompt.shape[0]
    zero_a = jnp.zeros((depth, bp) + state_conv_a.shape[2:], F32)
    zero_c = jnp.zeros((depth, bp) + state_conv_c.shape[2:], F32)
    zero_s = jnp.zeros((depth, bp) + state_delta.shape[2:], F32)
    y_p, k_p, v_p, ca_p, cc_p, s_p = _trunk(
        x_prompt, lambda u3, l: _sb_prompt(u3, tq=512), zero_a, zero_c, zero_s, p)

    cache_kt = jnp.transpose(cache_k, (0, 1, 3, 4, 2))
    cache_vt = jnp.transpose(cache_v, (0, 1, 3, 4, 2))
    y_s, k_s, v_s, ca_s, cc_s, s_s = _trunk(
        x_sample, lambda u3, l: _sb_sample(u3, cache_kt, cache_vt, page_table, l, pps=32),
        state_conv_a, state_conv_c, state_delta, p)
    return (y_p, y_s, k_p, v_p, k_s, v_s, ca_p, ca_s, cc_p, cc_s, s_p, s_s)
```

```python
import functools

import jax
import jax.numpy as jnp
from jax import lax
from jax.experimental import pallas as pl
from jax.experimental.pallas import tpu as pltpu

F32 = jnp.float32
BF16 = jnp.bfloat16

RMS_EPS = 1e-6
L2_EPS = 1e-6

N_HEADS_B = 8
HEAD_DIM_B = 64
N_HEADS_C = 8
DK_C = 128
DV_C = 128
N_EXPERTS = 8
DELTA_CHUNK = 128
DELTA_CHUNK_SHORT = 64
MOE_ROWS = 128

LANES = 128
SUBLANES = 8
VMEM_LIMIT_BYTES = 56 * 1024 * 1024

W_A = 512
W_B = 512
W_QKV_C = 3072
W_Z_C = 1024
N_MAIN = 3 * W_A + 3 * W_B + W_QKV_C + W_Z_C
N_GATES = 3072
COL_GATES = N_MAIN
COL_BD = N_MAIN + N_GATES
N_PROJ = COL_BD + LANES


def _cparams(*sem):
    return pltpu.CompilerParams(dimension_semantics=sem, vmem_limit_bytes=VMEM_LIMIT_BYTES)


def _sigmoid(x):
    return 1.0 / (1.0 + jnp.exp(-x))


def _softplus(x):
    return jnp.maximum(x, 0.0) + jnp.log1p(jnp.exp(-jnp.abs(x)))


def _dot(a, b):
    return jnp.dot(a, b, preferred_element_type=F32)


def _dot_nt(a, b):
    return lax.dot_general(a, b, (((1,), (1,)), ((), ())), preferred_element_type=F32)


def _dot_tn(a, b):
    return lax.dot_general(a, b, (((0,), (0,)), ((), ())), preferred_element_type=F32)


def _dot_hi(a, b):
    return jnp.dot(a, b, preferred_element_type=F32, precision=lax.Precision.HIGHEST)


def _rms_scale(x):
    return x * lax.rsqrt(jnp.mean(x * x, axis=-1, keepdims=True) + RMS_EPS)


def _split_hi_lo(x):
    bits = lax.bitcast_convert_type(x, jnp.uint32) & jnp.uint32(0xFFFF0000)
    hi = lax.bitcast_convert_type(bits, F32)
    return hi.astype(BF16), (x - hi).astype(BF16)


def _norm_matmul_kernel(x_ref, g_ref, w_ref, b_ref, o_ref, xn_ref):
    @pl.when(pl.program_id(1) == 0)
    def _():
        xn_ref[...] = (_rms_scale(x_ref[...]) * g_ref[...]).astype(BF16)

    o_ref[...] = _dot(xn_ref[...], w_ref[...]) + b_ref[...]


def _norm_matmul(x, g, w, b, layer, *, tm, tn):
    m, d = x.shape
    n = w.shape[2]
    return pl.pallas_call(
        _norm_matmul_kernel,
        grid=(m // tm, n // tn),
        in_specs=[
            pl.BlockSpec((tm, d), lambda i, j: (i, 0)),
            pl.BlockSpec((None, 1, d), lambda i, j: (layer, 0, 0)),
            pl.BlockSpec((None, d, tn), lambda i, j: (layer, 0, j)),
            pl.BlockSpec((None, 1, tn), lambda i, j: (layer, 0, j)),
        ],
        out_specs=pl.BlockSpec((tm, tn), lambda i, j: (i, j)),
        out_shape=jax.ShapeDtypeStruct((m, n), F32),
        scratch_shapes=[pltpu.VMEM((tm, d), BF16)],
        compiler_params=_cparams("parallel", "arbitrary"),
        name="norm_in_proj",
    )(x, g, w, b)


def _mixer_a_kernel(h_ref, gb_ref, gc_ref, buf_ref, w_ref, act_ref, nbuf_ref, p_ref, *, tt, width):
    t = pl.program_id(1)
    lo = SUBLANES - (width - 1)

    @pl.when(t == 0)
    def _():
        p_ref[lo:SUBLANES, :] = buf_ref[...]

    @pl.when(t > 0)
    def _():
        p_ref[0:SUBLANES, :] = p_ref[tt:tt + SUBLANES, :]

    p_ref[SUBLANES:SUBLANES + tt, :] = gc_ref[...] * h_ref[...]
    y = p_ref[lo:lo + tt, :] * w_ref[0:1, :]
    for i in range(1, width):
        y = y + p_ref[lo + i:lo + i + tt, :] * w_ref[i:i + 1, :]
    act_ref[...] = (gb_ref[...] * y).astype(BF16)
    nbuf_ref[...] = p_ref[SUBLANES + tt - (width - 1):SUBLANES + tt, :]


def _mixer_a(u3, buf, w, *, tt):
    b, t, _ = u3.shape
    width, c = w.shape
    assert t % tt == 0 and t >= width - 1 and c == W_A
    col = lambda k: pl.BlockSpec((None, tt, c), lambda bi, ti: (bi, ti, k))
    return pl.pallas_call(
        functools.partial(_mixer_a_kernel, tt=tt, width=width),
        grid=(b, t // tt),
        in_specs=[
            col(0), col(1), col(2),
            pl.BlockSpec((None, width - 1, c), lambda bi, ti: (bi, 0, 0)),
            pl.BlockSpec((width, c), lambda bi, ti: (0, 0)),
        ],
        out_specs=[
            pl.BlockSpec((None, tt, c), lambda bi, ti: (bi, ti, 0)),
            pl.BlockSpec((None, width - 1, c), lambda bi, ti: (bi, 0, 0)),
        ],
        out_shape=[
            jax.ShapeDtypeStruct((b, t, c), BF16),
            jax.ShapeDtypeStruct((b, width - 1, c), F32),
        ],
        scratch_shapes=[pltpu.VMEM((tt + SUBLANES, c), F32)],
        compiler_params=_cparams("parallel", "arbitrary"),
        name="mixer_a_conv",
    )(u3, u3, u3, buf, w)


def _suffix_sum_matrix(n, passes):
    s = lax.broadcasted_iota(jnp.int32, (passes * n, n), 0)
    s = jnp.where(s >= n, s - n, s)
    j = lax.broadcasted_iota(jnp.int32, (passes * n, n), 1)
    return jnp.where(s >= j, 1.0, 0.0).astype(BF16)


def _sb_softplus_sums(z, tri, mask):
    sp = jnp.maximum(z, 0.0) + jnp.log(1.0 + jnp.exp(-jnp.abs(z)))
    if mask is not None:
        sp = jnp.where(mask, sp, 0.0)
    if tri.shape[0] == 2 * tri.shape[1]:
        addends = jnp.concatenate(_split_hi_lo(sp), axis=1)
    else:
        addends = sp.astype(BF16)
    return _dot(addends, tri), jnp.sum(sp, axis=-1, keepdims=True)


def _sb_weights(z, suffix, r_run, mask):
    a = jnp.exp(z - suffix - r_run)
    if mask is not None:
        a = jnp.where(mask, a, 0.0)
    return a.astype(BF16)


def _sb_prompt_kernel(q_ref, k_ref, v_ref, o_ref, kb_ref, vb_ref, acc_ref, r_ref, qh_ref, tri_ref,
                      za_ref, zb_ref, aa_ref, ab_ref, *, tq, scale):
    qi = pl.program_id(2)
    z_refs, a_refs = (za_ref, zb_ref), (aa_ref, ab_ref)
    kbn = LANES

    @pl.when(qi == 0)
    def _():
        kb_ref[...] = k_ref[...].astype(BF16)
        vb_ref[...] = v_ref[...].astype(BF16)

    lane = lax.broadcasted_iota(jnp.int32, (tq, LANES), 1)
    first = lane < HEAD_DIM_B
    q = q_ref[...] * scale
    qh_ref[0] = jnp.where(first, q, 0.0).astype(BF16)
    qh_ref[1] = jnp.where(first, 0.0, q).astype(BF16)
    tri_ref[...] = _suffix_sum_matrix(kbn, 1)
    row = lax.broadcasted_iota(jnp.int32, (tq, kbn), 0)
    col = lax.broadcasted_iota(jnp.int32, (tq, kbn), 1)
    acc_ref[...] = jnp.zeros(acc_ref.shape, F32)
    r_ref[...] = jnp.zeros(r_ref.shape, F32)

    def logits(kb2):
        k2 = kb_ref[pl.ds(pl.multiple_of(kb2 * 2 * kbn, 2 * kbn), 2 * kbn), :]
        pieces = []
        for h in range(2):
            z = _dot_nt(qh_ref[h], k2)
            pieces += [z[:, kbn:], z[:, :kbn]]
        return jnp.concatenate(pieces, axis=0)

    def weights(z, masks):
        mask = None if masks is None else jnp.concatenate(list(masks) * 2, axis=0)
        suffix, rs = _sb_softplus_sums(z, tri_ref[...], mask)
        r_parts = []
        for h in range(2):
            r_in = r_ref[h]
            r_mid = r_in + rs[2 * h * tq:(2 * h + 1) * tq]
            r_parts += [r_in, r_mid]
            r_ref[h] = r_mid + rs[(2 * h + 1) * tq:(2 * h + 2) * tq]
        return _sb_weights(z, suffix, jnp.concatenate(r_parts, axis=0), mask)

    def accumulate(a, kb2):
        v2 = vb_ref[pl.ds(pl.multiple_of(kb2 * 2 * kbn, 2 * kbn), 2 * kbn), :]
        for h in range(2):
            a_h = jnp.concatenate([a[(2 * h + 1) * tq:(2 * h + 2) * tq], a[2 * h * tq:(2 * h + 1) * tq]], axis=1)
            acc_ref[h] += _dot(a_h, v2)

    assert tq == 4 * kbn
    n_all = 2 * (qi + 1)
    blk = lambda s: jnp.maximum(n_all - 1 - s, 0)

    def step(s, half, masks, first_step=False):
        cur, nxt = half, 1 - half
        z_refs[nxt][...] = logits(blk(s + 1))
        if not first_step:
            accumulate(a_refs[nxt][...], blk(s - 1))
        a_refs[cur][...] = weights(z_refs[cur][...], masks)

    z_refs[0][...] = logits(blk(0))
    step(0, 0, (col + 3 * kbn < row, col + 2 * kbn < row), first_step=True)
    step(1, 1, (col + kbn < row, col < row))

    def trip(j, carry):
        for half in range(2):
            step(2 * j + 2 + half, half, None)
        return carry

    lax.fori_loop(0, qi, trip, 0)
    accumulate(a_refs[1][...], blk(n_all - 1))
    o_ref[...] = jnp.where(first, acc_ref[0], acc_ref[1]).astype(BF16)


def _sb_prompt(u3, *, tq):
    b, t, _ = u3.shape
    assert t % tq == 0 and tq % (2 * LANES) == 0
    pairs = W_B // LANES
    q_blk, k_blk, v_blk = (3 * W_A) // LANES, (3 * W_A + W_B) // LANES, (3 * W_A + 2 * W_B) // LANES
    return pl.pallas_call(
        functools.partial(_sb_prompt_kernel, tq=tq, scale=HEAD_DIM_B ** -0.5),
        grid=(b, pairs, t // tq),
        in_specs=[
            pl.BlockSpec((None, tq, LANES), lambda bi, hp, qi: (bi, qi, q_blk + hp)),
            pl.BlockSpec((None, t, LANES), lambda bi, hp, qi: (bi, 0, k_blk + hp)),
            pl.BlockSpec((None, t, LANES), lambda bi, hp, qi: (bi, 0, v_blk + hp)),
        ],
        out_specs=pl.BlockSpec((None, tq, LANES), lambda bi, hp, qi: (bi, qi, hp)),
        out_shape=jax.ShapeDtypeStruct((b, t, W_B), BF16),
        scratch_shapes=[
            pltpu.VMEM((t, LANES), BF16), pltpu.VMEM((t, LANES), BF16),
            pltpu.VMEM((2, tq, LANES), F32), pltpu.VMEM((2, tq, LANES), F32),
            pltpu.VMEM((2, tq, LANES), BF16), pltpu.VMEM((LANES, LANES), BF16),
            pltpu.VMEM((4 * tq, LANES), F32), pltpu.VMEM((4 * tq, LANES), F32),
            pltpu.VMEM((4 * tq, LANES), BF16), pltpu.VMEM((4 * tq, LANES), BF16),
        ],
        compiler_params=_cparams("parallel", "parallel", "arbitrary"),
        name="stickbreak_prompt",
    )(u3, u3, u3)


def _sb_sample_kernel(pt_ref, q_ref, ko_ref, vo_ref, *refs, n_new, page, pps, scale):
    del pt_ref
    k_refs, v_refs = refs[:pps], refs[pps:2 * pps]
    o_ref, qbd_ref, acc_ref, r_ref, own_ref = refs[2 * pps:]
    j = pl.program_id(1)
    nh, hd = N_HEADS_B, HEAD_DIM_B
    rows = nh * n_new
    tri2 = _suffix_sum_matrix(page, 2)

    @pl.when(j == 0)
    def _():
        qt = jnp.concatenate([q_ref[...] * scale] * nh, axis=0)
        rh = lax.broadcasted_iota(jnp.int32, (rows, W_B), 0) // n_new
        ch = lax.broadcasted_iota(jnp.int32, (rows, W_B), 1) // hd
        qbd_ref[...] = jnp.where(rh == ch, qt, 0.0).astype(BF16)
        r_ref[...] = jnp.zeros(r_ref.shape, F32)
        own_ref[...] = jnp.zeros(own_ref.shape, BF16)
        own_ref[0, 0:n_new, :] = ko_ref[...].astype(BF16)
        own_ref[1, 0:n_new, :] = vo_ref[...].astype(BF16)
        qpos = lax.broadcasted_iota(jnp.int32, (rows, page), 0) % n_new
        kpos = lax.broadcasted_iota(jnp.int32, (rows, page), 1)
        z = _dot_nt(qbd_ref[...], own_ref[0])
        suffix, rs = _sb_softplus_sums(z, tri2, kpos < qpos)
        r_ref[...] = jnp.broadcast_to(rs, r_ref.shape)
        acc_ref[...] = _dot(_sb_weights(z, suffix, 0.0, kpos < qpos), own_ref[1])

    def lanes(refs_):
        return jnp.concatenate([r[...].reshape(nh * hd, page).astype(BF16) for r in refs_], axis=1)

    z = _dot(qbd_ref[...], lanes(k_refs))
    z = jnp.concatenate([z[:, i * page:(i + 1) * page] for i in range(pps)], axis=0)
    suffix, rs = _sb_softplus_sums(z, tri2, None)
    r = r_ref[...]
    r_parts = []
    for i in range(pps):
        r_parts.append(r)
        r = r + rs[i * rows:(i + 1) * rows]
    r_ref[...] = r
    a = _sb_weights(z, suffix, jnp.concatenate(r_parts, axis=0), None)
    a = jnp.concatenate([a[i * rows:(i + 1) * rows] for i in range(pps)], axis=1)
    acc_ref[...] += _dot_nt(a, lanes(v_refs))

    @pl.when(j == pl.num_programs(1) - 1)
    def _():
        acc = acc_ref[...]
        ch = lax.broadcasted_iota(jnp.int32, (n_new, W_B), 1) // hd
        out = jnp.zeros((n_new, W_B), F32)
        for h in range(nh):
            out = out + jnp.where(ch == h, acc[h * n_new:(h + 1) * n_new, :], 0.0)
        o_ref[...] = out.astype(BF16)


def _sb_sample(u3, cache_kt, cache_vt, page_table, layer, *, pps):
    db, n_new, _ = u3.shape
    _, _, nh, hd, page = cache_kt.shape
    n_pages = page_table.shape[1]
    assert nh == N_HEADS_B and hd == HEAD_DIM_B and n_new % SUBLANES == 0 and n_pages % pps == 0
    blk = (3 * W_A) // W_B

    def page_spec(i):
        return pl.BlockSpec((None, None, nh, hd, page),
                            lambda bi, j, pt: (layer, pt[bi, n_pages - 1 - (j * pps + i)], 0, 0, 0))

    new_spec = lambda k: pl.BlockSpec((None, n_new, W_B), lambda bi, j, pt: (bi, 0, blk + k))
    grid_spec = pltpu.PrefetchScalarGridSpec(
        num_scalar_prefetch=1,
        grid=(db, n_pages // pps),
        in_specs=[new_spec(0), new_spec(1), new_spec(2)] + [page_spec(i) for i in range(pps)] * 2,
        out_specs=pl.BlockSpec((None, n_new, W_B), lambda bi, j, pt: (bi, 0, 0)),
        scratch_shapes=[
            pltpu.VMEM((nh * n_new, W_B), BF16),
            pltpu.VMEM((nh * n_new, W_B), F32),
            pltpu.VMEM((nh * n_new, LANES), F32),
            pltpu.VMEM((2, page, W_B), BF16),
        ],
    )
    return pl.pallas_call(
        functools.partial(_sb_sample_kernel, n_new=n_new, page=page, pps=pps, scale=hd ** -0.5),
        grid_spec=grid_spec,
        out_shape=jax.ShapeDtypeStruct((db, n_new, W_B), BF16),
        compiler_params=_cparams("parallel", "arbitrary"),
        name="stickbreak_sample",
    )(page_table, u3, u3, u3, *([cache_kt] * pps), *([cache_vt] * pps))


def _dot3(a_split, b_split):
    a_hi, a_lo = a_split
    b_hi, b_lo = b_split
    return _dot(jnp.concatenate([a_hi, a_hi, a_lo, a_lo], axis=1),
                jnp.concatenate([b_hi, b_lo, b_hi, b_lo], axis=0))


def _delta_kernel(q_ref, k_ref, v_ref, z_ref, bd_ref, buf_ref, cw_ref, gp_ref, nc_ref, s0_ref,
                  y_ref, s_ref, p_ref, *, tin, tt, chunk, width, hps):
    hg = pl.program_id(1)
    t = pl.program_id(2)
    lo = SUBLANES - (width - 1)
    dk = DK_C

    @pl.when(t == 0)
    def _():
        if tin < tt:
            p_ref[...] = jnp.zeros(p_ref.shape, F32)
        for hh in range(hps):
            for i in range(3):
                p_ref[hh, i, lo:SUBLANES, :] = buf_ref[hh, i]
        s_ref[...] = s0_ref[...]

    @pl.when(t > 0)
    def _():
        for hh in range(hps):
            for i in range(3):
                p_ref[hh, i, 0:SUBLANES, :] = p_ref[hh, i, tt:tt + SUBLANES, :]

    lane = lax.broadcasted_iota(jnp.int32, (tin, LANES), 1)
    bd = bd_ref[...]

    def head_inputs(hh):
        cols = slice(hh * LANES, (hh + 1) * LANES)
        conv = []
        for i, ref in enumerate((q_ref, k_ref, v_ref)):
            p_ref[hh, i, SUBLANES:SUBLANES + tin, :] = ref[:, cols]
            y = p_ref[hh, i, lo:lo + tt, :] * cw_ref[i, hh, 0:1, :]
            for w in range(1, width):
                y = y + p_ref[hh, i, lo + w:lo + w + tt, :] * cw_ref[i, hh, w:w + 1, :]
            conv.append(y * _sigmoid(y))
        qc, kc, vc = conv
        qn = qc * lax.rsqrt(jnp.sum(qc * qc, axis=-1, keepdims=True) + L2_EPS) * (dk ** -0.5)
        kn = kc * lax.rsqrt(jnp.sum(kc * kc, axis=-1, keepdims=True) + L2_EPS)
        h = hg * hps + hh
        neg_a = -jnp.exp(gp_ref[hh, 0:1, 0:1])
        dt_b = gp_ref[hh, 1:2, 0:1]
        b_col = jnp.sum(jnp.where(lane == h, bd, 0.0), axis=-1, keepdims=True)
        a_col = jnp.sum(jnp.where(lane == h + N_HEADS_C, bd, 0.0), axis=-1, keepdims=True)
        beta_col = _sigmoid(b_col)
        g_col = neg_a * _softplus(a_col + dt_b)
        if tin < tt:
            pad = jnp.zeros((tt - tin, 1), F32)
            beta_col = jnp.concatenate([beta_col, pad], axis=0)
            g_col = jnp.concatenate([g_col, pad], axis=0)
        return qn, kn, vc, g_col, beta_col

    ri = lax.broadcasted_iota(jnp.int32, (chunk, chunk), 0)
    ci = lax.broadcasted_iota(jnp.int32, (chunk, chunk), 1)
    incl = ri >= ci
    strict = ri > ci
    eye = jnp.where(ri == ci, 1.0, 0.0)
    n_doubling = (min(tin, chunk) - 1).bit_length() - 1

    def prepare(inputs, c):
        sl = slice(c * chunk, (c + 1) * chunk)
        qk, kk_, vk, gc_col, bc = (a[sl] for a in inputs)
        g_lanes = jnp.transpose(jnp.broadcast_to(gc_col, (chunk, chunk)))
        gcum_col = jnp.sum(jnp.where(incl, g_lanes, 0.0), axis=1, keepdims=True)
        gcum_row = jnp.sum(jnp.where(ri <= ci, gc_col, 0.0), axis=0, keepdims=True)
        dec_incl = jnp.where(incl, jnp.exp(jnp.where(incl, gcum_col - gcum_row, 0.0)), 0.0)
        k_bf = kk_.astype(BF16)
        e_col = jnp.exp(gcum_col)
        g_last = gcum_col[chunk - 1:chunk, :]
        return dict(
            m=bc * _dot_nt(k_bf, k_bf) * jnp.where(strict, dec_incl, 0.0),
            rhs=_split_hi_lo(jnp.concatenate([kk_ * (bc * e_col), vk * bc], axis=-1)),
            aqk=(_dot_nt(qk.astype(BF16), k_bf) * dec_incl).astype(BF16),
            q_dec=qk * e_col,
            k_dec=(kk_ * jnp.exp(g_last - gcum_col)).astype(BF16),
            g_end=jnp.exp(g_last))

    n_chunks = tt // chunk
    pre = []
    for hh in range(hps):
        inputs = head_inputs(hh)
        pre += [prepare(inputs, c) for c in range(n_chunks)]
    pw_s = [_split_hi_lo(-p["m"]) for p in pre]
    inv = [eye - p["m"] for p in pre]
    for _ in range(n_doubling):
        pw_s = [_split_hi_lo(_dot3(s_, s_)) for s_ in pw_s]
        inv = [iv + _dot3(_split_hi_lo(iv), s_) for iv, s_ in zip(inv, pw_s)]
    sols = [_dot3(_split_hi_lo(iv), p["rhs"]) for iv, p in zip(inv, pre)]

    steps = []
    for p, sol in zip(pre, sols):
        w_bf, uv_bf = sol[:, :dk].astype(BF16), sol[:, dk:].astype(BF16)
        steps.append(dict(
            s_mix=_dot_tn(p["k_dec"], w_bf).astype(BF16), s_add=_dot_tn(p["k_dec"], uv_bf),
            o_mix=(p["q_dec"] - _dot(p["aqk"], w_bf)).astype(BF16), o_add=_dot(p["aqk"], uv_bf),
            g_end=p["g_end"]))

    outs = [[] for _ in range(hps)]
    for c in range(n_chunks):
        for hh in range(hps):
            st = steps[hh * n_chunks + c]
            s = s_ref[hh]
            s_bf = s.astype(BF16)
            outs[hh].append(_dot(st["o_mix"], s_bf) + st["o_add"])
            s_ref[hh] = st["g_end"] * s + st["s_add"] - _dot(st["s_mix"], s_bf)
    for hh in range(hps):
        cols = slice(hh * LANES, (hh + 1) * LANES)
        o = (jnp.concatenate(outs[hh], axis=0) if n_chunks > 1 else outs[hh][0])[:tin]
        zg = z_ref[:, cols]
        y_ref[:, cols] = (_rms_scale(o) * nc_ref[...] * (zg * _sigmoid(zg))).astype(BF16)


def _delta(u3, buf, conv_w, gate_par, norm_c, s0, layer, *, tin, tt, chunk, hps):
    b, t, _ = u3.shape
    width = conv_w.shape[0]
    nh = N_HEADS_C
    assert t % tin == 0 and tt % chunk == 0 and (tin == tt or t == tin) and tin >= width - 1 and nh % hps == 0
    qb = (3 * W_A + 3 * W_B) // LANES
    zb = (3 * W_A + 3 * W_B + W_QKV_C) // LANES
    bdb = COL_BD // LANES
    assert qb % hps == 0 and zb % hps == 0
    cw3 = conv_w.reshape(width, 3, nh, LANES).transpose(1, 2, 0, 3)
    buf3 = buf.reshape(b, width - 1, 3, nh, LANES).transpose(0, 3, 2, 1, 4)
    col = lambda k: pl.BlockSpec((None, tin, hps * LANES), lambda bi, hi, ti: (bi, ti, k // hps + hi))
    return pl.pallas_call(
        functools.partial(_delta_kernel, tin=tin, tt=tt, chunk=chunk, width=width, hps=hps),
        grid=(b, nh // hps, t // tin),
        in_specs=[
            col(qb), col(qb + nh), col(qb + 2 * nh), col(zb),
            pl.BlockSpec((None, tin, LANES), lambda bi, hi, ti: (bi, ti, bdb)),
            pl.BlockSpec((None, hps, 3, width - 1, LANES), lambda bi, hi, ti: (bi, hi, 0, 0, 0)),
            pl.BlockSpec((3, hps, width, LANES), lambda bi, hi, ti: (0, hi, 0, 0)),
            pl.BlockSpec((hps, 2, LANES), lambda bi, hi, ti: (hi, 0, 0)),
            pl.BlockSpec((1, DV_C), lambda bi, hi, ti: (0, 0)),
            pl.BlockSpec((None, None, hps, DK_C, DV_C), lambda bi, hi, ti: (layer, bi, hi, 0, 0)),
        ],
        out_specs=[
            pl.BlockSpec((None, tin, hps * LANES), lambda bi, hi, ti: (bi, ti, hi)),
            pl.BlockSpec((None, hps, DK_C, DV_C), lambda bi, hi, ti: (bi, hi, 0, 0)),
        ],
        out_shape=[
            jax.ShapeDtypeStruct((b, t, nh * DV_C), BF16),
            jax.ShapeDtypeStruct((b, nh, DK_C, DV_C), F32),
        ],
        scratch_shapes=[pltpu.VMEM((hps, 3, tt + SUBLANES, LANES), F32)],
        compiler_params=_cparams("parallel", "parallel", "arbitrary"),
        name="gated_delta",
    )(u3, u3, u3, u3, u3, buf3, cw3, gate_par, norm_c, s0)


def _merge_kernel(x_ref, a_ref, b_ref, c_ref, g0_ref, g1_ref, g2_ref, wa_ref, wb_ref, wc_ref, wo_ref, o_ref):
    merged = (_sigmoid(g0_ref[...]) * _dot(a_ref[...], wa_ref[...])
              + _sigmoid(g1_ref[...]) * _dot(b_ref[...], wb_ref[...])
              + _sigmoid(g2_ref[...]) * _dot(c_ref[...], wc_ref[...]))
    o_ref[...] = x_ref[...] + _dot(merged.astype(BF16), wo_ref[...])


def _merge(x, act_a, act_b, act_c, u, wa, wb, wc, wo, *, tm):
    m, d = x.shape
    gb = COL_GATES // d
    row = lambda width, k=0: pl.BlockSpec((tm, width), lambda i: (i, k))
    full = lambda w: pl.BlockSpec(w.shape, lambda i: (0, 0))
    return pl.pallas_call(
        _merge_kernel,
        grid=(m // tm,),
        in_specs=[
            row(d), row(act_a.shape[1]), row(act_b.shape[1]), row(act_c.shape[1]),
            row(d, gb), row(d, gb + 1), row(d, gb + 2),
            full(wa), full(wb), full(wc), full(wo),
        ],
        out_specs=row(d),
        out_shape=jax.ShapeDtypeStruct((m, d), F32),
        compiler_params=_cparams("parallel"),
        name="merge_out_proj",
    )(x, act_a, act_b, act_c, u, u, u, wa, wb, wc, wo)


def _finish(x_ref, total, gf_ref, o_ref, final_norm):
    y = x_ref[...] + total
    if final_norm:
        y = _rms_scale(y) * gf_ref[...]
    o_ref[...] = y


def _ffn_kernel(x_ref, g_ref, gf_ref, wg_ref, wu_ref, wd_ref, o_ref, hn_ref, acc_ref, *, final_norm):
    f = pl.program_id(1)

    @pl.when(f == 0)
    def _():
        hn_ref[...] = (_rms_scale(x_ref[...]) * g_ref[...]).astype(BF16)
        acc_ref[...] = jnp.zeros(acc_ref.shape, F32)

    hn = hn_ref[...]
    a = _dot(hn, wg_ref[...])
    hidden = (a * _sigmoid(a) * _dot(hn, wu_ref[...])).astype(BF16)
    acc_ref[...] += _dot(hidden, wd_ref[...])

    @pl.when(f == pl.num_programs(1) - 1)
    def _():
        _finish(x_ref, acc_ref[...], gf_ref, o_ref, final_norm)


def _ffn(x, g, gf, wg, wu, wd, *, tm, tf, final_norm):
    m, d = x.shape
    ff = wg.shape[1]
    return pl.pallas_call(
        functools.partial(_ffn_kernel, final_norm=final_norm),
        grid=(m // tm, ff // tf),
        in_specs=[
            pl.BlockSpec((tm, d), lambda i, f: (i, 0)),
            pl.BlockSpec((1, d), lambda i, f: (0, 0)),
            pl.BlockSpec((1, d), lambda i, f: (0, 0)),
            pl.BlockSpec((d, tf), lambda i, f: (0, f)),
            pl.BlockSpec((d, tf), lambda i, f: (0, f)),
            pl.BlockSpec((tf, d), lambda i, f: (f, 0)),
        ],
        out_specs=pl.BlockSpec((tm, d), lambda i, f: (i, 0)),
        out_shape=jax.ShapeDtypeStruct((m, d), F32),
        scratch_shapes=[pltpu.VMEM((tm, d), BF16), pltpu.VMEM((tm, d), F32)],
        compiler_params=_cparams("parallel", "arbitrary"),
        name="dense_swiglu",
    )(x, g, gf, wg, wu, wd)


def _top2_gates(logits):
    lane = lax.broadcasted_iota(jnp.int32, logits.shape, 1).astype(F32)
    neg = -jnp.inf
    lg = jnp.where(lane < N_EXPERTS, logits, neg)
    m1 = jnp.max(lg, axis=-1, keepdims=True)
    i1 = jnp.min(jnp.where(lg == m1, lane, float(LANES)), axis=-1, keepdims=True)
    lg2 = jnp.where(lane == i1, neg, lg)
    m2 = jnp.max(lg2, axis=-1, keepdims=True)
    i2 = jnp.min(jnp.where(lg2 == m2, lane, float(LANES)), axis=-1, keepdims=True)
    e2 = jnp.exp(m2 - m1)
    w1 = 1.0 / (1.0 + e2)
    gate = jnp.where(lane == i1, w1, jnp.where(lane == i2, e2 * w1, 0.0))
    return gate, jnp.where(lane == i1, 1.0, jnp.where(lane == i2, 1.0, 0.0))


GATE, RANK, SEL, TABLE_ROWS = 0, N_EXPERTS, 2 * N_EXPERTS, 4 * N_EXPERTS


def _router_kernel(x_ref, g_ref, wr_ref, hn_ref, col_ref, row_ref, cnt_ref):
    tm = x_ref.shape[0]
    hn = _rms_scale(x_ref[...]) * g_ref[...]
    hn_ref[...] = hn.astype(BF16)
    gate, sel = _top2_gates(_dot_hi(hn, wr_ref[...]))
    gate_t = jnp.transpose(gate)[0:N_EXPERTS]
    sel_t = jnp.transpose(sel)[0:N_EXPERTS]
    earlier = jnp.where(lax.broadcasted_iota(jnp.int32, (tm, tm), 0) < lax.broadcasted_iota(jnp.int32, (tm, tm), 1),
                        1.0, 0.0).astype(BF16)
    rank_t = _dot(sel_t.astype(BF16), earlier)
    table = jnp.concatenate([gate_t, rank_t, sel_t, jnp.zeros((LANES - 3 * N_EXPERTS, tm), F32)], axis=0)
    row_ref[...] = table[0:TABLE_ROWS]
    col_ref[...] = jnp.transpose(table)
    cnt_ref[...] = jnp.broadcast_to(jnp.sum(sel, axis=0, keepdims=True), cnt_ref.shape).astype(jnp.int32)


def _router(x, g, w_router, *, tm):
    m, d = x.shape
    nt = m // tm
    return pl.pallas_call(
        _router_kernel,
        grid=(nt,),
        in_specs=[
            pl.BlockSpec((tm, d), lambda i: (i, 0)),
            pl.BlockSpec((1, d), lambda i: (0, 0)),
            pl.BlockSpec((d, LANES), lambda i: (0, 0)),
        ],
        out_specs=[
            pl.BlockSpec((tm, d), lambda i: (i, 0)),
            pl.BlockSpec((tm, LANES), lambda i: (i, 0)),
            pl.BlockSpec((None, TABLE_ROWS, tm), lambda i: (i, 0, 0)),
            pl.BlockSpec((None, SUBLANES, LANES), lambda i: (i, 0, 0)),
        ],
        out_shape=[
            jax.ShapeDtypeStruct((m, d), BF16),
            jax.ShapeDtypeStruct((m, LANES), F32),
            jax.ShapeDtypeStruct((nt, TABLE_ROWS, tm), F32),
            jax.ShapeDtypeStruct((nt, SUBLANES, LANES), jnp.int32),
        ],
        compiler_params=_cparams("parallel"),
        name="moe_router",
    )(x, g, w_router)


def _experts_kernel(cnt_ref, x_ref, hn_ref, col_ref, row_ref, gf_ref, wg_ref, wu_ref, wd_ref, o_ref,
                    hc_ref, yacc_ref, tot_ref, ecol_ref, *, sub_blocks, final_norm):
    i = pl.program_id(0)
    e = pl.program_id(1)
    f = pl.program_id(2)
    last_f = pl.num_programs(2) - 1
    tm = x_ref.shape[0]
    count = cnt_ref[i, e]

    @pl.when((e == 0) & (f == 0))
    def _():
        tot_ref[...] = jnp.zeros(tot_ref.shape, F32)

    @pl.when(f == last_f)
    def _():
        lane = lax.broadcasted_iota(jnp.int32, (tm, LANES), 1)
        col = col_ref[...]
        for n, k in enumerate((GATE, RANK, SEL)):
            ecol_ref[n] = jnp.broadcast_to(
                jnp.sum(jnp.where(lane == k + e, col, 0.0), axis=-1, keepdims=True), (tm, LANES))

    for start, rows in sub_blocks:
        @pl.when(start < count)
        def _(start=start, rows=rows):
            span = slice(start, start + rows)

            @pl.when(f == 0)
            def _():
                rank_row = row_ref[pl.ds(RANK + e, 1), :]
                sel_row = row_ref[pl.ds(SEL + e, 1), :]
                slot = (lax.broadcasted_iota(jnp.int32, (rows, tm), 0) + start).astype(F32)
                pick = jnp.where(rank_row == slot, sel_row, 0.0).astype(BF16)
                hc_ref[span, :] = _dot(pick, hn_ref[...]).astype(BF16)
                yacc_ref[span, :] = jnp.zeros((rows, yacc_ref.shape[1]), F32)

            hc = hc_ref[span, :]
            a = _dot(hc, wg_ref[...])
            hidden = (a * _sigmoid(a) * _dot(hc, wu_ref[...])).astype(BF16)
            yacc_ref[span, :] += _dot(hidden, wd_ref[...])

            @pl.when(f == last_f)
            def _():
                slot = (lax.broadcasted_iota(jnp.int32, (tm, rows), 1) + start).astype(F32)
                place = jnp.where(ecol_ref[1][:, 0:1] == slot, ecol_ref[2][:, 0:1], 0.0).astype(BF16)
                tot_ref[...] += ecol_ref[0][:, 0:1] * _dot(place, yacc_ref[span, :].astype(BF16))

    @pl.when((e == pl.num_programs(1) - 1) & (f == last_f))
    def _():
        _finish(x_ref, tot_ref[...], gf_ref, o_ref, final_norm)


def _moe(x, g, gf, w_router, wg, wu, wd, *, tm, tf, final_norm):
    m, d = x.shape
    ne, _, ff = wg.shape
    expected = (2 * tm) // ne
    sizes = [expected] if expected > MOE_ROWS else []
    sizes += [MOE_ROWS] * ((tm - sum(sizes)) // MOE_ROWS)
    assert sum(sizes) == tm and all(r % 16 == 0 for r in sizes)
    sub_blocks = tuple((sum(sizes[:n]), r) for n, r in enumerate(sizes))
    hn, col, row, cnt = _router(x, g, w_router, tm=tm)
    grid_spec = pltpu.PrefetchScalarGridSpec(
        num_scalar_prefetch=1,
        grid=(m // tm, ne, ff // tf),
        in_specs=[
            pl.BlockSpec((tm, d), lambda i, e, f, c: (i, 0)),
            pl.BlockSpec((tm, d), lambda i, e, f, c: (i, 0)),
            pl.BlockSpec((tm, LANES), lambda i, e, f, c: (i, 0)),
            pl.BlockSpec((None, TABLE_ROWS, tm), lambda i, e, f, c: (i, 0, 0)),
            pl.BlockSpec((1, d), lambda i, e, f, c: (0, 0)),
            pl.BlockSpec((None, d, tf), lambda i, e, f, c: (e, 0, f)),
            pl.BlockSpec((None, d, tf), lambda i, e, f, c: (e, 0, f)),
            pl.BlockSpec((None, tf, d), lambda i, e, f, c: (e, f, 0)),
        ],
        out_specs=pl.BlockSpec((tm, d), lambda i, e, f, c: (i, 0)),
        scratch_shapes=[
            pltpu.VMEM((tm, d), BF16), pltpu.VMEM((tm, d), F32),
            pltpu.VMEM((tm, d), F32), pltpu.VMEM((3, tm, LANES), F32),
        ],
    )
    return pl.pallas_call(
        functools.partial(_experts_kernel, sub_blocks=sub_blocks, final_norm=final_norm),
        grid_spec=grid_spec,
        out_shape=jax.ShapeDtypeStruct((m, d), F32),
        compiler_params=_cparams("parallel", "arbitrary", "arbitrary"),
        name="moe_experts",
    )(cnt[:, 0, :ne], x, hn, col, row, gf, wg, wu, wd)


def _tile(n, pref):
    return pref if n % pref == 0 else n


def _trunk(x3, attend, bufs_a, bufs_c, states, p):
    b, t, d = x3.shape
    m = b * t
    depth = p["w_proj"].shape[0]
    x = x3.reshape(m, d)
    tm = _tile(m, 1024)
    ks, vs, bas, bcs, ss = [], [], [], [], []
    for l in range(depth):
        u = _norm_matmul(x, p["norm_mix"], p["w_proj"], p["b_proj"], l, tm=_tile(m, 2048), tn=1152)
        u3 = u.reshape(b, t, N_PROJ)
        act_a, nbuf_a = _mixer_a(u3, bufs_a[l], p["conv_a_w"][l], tt=_tile(t, 512))
        act_b = attend(u3, l)
        tin = min(t, 512)
        chunk = DELTA_CHUNK if tin >= DELTA_CHUNK else DELTA_CHUNK_SHORT
        act_c, s_new = _delta(u3, bufs_c[l], p["conv_c_w"][l], p["gate_par"][l], p["norm_c"][l], states, l,
                              tin=tin, tt=max(tin, chunk), chunk=chunk,
                              hps=2 if tin >= chunk else N_HEADS_C)
        act_c = act_c.reshape(m, -1)
        x = _merge(x, act_a.reshape(m, -1), act_b.reshape(m, -1), act_c, u,
                   p["w_br_a"][l], p["w_br_b"][l], p["w_br_c"][l], p["w_out"][l], tm=_tile(m, 512))
        final = l == depth - 1
        if l % 2 == 0:
            x = _ffn(x, p["norm_ffn"][l], p["norm_final"], p["w_ffn_gate"][l // 2], p["w_ffn_up"][l // 2],
                     p["w_ffn_down"][l // 2], tm=tm, tf=1408, final_norm=final)
        else:
            x = _moe(x, p["norm_ffn"][l], p["norm_final"], p["w_router"][l // 2], p["w_exp_gate"][l // 2],
                     p["w_exp_up"][l // 2], p["w_exp_down"][l // 2], tm=tm, tf=896, final_norm=final)
        ks.append(u3[:, :, 3 * W_A + W_B:3 * W_A + 2 * W_B].reshape(b, t, N_HEADS_B, HEAD_DIM_B))
        vs.append(u3[:, :, 3 * W_A + 2 * W_B:3 * W_A + 3 * W_B].reshape(b, t, N_HEADS_B, HEAD_DIM_B))
        bas.append(nbuf_a)
        wc = p["conv_c_w"].shape[1]
        bcs.append(u3[:, t - (wc - 1):, 3 * W_A + 3 * W_B:3 * W_A + 3 * W_B + W_QKV_C])
        ss.append(s_new)
    return (x.reshape(b, t, d), jnp.stack(ks), jnp.stack(vs), jnp.stack(bas), jnp.stack(bcs), jnp.stack(ss))


def kernel(x_prompt, x_sample, cache_k, cache_v, page_table, state_conv_a, state_conv_c, state_delta, norm_mix, w_in, b_in, conv_a_w, conv_c_w, a_log, dt_bias, norm_c, w_br_a, w_br_b, w_br_c, w_out, norm_ffn, w_ffn_gate, w_ffn_up, w_ffn_down, w_router, w_exp_gate, w_exp_up, w_exp_down, norm_final):
    depth, d, n_in = w_in.shape
    assert n_in == N_MAIN + 2 * N_HEADS_C + N_GATES
    col_gate_src = N_MAIN + 2 * N_HEADS_C
    pad = N_PROJ - n_in

    def reorder(a):
        return jnp.concatenate(
            [a[..., :N_MAIN], a[..., col_gate_src:], a[..., N_MAIN:col_gate_src],
             jnp.zeros(a.shape[:-1] + (pad,), a.dtype)], axis=-1)

    p = {
        "norm_mix": norm_mix[:, None, :],
        "w_proj": reorder(w_in.astype(BF16)),
        "b_proj": reorder(b_in)[:, None, :],
        "conv_a_w": conv_a_w,
        "conv_c_w": conv_c_w,
        "gate_par": jnp.broadcast_to(jnp.stack([a_log, dt_bias], axis=-1)[..., None], (depth, N_HEADS_C, 2, LANES)),
        "norm_c": norm_c[:, None, :],
        "w_br_a": w_br_a.astype(BF16), "w_br_b": w_br_b.astype(BF16), "w_br_c": w_br_c.astype(BF16),
        "w_out": w_out.astype(BF16),
        "norm_ffn": norm_ffn[:, None, :],
        "norm_final": norm_final[None, :],
        "w_ffn_gate": w_ffn_gate.astype(BF16), "w_ffn_up": w_ffn_up.astype(BF16),
        "w_ffn_down": w_ffn_down.astype(BF16),
        "w_router": jnp.pad(w_router, ((0, 0), (0, 0), (0, LANES - w_router.shape[-1]))),
        "w_exp_gate": w_exp_gate.astype(BF16), "w_exp_up": w_exp_up.astype(BF16),
        "w_exp_down": w_exp_down.astype(BF16),
    }

    bp = x_prompt.shape[0]
    zero_a = jnp.zeros((depth, bp) + state_conv_a.shape[2:], F32)
    zero_c = jnp.zeros((depth, bp) + state_conv_c.shape[2:], F32)
    zero_s = jnp.zeros((depth, bp) + state_delta.shape[2:], F32)
    y_p, k_p, v_p, ca_p, cc_p, s_p = _trunk(
        x_prompt, lambda u3, l: _sb_prompt(u3, tq=512), zero_a, zero_c, zero_s, p)

    cache_kt = jnp.transpose(cache_k, (0, 1, 3, 4, 2))
    cache_vt = jnp.transpose(cache_v, (0, 1, 3, 4, 2))
    y_s, k_s, v_s, ca_s, cc_s, s_s = _trunk(
        x_sample, lambda u3, l: _sb_sample(u3, cache_kt, cache_vt, page_table, l, pps=32),
        state_conv_a, state_conv_c, state_delta, p)
    return (y_p, y_s, k_p, v_p, k_s, v_s, ca_p, ca_s, cc_p, cc_s, s_p, s_s)
```
